```python
import jax
import jax.numpy as jnp
from jax import lax
import numpy as np

D_MODEL = 1024
BATCH = 8
SEQ = 2048
DEPTH = 4
DEC_BATCH = 128
DEC_SEQ = 4
PAST_LEN = 2048
PAGE_SIZE = 128

HEAD_DIM = 64
NSA_HEADS = 8
NSA_KV_HEADS = 2
NSA_GROUP = NSA_HEADS // NSA_KV_HEADS
NSA_BLOCK = 64
NSA_TOP_K = 16
NSA_WINDOW = 512
FOX_HEADS = 4
CONV_CH = 256
CONV_WIDTH = 31
D_FF = 2816
ROPE_THETA = 10000.0
RMS_EPS = 1e-6
LN_EPS = 1e-5
Q_BLOCK = 128
NEG_INF = -1e30
POS_INF = 1e30
TINY = 1e-30
ADA_INIT = 0.5

NSA_Q = NSA_HEADS * HEAD_DIM
NSA_KV = NSA_KV_HEADS * HEAD_DIM
FOX_W = FOX_HEADS * HEAD_DIM
IN_SIZES = (NSA_Q, 2 * NSA_KV, 2 * NSA_KV, 2 * NSA_KV, 3 * NSA_HEADS, 2 * CONV_CH, 3 * FOX_W, FOX_HEADS, 3 * D_MODEL)
D_IN = NSA_Q + 6 * NSA_KV + 3 * NSA_HEADS + 2 * CONV_CH + 3 * FOX_W + FOX_HEADS + 3 * D_MODEL

kernel_name = 'hybrid_nsa_conformer_fox_decode_step'


def rms_norm(x, g):
    xf = x.astype(jnp.float32)
    y = xf * lax.rsqrt(jnp.mean(xf * xf, axis=-1, keepdims=True) + RMS_EPS)
    return (y * g.astype(jnp.float32)).astype(x.dtype)


def layer_norm(x, g, b):
    xf = x.astype(jnp.float32)
    xc = xf - jnp.mean(xf, axis=-1, keepdims=True)
    var = jnp.mean(xc * xc, axis=-1, keepdims=True)
    y = xc * lax.rsqrt(var + LN_EPS) * g.astype(jnp.float32) + b.astype(jnp.float32)
    return y.astype(x.dtype)


def modulate(h, shift, scale):
    return h * (1 + scale) + shift


def swiglu(h, w_gu, w_down):
    gate, up = jnp.split(h @ w_gu, 2, axis=-1)
    return (jax.nn.silu(gate) * up) @ w_down


def rope(x, pos):
    half = HEAD_DIM // 2
    inv_freq = ROPE_THETA ** (-jnp.arange(half, dtype=jnp.float32) / half)
    ang = pos.astype(jnp.float32)[:, None] * inv_freq[None, :]
    cos = jnp.cos(ang)[None, :, None, :]
    sin = jnp.sin(ang)[None, :, None, :]
    xf = x.astype(jnp.float32)
    x1, x2 = xf[..., :half], xf[..., half:]
    return jnp.concatenate([x1 * cos - x2 * sin, x2 * cos + x1 * sin], axis=-1).astype(x.dtype)


def rope_kv(kv, pos):
    return jnp.stack([rope(kv[:, :, 0], pos), kv[:, :, 1]], axis=2)


def masked_softmax(logits, mask):
    l = jnp.where(mask, logits, NEG_INF)
    l = l - jnp.max(l, axis=-1, keepdims=True)
    p = jnp.where(mask, jnp.exp(l), 0.0)
    return p / jnp.maximum(jnp.sum(p, axis=-1, keepdims=True), TINY)


def split_cols(z, sizes):
    parts, off = [], 0
    for s in sizes:
        parts.append(z[..., off:off + s])
        off += s
    return parts


def gather_pages(pool, page_table):
    rows = pool[page_table]
    return rows.reshape((rows.shape[0], rows.shape[1] * rows.shape[2]) + rows.shape[3:])


def q_blocking(tq):
    qb = Q_BLOCK if tq % Q_BLOCK == 0 else tq
    return qb, tq // qb


def nsa_compress(a, w, pe):
    B, Tk = a.shape[0], a.shape[1]
    nc = Tk // NSA_BLOCK
    blocks = a[:, :nc * NSA_BLOCK].reshape(B, nc, NSA_BLOCK, NSA_KV_HEADS, HEAD_DIM)
    blocks = blocks + pe[None, None, :, None, :].astype(a.dtype)
    return jnp.einsum('bnlgd,lde->bnge', blocks, w)


def nsa_attend(q, gates, k_cmp, v_cmp, k_slc, v_slc, k_win, v_win, q_pos0):
    B, Tq = q.shape[0], q.shape[1]
    Tk = k_slc.shape[1]
    nc = k_cmp.shape[1]
    nb = -(-Tk // NSA_BLOCK)
    k_sel = min(NSA_TOP_K, nb)
    qb, nqb = q_blocking(Tq)
    scale = HEAD_DIM ** -0.5

    def to_blocks(a):
        a = jnp.pad(a, ((0, 0), (0, nb * NSA_BLOCK - Tk), (0, 0), (0, 0)))
        a = a.reshape(B, nb, NSA_BLOCK, NSA_KV_HEADS, HEAD_DIM).transpose(0, 3, 1, 2, 4)
        return a.reshape(B, NSA_KV_HEADS, nb, NSA_BLOCK * HEAD_DIM)

    k_blk, v_blk = to_blocks(k_slc), to_blocks(v_slc)
    gather_blocks = jax.vmap(jax.vmap(lambda a, i: a[i]))
    cmp_end = (jnp.arange(nc) + 1) * NSA_BLOCK - 1
    blk_id = jnp.arange(nb)[None, None, None, :]
    in_blk = jnp.arange(NSA_BLOCK)
    win_off = jnp.arange(NSA_WINDOW + qb)

    def one_block(ib):
        i0 = ib * qb
        t = q_pos0 + i0 + jnp.arange(qb)
        qg = lax.dynamic_slice_in_dim(q, i0, qb, axis=1).reshape(B, qb, NSA_KV_HEADS, NSA_GROUP, HEAD_DIM)
        lc = jnp.einsum('bqgrd,bngd->bqgrn', qg, k_cmp, preferred_element_type=jnp.float32) * scale
        pc = masked_softmax(lc, (cmp_end[None, :] <= t[:, None])[None, :, None, None, :])
        o_cmp = jnp.einsum('bqgrn,bngd->bqgrd', pc.astype(v_cmp.dtype), v_cmp)
        score = jnp.pad(pc.sum(axis=3), ((0, 0), (0, 0), (0, 0), (0, nb - nc)))
        cur = (t // NSA_BLOCK)[None, :, None, None]
        forced = (blk_id == 0) | (blk_id == cur) | (blk_id == cur - 1)
        score = jnp.where(forced, POS_INF, jnp.where(blk_id > cur, NEG_INF, score))
        _, idx = lax.top_k(score, k_sel)
        idx_g = idx.transpose(0, 2, 1, 3).reshape(B, NSA_KV_HEADS, qb * k_sel)
        kg = gather_blocks(k_blk, idx_g).reshape(B, NSA_KV_HEADS, qb, k_sel, NSA_BLOCK, HEAD_DIM)
        vg = gather_blocks(v_blk, idx_g).reshape(B, NSA_KV_HEADS, qb, k_sel, NSA_BLOCK, HEAD_DIM)
        ls = jnp.einsum('bqgrd,bgqkld->bqgrkl', qg, kg, preferred_element_type=jnp.float32) * scale
        kpos = idx[..., None] * NSA_BLOCK + in_blk
        ms = (kpos <= t[None, :, None, None, None]).reshape(B, qb, NSA_KV_HEADS, 1, k_sel * NSA_BLOCK)
        ps = masked_softmax(ls.reshape(B, qb, NSA_KV_HEADS, NSA_GROUP, k_sel * NSA_BLOCK), ms)
        ps = ps.reshape(B, qb, NSA_KV_HEADS, NSA_GROUP, k_sel, NSA_BLOCK)
        o_slc = jnp.einsum('bqgrkl,bgqkld->bqgrd', ps.astype(vg.dtype), vg)
        kw = lax.dynamic_slice_in_dim(k_win, i0, NSA_WINDOW + qb, axis=1)
        vw = lax.dynamic_slice_in_dim(v_win, i0, NSA_WINDOW + qb, axis=1)
        wpos = q_pos0 + i0 - NSA_WINDOW + win_off
        mw = (wpos[None, :] <= t[:, None]) & (wpos[None, :] > t[:, None] - NSA_WINDOW) & (wpos[None, :] >= 0)
        lw = jnp.einsum('bqgrd,bkgd->bqgrk', qg, kw, preferred_element_type=jnp.float32) * scale
        pw = masked_softmax(lw, mw[None, :, None, None, :])
        o_win = jnp.einsum('bqgrk,bkgd->bqgrd', pw.astype(vw.dtype), vw)
        g = lax.dynamic_slice_in_dim(gates, i0, qb, axis=1).reshape(B, qb, NSA_KV_HEADS, NSA_GROUP, 3)
        o = g[..., 0:1] * o_cmp + g[..., 1:2] * o_slc + g[..., 2:3] * o_win
        return o.reshape(B, qb, NSA_Q)

    out = lax.map(one_block, jnp.arange(nqb))
    return out.transpose(1, 0, 2, 3).reshape(B, Tq, NSA_Q)


def fox_attend(q, k, v, cum, q_pos0):
    B, Tq = q.shape[0], q.shape[1]
    Tk = k.shape[1]
    qb, nqb = q_blocking(Tq)
    scale = HEAD_DIM ** -0.5
    cum_t = jnp.transpose(cum, (0, 2, 1))
    kpos = jnp.arange(Tk)

    def one_block(ib):
        i0 = ib * qb
        t = q_pos0 + i0 + jnp.arange(qb)
        qq = lax.dynamic_slice_in_dim(q, i0, qb, axis=1)
        cq = lax.dynamic_slice_in_dim(cum_t, q_pos0 + i0, qb, axis=2)
        logits = jnp.einsum('bqhd,bkhd->bhqk', qq, k, preferred_element_type=jnp.float32) * scale
        logits = logits + (cq[..., :, None] - cum_t[..., None, :])
        p = masked_softmax(logits, (kpos[None, :] <= t[:, None])[None, None])
        o = jnp.einsum('bhqk,bkhd->bqhd', p.astype(v.dtype), v)
        return o.reshape(B, qb, FOX_W)

    out = lax.map(one_block, jnp.arange(nqb))
    return out.transpose(1, 0, 2, 3).reshape(B, Tq, FOX_W)


def causal_depthwise_conv(ctx, w, b):
    out = lax.conv_general_dilated(ctx, w[:, None, :].astype(ctx.dtype), window_strides=(1,), padding='VALID',
                                   dimension_numbers=('NWC', 'WIO', 'NWC'), feature_group_count=CONV_CH)
    return out + b


def trunk_layer(x, cond, lw, pos0, past):
    (w_ada, b_ada, norm_g, w_ffn_gu, w_ffn_down, w_in, b_in, cmp_pe, cmp_wk, cmp_wv,
     dw_w, dw_b, ln_g, ln_b, wb_nsa, wb_conv, wb_fox, w_out) = lw
    B, T = x.shape[0], x.shape[1]
    mod = jax.nn.silu(cond) @ w_ada + b_ada
    sh1, sc1, ga1, sh2, sc2, ga2, sh3, sc3, ga3 = jnp.split(mod[:, None, :], 9, axis=-1)

    h = modulate(rms_norm(x, norm_g[0]), sh1, sc1)
    x = x + 0.5 * ga1 * swiglu(h, w_ffn_gu[0], w_ffn_down[0])

    u = modulate(rms_norm(x, norm_g[1]), sh2, sc2)
    z = u @ w_in + b_in
    q_n, kv_c, kv_s, kv_w, g_n, glu_in, qkv_f, f_logit, merge_g = split_cols(z, IN_SIZES)
    pos = pos0 + jnp.arange(T, dtype=jnp.int32)
    q_n = rope(q_n.reshape(B, T, NSA_HEADS, HEAD_DIM), pos)
    kv_c_new = rope_kv(kv_c.reshape(B, T, 2, NSA_KV_HEADS, HEAD_DIM), pos)
    kv_s_new = rope_kv(kv_s.reshape(B, T, 2, NSA_KV_HEADS, HEAD_DIM), pos)
    kv_w_new = rope_kv(kv_w.reshape(B, T, 2, NSA_KV_HEADS, HEAD_DIM), pos)
    glu_a, glu_b = jnp.split(glu_in, 2, axis=-1)
    glu = glu_a * jax.nn.sigmoid(glu_b)
    qkv_f = qkv_f.reshape(B, T, 3, FOX_HEADS, HEAD_DIM)
    q_f, kv_f_new = qkv_f[:, :, 0], qkv_f[:, :, 1:]
    log_f = jax.nn.log_sigmoid(f_logit.astype(jnp.float32))

    if past is None:
        kc_full, ks_full, kvf_full, logf_full = kv_c_new, kv_s_new, kv_f_new, log_f
        kw_ext = jnp.pad(kv_w_new, ((0, 0), (NSA_WINDOW, 0), (0, 0), (0, 0), (0, 0)))
        win_state = kv_w_new[:, T - min(NSA_WINDOW, T):]
        conv_ctx = jnp.pad(glu, ((0, 0), (CONV_WIDTH - 1, 0), (0, 0)))
    else:
        p_c, p_s, p_f, p_lf, p_win, p_conv = past
        kc_full = jnp.concatenate([p_c, kv_c_new], axis=1)
        ks_full = jnp.concatenate([p_s, kv_s_new], axis=1)
        kvf_full = jnp.concatenate([p_f, kv_f_new], axis=1)
        logf_full = jnp.concatenate([p_lf.astype(jnp.float32), log_f], axis=1)
        w_buf = p_win.shape[1]
        p_win_pad = jnp.pad(p_win, ((0, 0), (NSA_WINDOW - w_buf, 0), (0, 0), (0, 0), (0, 0)))
        kw_ext = jnp.concatenate([p_win_pad, kv_w_new], axis=1)
        win_state = jnp.concatenate([p_win, kv_w_new], axis=1)[:, T:]
        conv_ctx = jnp.concatenate([p_conv, glu], axis=1)
    conv_state = conv_ctx[:, conv_ctx.shape[1] - (CONV_WIDTH - 1):]

    k_cmp = nsa_compress(kc_full[:, :, 0], cmp_wk, cmp_pe)
    v_cmp = nsa_compress(kc_full[:, :, 1], cmp_wv, cmp_pe)
    nsa_gates = jax.nn.sigmoid(g_n).reshape(B, T, NSA_HEADS, 3)
    o_nsa = nsa_attend(q_n, nsa_gates, k_cmp, v_cmp, ks_full[:, :, 0], ks_full[:, :, 1],
                       kw_ext[:, :, 0], kw_ext[:, :, 1], pos0)
    y_conv = jax.nn.silu(layer_norm(causal_depthwise_conv(conv_ctx, dw_w, dw_b), ln_g, ln_b))
    cum = jnp.cumsum(logf_full, axis=1)
    o_fox = fox_attend(q_f, kvf_full[:, :, 0], kvf_full[:, :, 1], cum, pos0)

    g_a, g_b, g_c = jnp.split(merge_g, 3, axis=-1)
    merged = (jax.nn.sigmoid(g_a) * (o_nsa @ wb_nsa) + jax.nn.sigmoid(g_b) * (y_conv @ wb_conv)
              + jax.nn.sigmoid(g_c) * (o_fox @ wb_fox))
    x = x + ga2 * (merged @ w_out)

    h = modulate(rms_norm(x, norm_g[2]), sh3, sc3)
    x = x + 0.5 * ga3 * swiglu(h, w_ffn_gu[1], w_ffn_down[1])
    return x, (kv_c_new, kv_s_new, kv_f_new, log_f, win_state, conv_state)


def setup_inputs(seed: int = 0) -> dict:
    key = jax.random.key(seed)
    ks = jax.random.split(key, 40)
    f32 = jnp.float32
    n_pages = PAST_LEN // PAGE_SIZE
    n_used = DEC_BATCH * n_pages
    n_pool = n_used + max(1, n_used // 4)
    w_buf = min(NSA_WINDOW, PAST_LEN)

    def nrm(k, shape, scale=1.0):
        return jax.random.normal(k, shape, f32) * scale

    page_table = jax.random.permutation(ks[0], n_pool)[:n_used].reshape(DEC_BATCH, n_pages).astype(jnp.int32)
    return {
        'x_prompt': nrm(ks[1], (BATCH, SEQ, D_MODEL)),
        'x_sample': nrm(ks[2], (DEC_BATCH, DEC_SEQ, D_MODEL)),
        'cache_nsa_cmp_kv': nrm(ks[3], (DEPTH, n_pool, PAGE_SIZE, 2, NSA_KV_HEADS, HEAD_DIM)),
        'cache_nsa_slc_kv': nrm(ks[4], (DEPTH, n_pool, PAGE_SIZE, 2, NSA_KV_HEADS, HEAD_DIM)),
        'cache_fox_kv': nrm(ks[5], (DEPTH, n_pool, PAGE_SIZE, 2, FOX_HEADS, HEAD_DIM)),
        'cache_fox_logf': jax.nn.log_sigmoid(nrm(ks[6], (DEPTH, n_pool, PAGE_SIZE, FOX_HEADS)) + 2.0),
        'state_nsa_win_kv': nrm(ks[7], (DEPTH, DEC_BATCH, w_buf, 2, NSA_KV_HEADS, HEAD_DIM)),
        'state_conv': nrm(ks[8], (DEPTH, DEC_BATCH, CONV_WIDTH - 1, CONV_CH), 0.5),
        'page_table': page_table,
        'c_prompt': nrm(ks[9], (BATCH, D_MODEL)),
        'c_sample': nrm(ks[10], (DEC_BATCH, D_MODEL)),
        'w_ada': nrm(ks[11], (DEPTH, D_MODEL, 9 * D_MODEL), ADA_INIT * D_MODEL ** -0.5),
        'b_ada': nrm(ks[12], (DEPTH, 9 * D_MODEL), 0.02),
        'norm_g': 1.0 + nrm(ks[13], (DEPTH, 3, D_MODEL), 0.05),
        'w_ffn_gu': nrm(ks[14], (DEPTH, 2, D_MODEL, 2 * D_FF), D_MODEL ** -0.5),
        'w_ffn_down': nrm(ks[15], (DEPTH, 2, D_FF, D_MODEL), D_FF ** -0.5),
        'w_in': nrm(ks[16], (DEPTH, D_MODEL, D_IN), D_MODEL ** -0.5),
        'b_in': nrm(ks[17], (DEPTH, D_IN), 0.02),
        'nsa_cmp_pe': nrm(ks[18], (DEPTH, NSA_BLOCK, HEAD_DIM), 0.1),
        'nsa_cmp_wk': nrm(ks[19], (DEPTH, NSA_BLOCK, HEAD_DIM, HEAD_DIM), (NSA_BLOCK * HEAD_DIM) ** -0.5),
        'nsa_cmp_wv': nrm(ks[20], (DEPTH, NSA_BLOCK, HEAD_DIM, HEAD_DIM), (NSA_BLOCK * HEAD_DIM) ** -0.5),
        'conv_dw_w': nrm(ks[21], (DEPTH, CONV_WIDTH, CONV_CH), CONV_WIDTH ** -0.5),
        'conv_dw_b': nrm(ks[22], (DEPTH, CONV_CH), 0.02),
        'conv_ln_g': 1.0 + nrm(ks[23], (DEPTH, CONV_CH), 0.05),
        'conv_ln_b': nrm(ks[24], (DEPTH, CONV_CH), 0.02),
        'w_branch_nsa': nrm(ks[25], (DEPTH, NSA_Q, D_MODEL), NSA_Q ** -0.5),
        'w_branch_conv': nrm(ks[26], (DEPTH, CONV_CH, D_MODEL), CONV_CH ** -0.5),
        'w_branch_fox': nrm(ks[27], (DEPTH, FOX_W, D_MODEL), FOX_W ** -0.5),
        'w_out': nrm(ks[28], (DEPTH, D_MODEL, D_MODEL), D_MODEL ** -0.5),
        'final_norm_g': 1.0 + nrm(ks[29], (D_MODEL,), 0.05),
    }


def reference(x_prompt, x_sample, cache_nsa_cmp_kv, cache_nsa_slc_kv, cache_fox_kv, cache_fox_logf,
              state_nsa_win_kv, state_conv, page_table, c_prompt, c_sample, w_ada, b_ada, norm_g,
              w_ffn_gu, w_ffn_down, w_in, b_in, nsa_cmp_pe, nsa_cmp_wk, nsa_cmp_wv, conv_dw_w, conv_dw_b,
              conv_ln_g, conv_ln_b, w_branch_nsa, w_branch_conv, w_branch_fox, w_out, final_norm_g):
    past_len = page_table.shape[1] * cache_nsa_cmp_kv.shape[2]
    xp, xs = x_prompt, x_sample
    rows_p, rows_s = [], []
    for l in range(DEPTH):
        lw = (w_ada[l], b_ada[l], norm_g[l], w_ffn_gu[l], w_ffn_down[l], w_in[l], b_in[l],
              nsa_cmp_pe[l], nsa_cmp_wk[l], nsa_cmp_wv[l], conv_dw_w[l], conv_dw_b[l],
              conv_ln_g[l], conv_ln_b[l], w_branch_nsa[l], w_branch_conv[l], w_branch_fox[l], w_out[l])
        xp, st_p = trunk_layer(xp, c_prompt, lw, 0, None)
        past = (gather_pages(cache_nsa_cmp_kv[l], page_table), gather_pages(cache_nsa_slc_kv[l], page_table),
                gather_pages(cache_fox_kv[l], page_table), gather_pages(cache_fox_logf[l], page_table),
                state_nsa_win_kv[l], state_conv[l])
        xs, st_s = trunk_layer(xs, c_sample, lw, past_len, past)
        rows_p.append(st_p)
        rows_s.append(st_s)

    def stack_field(rows, i):
        return jnp.stack([r[i] for r in rows])

    y_prompt = rms_norm(xp, final_norm_g)
    y_sample = rms_norm(xs, final_norm_g)
    return (y_prompt, y_sample,
            stack_field(rows_p, 0), stack_field(rows_s, 0),
            stack_field(rows_p, 1), stack_field(rows_s, 1),
            stack_field(rows_p, 2), stack_field(rows_s, 2),
            stack_field(rows_p, 3), stack_field(rows_s, 3),
            stack_field(rows_p, 4), stack_field(rows_s, 4),
            stack_field(rows_p, 5), stack_field(rows_s, 5))
```

```python
import functools

import jax
import jax.numpy as jnp
from jax import lax
from jax.experimental import pallas as pl
from jax.experimental.pallas import tpu as pltpu

D_MODEL = 1024
HEAD_DIM = 64
NSA_HEADS = 8
NSA_KV_HEADS = 2
NSA_GROUP = NSA_HEADS // NSA_KV_HEADS
NSA_BLOCK = 64
NSA_TOP_K = 16
NSA_WINDOW = 512
FOX_HEADS = 4
CONV_CH = 256
CONV_WIDTH = 31
D_FF = 2816
ROPE_THETA = 10000.0
RMS_EPS = 1e-6
LN_EPS = 1e-5
NEG_INF = -1e30
POS_INF = 1e30
TINY = 1e-30
MASKED_BELOW = -5e29
SCALE = HEAD_DIM ** -0.5

NSA_Q = NSA_HEADS * HEAD_DIM
NSA_KV = NSA_KV_HEADS * HEAD_DIM
FOX_W = FOX_HEADS * HEAD_DIM
KV_ROW = 2 * NSA_KV
FOX_ROW = 2 * FOX_W

LANE = 128
C_Q = 0
C_KVC = C_Q + NSA_Q
C_KVS = C_KVC + KV_ROW
C_KVW = C_KVS + KV_ROW
C_GLU = C_KVW + KV_ROW
C_QF = C_GLU + 2 * CONV_CH
C_KVF = C_QF + FOX_W
C_MG = C_KVF + FOX_ROW
C_GN = C_MG + 3 * D_MODEL
C_LF = C_GN + LANE
D_IN_PAD = C_LF + LANE

TQ_PAD = 8
VMEM_LIMIT = 56 * 1024 * 1024

F32 = jnp.float32
BF16 = jnp.bfloat16
NT = (((1,), (1,)), ((), ()))


def _params(*sem):
    return pltpu.CompilerParams(dimension_semantics=sem, vmem_limit_bytes=VMEM_LIMIT)


def _dot(a, b):
    return jnp.dot(a, b, preferred_element_type=F32)


def _dot_nt(a, b):
    return lax.dot_general(a, b, NT, preferred_element_type=F32)


def _rms_mod(x, g, shift, scale):
    y = x * lax.rsqrt(jnp.mean(x * x, axis=-1, keepdims=True) + RMS_EPS) * g
    return y * (1.0 + scale) + shift


def _sigmoid(x):
    return 1.0 / (1.0 + jnp.exp(-x))


def _silu(x):
    return x * _sigmoid(x)


def _ada_kernel(c_ref, w_ref, b_ref, o_ref):
    c = _silu(c_ref[...]).astype(BF16)
    o_ref[...] = _dot(c, w_ref[...]) + b_ref[...]


def ada_all_layers(cond, w_ada, b_ada):
    depth, _, n = w_ada.shape
    rows = cond.shape[0]
    tn = n // 8
    return pl.pallas_call(
        _ada_kernel,
        grid=(depth, n // tn),
        in_specs=[pl.BlockSpec((rows, D_MODEL), lambda l, j: (0, 0)),
                  pl.BlockSpec((None, D_MODEL, tn), lambda l, j: (l, 0, j)),
                  pl.BlockSpec((None, 1, tn), lambda l, j: (l, 0, j))],
        out_specs=pl.BlockSpec((None, rows, tn), lambda l, j: (l, 0, j)),
        out_shape=jax.ShapeDtypeStruct((depth, rows, n), F32),
        compiler_params=_params("parallel", "parallel"),
        name="ada",
    )(cond, w_ada, b_ada.reshape(depth, 1, n))


class Mod:
    def __init__(self, arr, tokens_per_cond):
        self.arr = arr
        self.tokens_per_cond = tokens_per_cond

    def spec(self, k, tm):
        if self.arr.ndim == 3:
            per = self.tokens_per_cond // tm
            return pl.BlockSpec((None, 1, D_MODEL), lambda m, *_: (m // per, 0, k))
        return pl.BlockSpec((tm, D_MODEL), lambda m, *_: (m, k))


def _ffn_kernel(x_ref, g_ref, sh_ref, sc_ref, ga_ref, wg_ref, wu_ref, wd_ref, o_ref, h_scr, acc_scr):
    j = pl.program_id(1)

    @pl.when(j == 0)
    def _():
        h_scr[...] = _rms_mod(x_ref[...], g_ref[...], sh_ref[...], sc_ref[...]).astype(BF16)
        acc_scr[...] = jnp.zeros_like(acc_scr)

    h = h_scr[...]
    gate = _dot(h, wg_ref[...])
    up = _dot(h, wu_ref[...])
    a = (_silu(gate) * up).astype(BF16)
    acc_scr[...] += _dot(a, wd_ref[...])

    @pl.when(j == pl.num_programs(1) - 1)
    def _():
        o_ref[...] = x_ref[...] + 0.5 * ga_ref[...] * acc_scr[...]


def ffn_half_step(x, mod, k0, norm_g, w_gu, w_down, layer, idx, tm):
    m = x.shape[0]
    nf = 2
    tf = D_FF // nf
    return pl.pallas_call(
        _ffn_kernel,
        grid=(m // tm, nf),
        in_specs=[pl.BlockSpec((tm, D_MODEL), lambda i, j: (i, 0)),
                  pl.BlockSpec((None, None, 1, D_MODEL), lambda i, j: (layer, 2 * idx, 0, 0)),
                  mod.spec(k0, tm), mod.spec(k0 + 1, tm), mod.spec(k0 + 2, tm),
                  pl.BlockSpec((None, None, D_MODEL, tf), lambda i, j: (layer, idx, 0, j)),
                  pl.BlockSpec((None, None, D_MODEL, tf), lambda i, j: (layer, idx, 0, nf + j)),
                  pl.BlockSpec((None, None, tf, D_MODEL), lambda i, j: (layer, idx, j, 0))],
        out_specs=pl.BlockSpec((tm, D_MODEL), lambda i, j: (i, 0)),
        out_shape=jax.ShapeDtypeStruct((m, D_MODEL), F32),
        scratch_shapes=[pltpu.VMEM((tm, D_MODEL), BF16), pltpu.VMEM((tm, D_MODEL), F32)],
        compiler_params=_params("parallel", "arbitrary"),
        name="ffn",
    )(x, norm_g, mod.arr, mod.arr, mod.arr, w_gu, w_gu, w_down)


def _rope(z, cos, sin):
    n = z.shape[1]
    lane = lax.broadcasted_iota(jnp.int32, z.shape, 1)
    first_half = (lane % HEAD_DIM) < (HEAD_DIM // 2)
    partner = jnp.where(first_half, pltpu.roll(z, n - HEAD_DIM // 2, 1), pltpu.roll(z, HEAD_DIM // 2, 1))
    reps = n // LANE
    if reps > 1:
        cos = jnp.concatenate([cos] * reps, axis=1)
        sin = jnp.concatenate([sin] * reps, axis=1)
    return z * cos + partner * sin


def _inproj_kernel(x_ref, g_ref, sh_ref, sc_ref, cos_ref, sin_ref, w_ref, b_ref,
                   q_ref, kvc_ref, kvs_ref, kvw_ref, glu_ref, qf_ref, kvf_ref, mg_ref, gn_ref, lf_ref):
    u = _rms_mod(x_ref[...], g_ref[...], sh_ref[...], sc_ref[...]).astype(BF16)
    cos = cos_ref[...]
    sin = sin_ref[...]

    def proj(lo, hi):
        return _dot(u, w_ref[:, lo:hi]) + b_ref[:, lo:hi]

    q_ref[...] = _rope(proj(C_Q, C_KVC), cos, sin)
    for ref, lo in ((kvc_ref, C_KVC), (kvs_ref, C_KVS), (kvw_ref, C_KVW)):
        z = proj(lo, lo + KV_ROW)
        ref[:, :NSA_KV] = _rope(z[:, :NSA_KV], cos, sin)
        ref[:, NSA_KV:] = z[:, NSA_KV:]
    z = proj(C_GLU, C_QF)
    glu_ref[...] = z[:, :CONV_CH] * _sigmoid(z[:, CONV_CH:])
    qf_ref[...] = proj(C_QF, C_KVF)
    kvf_ref[...] = proj(C_KVF, C_MG)
    mg_ref[...] = _sigmoid(proj(C_MG, C_GN))
    gn_ref[...] = _sigmoid(proj(C_GN, C_LF))
    z = proj(C_LF, D_IN_PAD)
    lf_ref[...] = jnp.minimum(z, 0.0) - jnp.log(1.0 + jnp.exp(-jnp.abs(z)))


def input_projection(x, mod, norm_g, w_in, b_in, cos, sin, layer, tm, pos_tiles):
    m = x.shape[0]
    widths = (NSA_Q, KV_ROW, KV_ROW, KV_ROW, CONV_CH, FOX_W, FOX_ROW, 3 * D_MODEL, LANE, LANE)
    return pl.pallas_call(
        _inproj_kernel,
        grid=(m // tm,),
        in_specs=[pl.BlockSpec((tm, D_MODEL), lambda i: (i, 0)),
                  pl.BlockSpec((None, None, 1, D_MODEL), lambda i: (layer, 1, 0, 0)),
                  mod.spec(3, tm), mod.spec(4, tm),
                  pl.BlockSpec((tm, LANE), lambda i: (i % pos_tiles, 0)),
                  pl.BlockSpec((tm, LANE), lambda i: (i % pos_tiles, 0)),
                  pl.BlockSpec((None, D_MODEL, D_IN_PAD), lambda i: (layer, 0, 0)),
                  pl.BlockSpec((None, 1, D_IN_PAD), lambda i: (layer, 0, 0))],
        out_specs=[pl.BlockSpec((tm, w), lambda i: (i, 0)) for w in widths],
        out_shape=[jax.ShapeDtypeStruct((m, w), F32) for w in widths],
        compiler_params=_params("parallel"),
        name="inproj",
    )(x, norm_g, mod.arr, mod.arr, cos, sin, w_in, b_in)


def _compress_kernel(x_ref, pe_ref, wk_ref, wv_ref, o_ref):
    nblk = o_ref.shape[0]
    acc_k = jnp.zeros((nblk, NSA_KV), F32)
    acc_v = jnp.zeros((nblk, NSA_KV), F32)
    for l in range(NSA_BLOCK):
        pe = pe_ref[l:l + 1, :]
        k_rows = x_ref[pl.ds(2 * l, nblk, stride=2 * NSA_BLOCK), :] + pe
        v_rows = x_ref[pl.ds(2 * l + 1, nblk, stride=2 * NSA_BLOCK), :] + pe
        acc_k = acc_k + _dot(k_rows.astype(BF16), wk_ref[l])
        acc_v = acc_v + _dot(v_rows.astype(BF16), wv_ref[l])
    o_ref[:, :NSA_KV] = acc_k
    o_ref[:, NSA_KV:] = acc_v


def nsa_compress_rows(rows, pe2, wk_bd, wv_bd, layer, blocks_per_step):
    lead = rows.shape[:-2]
    nblk = rows.shape[-2] // (2 * NSA_BLOCK)
    nb = max(d for d in range(8, blocks_per_step + 1, 8) if nblk % d == 0)
    if lead:
        x_spec = pl.BlockSpec((None, nb * 2 * NSA_BLOCK, NSA_KV), lambda i: (layer, i, 0))
    else:
        x_spec = pl.BlockSpec((nb * 2 * NSA_BLOCK, NSA_KV), lambda i: (i, 0))
    w_spec = pl.BlockSpec((None, NSA_BLOCK, NSA_KV, NSA_KV), lambda i: (layer, 0, 0, 0))
    return pl.pallas_call(
        _compress_kernel,
        grid=(nblk // nb,),
        in_specs=[x_spec, pl.BlockSpec((None, NSA_BLOCK, NSA_KV), lambda i: (layer, 0, 0)), w_spec, w_spec],
        out_specs=pl.BlockSpec((nb, KV_ROW), lambda i: (i, 0)),
        out_shape=jax.ShapeDtypeStruct((nblk, KV_ROW), F32),
        compiler_params=_params("parallel"),
        name="nsa_compress",
    )(rows, pe2, wk_bd, wv_bd)


def _softmax_rows(logits):
    m = jnp.max(logits, axis=-1, keepdims=True)
    p = jnp.where(logits > MASKED_BELOW, jnp.exp(logits - m), 0.0)
    return p / jnp.maximum(jnp.sum(p, axis=-1, keepdims=True), TINY)


def _online_step(carry, s, v):
    m, l, acc = carry
    m_new = jnp.maximum(m, jnp.max(s, axis=-1, keepdims=True))
    alpha = jnp.exp(m - m_new)
    p = jnp.where(s > MASKED_BELOW, jnp.exp(s - m_new), 0.0)
    l = alpha * l + jnp.sum(p, axis=-1, keepdims=True)
    acc = alpha * acc + _dot(p.astype(BF16), v)
    return m_new, l, acc


def _online_init(rows, d):
    return (jnp.full((rows, 1), NEG_INF, F32), jnp.zeros((rows, 1), F32), jnp.zeros((rows, d), F32))


def _online_finish(carry):
    _, l, acc = carry
    return acc / jnp.maximum(l, TINY)


def _select_blocks(score, t, nb):
    blk = lax.broadcasted_iota(jnp.int32, score.shape, 1)
    cur = t // NSA_BLOCK
    forced = jnp.where(blk == 0, 1, jnp.where(blk == cur, 1, jnp.where(blk == cur - 1, 1, 0)))
    s = jnp.where(forced == 1, POS_INF, jnp.where(blk > cur, NEG_INF, score))
    rank = jnp.zeros(score.shape, jnp.int32)
    for i in range(nb):
        col = s[:, i:i + 1]
        ahead = jnp.where(col > s, 1, jnp.where(col == s, jnp.where(blk > i, 1, 0), 0))
        rank = rank + ahead
    k_sel = min(NSA_TOP_K, nb)
    return jnp.where(rank < k_sel, jnp.where(blk < nb, 1.0, 0.0), 0.0)


def _expand_blocks(sel, k0, kc):
    width = sel.shape[1]
    kblk = (k0 + lax.broadcasted_iota(jnp.int32, (width, kc), 1)) // NSA_BLOCK
    e = jnp.where(kblk == lax.broadcasted_iota(jnp.int32, (width, kc), 0), 1.0, 0.0).astype(BF16)
    return _dot(sel.astype(BF16), e)


def _nsa_prompt_kernel(q_ref, gate_ref, kcmp_ref, kvs_ref, kvw_ref, o_ref, *, tq, kc_slc, kc_win):
    i = pl.program_id(1)
    t0 = i * tq
    t = t0 + lax.broadcasted_iota(jnp.int32, (tq, 1), 0)
    t4 = jnp.concatenate([t] * NSA_GROUP, axis=0)
    rows = NSA_GROUP * tq
    nc = kcmp_ref.shape[0]
    gates = gate_ref[...]

    def tile_rows(a):
        return jnp.concatenate([a] * NSA_GROUP, axis=0)

    for g in range(NSA_KV_HEADS):
        k_lo, v_lo = g * HEAD_DIM, NSA_KV + g * HEAD_DIM
        q4 = jnp.concatenate(
            [q_ref[:, (g * NSA_GROUP + r) * HEAD_DIM:(g * NSA_GROUP + r + 1) * HEAD_DIM] for r in range(NSA_GROUP)],
            axis=0)
        q4 = (q4 * SCALE).astype(BF16)

        kcm = kcmp_ref[:, k_lo:k_lo + HEAD_DIM].astype(BF16)
        vcm = kcmp_ref[:, v_lo:v_lo + HEAD_DIM].astype(BF16)
        blk = lax.broadcasted_iota(jnp.int32, (1, nc), 1)
        lc = _dot_nt(q4, kcm) + jnp.where((blk + 1) * NSA_BLOCK - 1 <= t4, 0.0, NEG_INF)
        pc = _softmax_rows(lc)
        o_cmp = _dot(pc.astype(BF16), vcm)
        score = pc[0:tq] + pc[tq:2 * tq] + pc[2 * tq:3 * tq] + pc[3 * tq:4 * tq]
        sel = _select_blocks(score, t, nc)

        def slc_body(j, carry):
            k0 = pl.multiple_of(j * kc_slc, kc_slc)
            kpos = k0 + lax.broadcasted_iota(jnp.int32, (1, kc_slc), 1)
            ok = jnp.where(kpos <= t, _expand_blocks(sel, k0, kc_slc), 0.0)
            bias = tile_rows(jnp.where(ok > 0.5, 0.0, NEG_INF))
            kb = kvs_ref[pl.ds(k0, kc_slc), k_lo:k_lo + HEAD_DIM].astype(BF16)
            vb = kvs_ref[pl.ds(k0, kc_slc), v_lo:v_lo + HEAD_DIM].astype(BF16)
            return _online_step(carry, _dot_nt(q4, kb) + bias, vb)

        n_slc = (t0 + tq + kc_slc - 1) // kc_slc
        o_slc = _online_finish(lax.fori_loop(0, n_slc, slc_body, _online_init(rows, HEAD_DIM)))

        def win_body(j, carry):
            k0 = pl.multiple_of(j * kc_win, kc_win)
            kpos = k0 + lax.broadcasted_iota(jnp.int32, (1, kc_win), 1)
            ok = jnp.where(kpos <= t, jnp.where(kpos > t - NSA_WINDOW, 1.0, 0.0), 0.0)
            bias = tile_rows(jnp.where(ok > 0.5, 0.0, NEG_INF))
            kb = kvw_ref[pl.ds(k0, kc_win), k_lo:k_lo + HEAD_DIM].astype(BF16)
            vb = kvw_ref[pl.ds(k0, kc_win), v_lo:v_lo + HEAD_DIM].astype(BF16)
            return _online_step(carry, _dot_nt(q4, kb) + bias, vb)

        j_lo = jnp.maximum(t0 - NSA_WINDOW, 0) // kc_win
        j_hi = (t0 + tq + kc_win - 1) // kc_win
        o_win = _online_finish(lax.fori_loop(j_lo, j_hi, win_body, _online_init(rows, HEAD_DIM)))

        for r in range(NSA_GROUP):
            h = g * NSA_GROUP + r
            rs = slice(r * tq, (r + 1) * tq)
            o = (gates[:, 3 * h:3 * h + 1] * o_cmp[rs] + gates[:, 3 * h + 1:3 * h + 2] * o_slc[rs]
                 + gates[:, 3 * h + 2:3 * h + 3] * o_win[rs])
            o_ref[:, h * HEAD_DIM:(h + 1) * HEAD_DIM] = o


def nsa_attend_prompt(q, gates, kcmp, kvs, kvw, batch, seq, tq=128):
    nq = seq // tq
    nc = seq // NSA_BLOCK
    kern = functools.partial(_nsa_prompt_kernel, tq=tq, kc_slc=min(256, seq), kc_win=tq)
    return pl.pallas_call(
        kern,
        grid=(batch, nq),
        in_specs=[pl.BlockSpec((tq, NSA_Q), lambda b, i: (b * nq + i, 0)),
                  pl.BlockSpec((tq, LANE), lambda b, i: (b * nq + i, 0)),
                  pl.BlockSpec((nc, KV_ROW), lambda b, i: (b, 0)),
                  pl.BlockSpec((seq, KV_ROW), lambda b, i: (b, 0)),
                  pl.BlockSpec((seq, KV_ROW), lambda b, i: (b, 0))],
        out_specs=pl.BlockSpec((tq, NSA_Q), lambda b, i: (b * nq + i, 0)),
        out_shape=jax.ShapeDtypeStruct((batch * seq, NSA_Q), F32),
        compiler_params=_params("parallel", "arbitrary"),
        name="nsa_prompt",
    )(q, gates, kcmp, kvs, kvw)


def _nsa_decode_kernel(pt_ref, q_ref, gate_ref, kvs_new_ref, kvw_new_ref, win_ref, *rest, n_pages, page, past):
    kcmp_refs = rest[:n_pages]
    kvs_refs = rest[n_pages:2 * n_pages]
    o_ref = rest[2 * n_pages]
    kcmp_scr = rest[2 * n_pages + 1]
    del pt_ref
    tq = TQ_PAD
    rows = NSA_HEADS * tq
    blocks_per_page = page // NSA_BLOCK
    nc = n_pages * blocks_per_page
    nb = nc + 1
    for p in range(n_pages):
        kcmp_scr[p * blocks_per_page:(p + 1) * blocks_per_page, :] = kcmp_refs[p][...]

    row = lax.broadcasted_iota(jnp.int32, (rows, 1), 0)
    t_rows = past + row % tq
    t1 = past + lax.broadcasted_iota(jnp.int32, (tq, 1), 0)
    lane = lax.broadcasted_iota(jnp.int32, (rows, LANE), 1)
    own_group = (lane // HEAD_DIM) == (row // (NSA_GROUP * tq))

    q = q_ref[...] * SCALE
    pieces = []
    for h in range(NSA_HEADS):
        qh = q[:, h * HEAD_DIM:(h + 1) * HEAD_DIM]
        zero = jnp.zeros_like(qh)
        pieces.append(jnp.concatenate([qh, zero] if h < NSA_GROUP else [zero, qh], axis=1))
    qbd = jnp.concatenate(pieces, axis=0).astype(BF16)

    def own_half(o):
        o = jnp.where(own_group, o, 0.0)
        return o[:, :HEAD_DIM] + o[:, HEAD_DIM:]

    kcm = kcmp_scr[:, :NSA_KV].astype(BF16)
    vcm = kcmp_scr[:, NSA_KV:].astype(BF16)
    blk = lax.broadcasted_iota(jnp.int32, (1, nc), 1)
    lc = _dot_nt(qbd, kcm) + jnp.where((blk + 1) * NSA_BLOCK - 1 <= t_rows, 0.0, NEG_INF)
    pc = _softmax_rows(lc)
    o_cmp = own_half(_dot(pc.astype(BF16), vcm))

    gates = gate_ref[...]
    new_k_pos = past + lax.broadcasted_iota(jnp.int32, (1, page), 1)
    pad = jnp.zeros((page - tq, KV_ROW), F32)
    kvs_new = jnp.concatenate([kvs_new_ref[...], pad], axis=0).astype(BF16)
    kvw_new = jnp.concatenate([kvw_new_ref[...], pad], axis=0).astype(BF16)

    o_slc_parts = []
    for g in range(NSA_KV_HEADS):
        base = g * NSA_GROUP * tq
        pcg = pc[base:base + NSA_GROUP * tq]
        score = pcg[0:tq] + pcg[tq:2 * tq] + pcg[2 * tq:3 * tq] + pcg[3 * tq:4 * tq]
        score = jnp.concatenate([score, jnp.zeros((tq, LANE - nc), F32)], axis=1)
        sel = _select_blocks(score, t1, nb)
        qg = qbd[base:base + NSA_GROUP * tq]
        carry = _online_init(NSA_GROUP * tq, NSA_KV)
        for p in range(n_pages):
            ok = _expand_blocks(sel, p * page, page)
            bias = jnp.concatenate([jnp.where(ok > 0.5, 0.0, NEG_INF)] * NSA_GROUP, axis=0)
            kb = kvs_refs[p][:, :NSA_KV].astype(BF16)
            vb = kvs_refs[p][:, NSA_KV:].astype(BF16)
            carry = _online_step(carry, _dot_nt(qg, kb) + bias, vb)
        ok = jnp.where(new_k_pos <= t1, _expand_blocks(sel, past, page), 0.0)
        bias = jnp.concatenate([jnp.where(ok > 0.5, 0.0, NEG_INF)] * NSA_GROUP, axis=0)
        carry = _online_step(carry, _dot_nt(qg, kvs_new[:, :NSA_KV]) + bias, kvs_new[:, NSA_KV:])
        o_slc_parts.append(_online_finish(carry))
    o_slc = own_half(jnp.concatenate(o_slc_parts, axis=0))

    w_buf = win_ref.shape[0]
    wpos = past - w_buf + lax.broadcasted_iota(jnp.int32, (1, w_buf), 1)
    ok = jnp.where(wpos <= t_rows, jnp.where(wpos > t_rows - NSA_WINDOW, jnp.where(wpos >= 0, 1.0, 0.0), 0.0), 0.0)
    carry = _online_init(rows, NSA_KV)
    s = _dot_nt(qbd, win_ref[:, :NSA_KV].astype(BF16)) + jnp.where(ok > 0.5, 0.0, NEG_INF)
    carry = _online_step(carry, s, win_ref[:, NSA_KV:].astype(BF16))
    ok = jnp.where(new_k_pos <= t_rows, jnp.where(new_k_pos > t_rows - NSA_WINDOW, 1.0, 0.0), 0.0)
    s = _dot_nt(qbd, kvw_new[:, :NSA_KV]) + jnp.where(ok > 0.5, 0.0, NEG_INF)
    carry = _online_step(carry, s, kvw_new[:, NSA_KV:])
    o_win = own_half(_online_finish(carry))

    for h in range(NSA_HEADS):
        rs = slice(h * tq, (h + 1) * tq)
        o = (gates[:, 3 * h:3 * h + 1] * o_cmp[rs] + gates[:, 3 * h + 1:3 * h + 2] * o_slc[rs]
             + gates[:, 3 * h + 2:3 * h + 3] * o_win[rs])
        o_ref[:, h * HEAD_DIM:(h + 1) * HEAD_DIM] = o


def nsa_attend_decode(page_table, q, gates, kvs_new, kvw_new, win_state, kcmp_pool, slc_pool, layer, past):
    n_seq, n_pages = page_table.shape
    page = slc_pool.shape[2]
    bpp = page // NSA_BLOCK
    w_buf = win_state.shape[2]

    def seq_spec(width):
        return pl.BlockSpec((None, TQ_PAD, width), lambda b, pt: (b, 0, 0))

    def page_spec(shape, p, lead):
        return pl.BlockSpec((None, None) + shape, lambda b, pt: (lead, pt[b, p], 0, 0))

    assert page % NSA_BLOCK == 0 and TQ_PAD <= NSA_BLOCK and n_pages * bpp + 1 <= LANE
    kern = functools.partial(_nsa_decode_kernel, n_pages=n_pages, page=page, past=past)
    grid_spec = pltpu.PrefetchScalarGridSpec(
        num_scalar_prefetch=1,
        grid=(n_seq,),
        in_specs=([seq_spec(NSA_Q), seq_spec(LANE), seq_spec(KV_ROW), seq_spec(KV_ROW),
                   pl.BlockSpec((None, None, w_buf, KV_ROW), lambda b, pt: (layer, b, 0, 0))]
                  + [page_spec((bpp, KV_ROW), p, 0) for p in range(n_pages)]
                  + [page_spec((page, KV_ROW), p, layer) for p in range(n_pages)]),
        out_specs=seq_spec(NSA_Q),
        scratch_shapes=[pltpu.VMEM((n_pages * bpp, KV_ROW), F32)],
    )
    return pl.pallas_call(
        kern,
        grid_spec=grid_spec,
        out_shape=jax.ShapeDtypeStruct((n_seq, TQ_PAD, NSA_Q), F32),
        compiler_params=_params("arbitrary"),
        name="nsa_decode",
    )(page_table, q, gates, kvs_new, kvw_new, win_state, *([kcmp_pool] * n_pages), *([slc_pool] * n_pages))


def _lane_cumsum(x):
    n = x.shape[-1]
    lane = lax.broadcasted_iota(jnp.int32, x.shape, x.ndim - 1)
    s = 1
    while s < n:
        x = x + jnp.where(lane >= s, pltpu.roll(x, s, x.ndim - 1), 0.0)
        s *= 2
    return x


def _cumsum_kernel(x_ref, o_ref):
    o_ref[...] = _lane_cumsum(x_ref[...])


def cumsum_lanes(x):
    b, r, t = x.shape
    return pl.pallas_call(
        _cumsum_kernel,
        grid=(b,),
        in_specs=[pl.BlockSpec((None, r, t), lambda i: (i, 0, 0))],
        out_specs=pl.BlockSpec((None, r, t), lambda i: (i, 0, 0)),
        out_shape=jax.ShapeDtypeStruct(x.shape, F32),
        compiler_params=_params("parallel"),
        name="logf_cumsum",
    )(x)


def _fox_prompt_kernel(q_ref, kv_ref, cum_ref, cumq_ref, o_ref, *, tq, kc):
    i = pl.program_id(1)
    t0 = i * tq
    t = t0 + lax.broadcasted_iota(jnp.int32, (tq, 1), 0)
    n_chunks = (t0 + tq + kc - 1) // kc
    for h in range(FOX_HEADS):
        qh = (q_ref[:, h * HEAD_DIM:(h + 1) * HEAD_DIM] * SCALE).astype(BF16)
        cq = cumq_ref[:, h:h + 1]

        def body(j, carry):
            k0 = pl.multiple_of(j * kc, kc)
            kpos = k0 + lax.broadcasted_iota(jnp.int32, (1, kc), 1)
            kb = kv_ref[pl.ds(k0, kc), h * HEAD_DIM:(h + 1) * HEAD_DIM].astype(BF16)
            vb = kv_ref[pl.ds(k0, kc), FOX_W + h * HEAD_DIM:FOX_W + (h + 1) * HEAD_DIM].astype(BF16)
            decay = cq - cum_ref[j][h:h + 1, :]
            s = _dot_nt(qh, kb) + decay + jnp.where(kpos <= t, 0.0, NEG_INF)
            return _online_step(carry, s, vb)

        o = _online_finish(lax.fori_loop(0, n_chunks, body, _online_init(tq, HEAD_DIM)))
        o_ref[:, h * HEAD_DIM:(h + 1) * HEAD_DIM] = o


def fox_attend_prompt(qf, kvf, cum, cum_q, batch, seq, tq=256):
    tq = min(tq, seq)
    nq = seq // tq
    kern = functools.partial(_fox_prompt_kernel, tq=tq, kc=tq)
    return pl.pallas_call(
        kern,
        grid=(batch, nq),
        in_specs=[pl.BlockSpec((tq, FOX_W), lambda b, i: (b * nq + i, 0)),
                  pl.BlockSpec((seq, FOX_ROW), lambda b, i: (b, 0)),
                  pl.BlockSpec((None, nq, 8, tq), lambda b, i: (b, 0, 0, 0)),
                  pl.BlockSpec((tq, 8), lambda b, i: (b * nq + i, 0))],
        out_specs=pl.BlockSpec((tq, FOX_W), lambda b, i: (b * nq + i, 0)),
        out_shape=jax.ShapeDtypeStruct((batch * seq, FOX_W), F32),
        compiler_params=_params("parallel", "arbitrary"),
        name="fox_prompt",
    )(qf, kvf, jnp.swapaxes(cum.reshape(batch, 8, nq, tq), 1, 2), cum_q)


def _fox_decode_kernel(pt_ref, q_ref, kv_new_ref, lf_new_ref, *rest, n_pages, page, past):
    kv_refs = rest[:n_pages]
    lf_refs = rest[n_pages:2 * n_pages]
    o_ref = rest[2 * n_pages]
    del pt_ref
    tq = TQ_PAD
    rows = FOX_HEADS * tq
    row = lax.broadcasted_iota(jnp.int32, (rows, 1), 0)
    lane = lax.broadcasted_iota(jnp.int32, (rows, FOX_W), 1)
    own_head = (lane // HEAD_DIM) == (row // tq)
    t_rows = past + row % tq

    q = q_ref[...] * SCALE
    qbd = jnp.where(own_head, jnp.concatenate([q] * FOX_HEADS, axis=0), 0.0).astype(BF16)

    def head_rows(a):
        return jnp.concatenate(
            [jnp.broadcast_to(a[h:h + 1, :], (tq, a.shape[1])) for h in range(FOX_HEADS)], axis=0)

    lf_past = jnp.concatenate([r[...] for r in lf_refs], axis=1)
    cum_past = _lane_cumsum(lf_past)
    cum_new = cum_past[:, past - 1:past] + _lane_cumsum(lf_new_ref[...])
    ck_past = head_rows(cum_past)
    ck_new = head_rows(cum_new)
    new_lane = lax.broadcasted_iota(jnp.int32, (rows, page), 1)
    cq = jnp.sum(jnp.where(new_lane == row % tq, ck_new, 0.0), axis=-1, keepdims=True)

    carry = _online_init(rows, FOX_W)
    for p in range(n_pages):
        kb = kv_refs[p][:, :FOX_W].astype(BF16)
        vb = kv_refs[p][:, FOX_W:].astype(BF16)
        s = _dot_nt(qbd, kb) + (cq - ck_past[:, p * page:(p + 1) * page])
        carry = _online_step(carry, s, vb)

    kv_new = jnp.concatenate([kv_new_ref[...], jnp.zeros((page - tq, FOX_ROW), F32)], axis=0).astype(BF16)
    new_k_pos = past + lax.broadcasted_iota(jnp.int32, (1, page), 1)
    s = (_dot_nt(qbd, kv_new[:, :FOX_W]) + (cq - ck_new) + jnp.where(new_k_pos <= t_rows, 0.0, NEG_INF))
    carry = _online_step(carry, s, kv_new[:, FOX_W:])
    o = jnp.where(own_head, _online_finish(carry), 0.0)
    o_ref[...] = o[0:tq] + o[tq:2 * tq] + o[2 * tq:3 * tq] + o[3 * tq:4 * tq]


def fox_attend_decode(page_table, qf, kvf_new, lf_new_t, fox_pool, lf_pool_t, layer, past):
    n_seq, n_pages = page_table.shape
    page = fox_pool.shape[2]

    def seq_spec(r, width):
        return pl.BlockSpec((None, r, width), lambda b, pt: (b, 0, 0))

    def page_spec(shape, p):
        return pl.BlockSpec((None, None) + shape, lambda b, pt: (layer, pt[b, p], 0, 0))

    assert TQ_PAD <= page
    kern = functools.partial(_fox_decode_kernel, n_pages=n_pages, page=page, past=past)
    grid_spec = pltpu.PrefetchScalarGridSpec(
        num_scalar_prefetch=1,
        grid=(n_seq,),
        in_specs=([seq_spec(TQ_PAD, FOX_W), seq_spec(TQ_PAD, FOX_ROW), seq_spec(8, page)]
                  + [page_spec((page, FOX_ROW), p) for p in range(n_pages)]
                  + [page_spec((8, page), p) for p in range(n_pages)]),
        out_specs=seq_spec(TQ_PAD, FOX_W),
    )
    return pl.pallas_call(
        kern,
        grid_spec=grid_spec,
        out_shape=jax.ShapeDtypeStruct((n_seq, TQ_PAD, FOX_W), F32),
        compiler_params=_params("arbitrary"),
        name="fox_decode",
    )(page_table, qf, kvf_new, lf_new_t, *([fox_pool] * n_pages), *([lf_pool_t] * n_pages))


CONV_PAD = 32
CONV_SUB = 64


def _ln_silu(y, g, b):
    yc = y - jnp.mean(y, axis=-1, keepdims=True)
    var = jnp.mean(yc * yc, axis=-1, keepdims=True)
    return _silu(yc * lax.rsqrt(var + LN_EPS) * g + b)


def _conv_prompt_kernel(prev_ref, cur_ref, w_ref, b_ref, g_ref, bb_ref, o_ref, ctx_scr, *, tc):
    k = pl.program_id(1)
    ctx_scr[0:CONV_PAD, :] = jnp.where(k > 0, prev_ref[...], 0.0)
    ctx_scr[CONV_PAD:, :] = cur_ref[...]
    lead = CONV_PAD - (CONV_WIDTH - 1)
    for sub in range(tc // CONV_SUB):
        acc = jnp.zeros((CONV_SUB, CONV_CH), F32) + b_ref[...]
        for w in range(CONV_WIDTH):
            acc = acc + ctx_scr[pl.ds(sub * CONV_SUB + lead + w, CONV_SUB), :] * w_ref[w:w + 1, :]
        o_ref[sub * CONV_SUB:(sub + 1) * CONV_SUB, :] = _ln_silu(acc, g_ref[...], bb_ref[...])


def conv_module_prompt(glu, dw_w, dw_b, ln_g, ln_b, layer, batch, seq, tc=256):
    tc = min(tc, seq)
    nt = seq // tc
    per = tc // CONV_PAD

    def vec_spec():
        return pl.BlockSpec((None, 1, CONV_CH), lambda b, k: (layer, 0, 0))

    return pl.pallas_call(
        functools.partial(_conv_prompt_kernel, tc=tc),
        grid=(batch, nt),
        in_specs=[pl.BlockSpec((CONV_PAD, CONV_CH), lambda b, k: (jnp.maximum((b * nt + k) * per - 1, 0), 0)),
                  pl.BlockSpec((tc, CONV_CH), lambda b, k: (b * nt + k, 0)),
                  pl.BlockSpec((None, CONV_WIDTH, CONV_CH), lambda b, k: (layer, 0, 0)),
                  vec_spec(), vec_spec(), vec_spec()],
        out_specs=pl.BlockSpec((tc, CONV_CH), lambda b, k: (b * nt + k, 0)),
        out_shape=jax.ShapeDtypeStruct((batch * seq, CONV_CH), F32),
        scratch_shapes=[pltpu.VMEM((CONV_PAD + tc, CONV_CH), F32)],
        compiler_params=_params("parallel", "arbitrary"),
        name="conv_prompt",
    )(glu, glu, dw_w, dw_b, ln_g, ln_b)


def _conv_decode_kernel(state_ref, glu_ref, w_ref, b_ref, g_ref, bb_ref, o_ref):
    n_state = state_ref.shape[0]
    t_new = glu_ref.shape[0]
    ctx = [state_ref[i] for i in range(n_state)] + [glu_ref[i] for i in range(t_new)]
    for i in range(t_new):
        acc = jnp.zeros(ctx[0].shape, F32) + b_ref[...]
        for w in range(CONV_WIDTH):
            acc = acc + ctx[i + w] * w_ref[w:w + 1, :]
        o_ref[i] = _ln_silu(acc, g_ref[...], bb_ref[...])


def conv_module_decode(state_t, glu_t, dw_w, dw_b, ln_g, ln_b, layer):
    t_new, n_seq, _ = glu_t.shape

    def vec_spec():
        return pl.BlockSpec((None, 1, CONV_CH), lambda i: (layer, 0, 0))

    return pl.pallas_call(
        _conv_decode_kernel,
        grid=(1,),
        in_specs=[pl.BlockSpec(state_t.shape, lambda i: (0, 0, 0)),
                  pl.BlockSpec(glu_t.shape, lambda i: (0, 0, 0)),
                  pl.BlockSpec((None, CONV_WIDTH, CONV_CH), lambda i: (layer, 0, 0)),
                  vec_spec(), vec_spec(), vec_spec()],
        out_specs=pl.BlockSpec((t_new, n_seq, CONV_CH), lambda i: (0, 0, 0)),
        out_shape=jax.ShapeDtypeStruct((t_new, n_seq, CONV_CH), F32),
        compiler_params=_params("arbitrary"),
        name="conv_decode",
    )(state_t, glu_t, dw_w, dw_b, ln_g, ln_b)


def _merge_kernel(x_ref, ga_ref, on_ref, yc_ref, of_ref, mg_ref, wn_ref, wc_ref, wf_ref, wo_ref, o_ref):
    merged = (mg_ref[:, 0:D_MODEL] * _dot(on_ref[...].astype(BF16), wn_ref[...])
              + mg_ref[:, D_MODEL:2 * D_MODEL] * _dot(yc_ref[...].astype(BF16), wc_ref[...])
              + mg_ref[:, 2 * D_MODEL:] * _dot(of_ref[...].astype(BF16), wf_ref[...]))
    o_ref[...] = x_ref[...] + ga_ref[...] * _dot(merged.astype(BF16), wo_ref[...])


def merge_and_project(x, mod, o_nsa, y_conv, o_fox, mg, wb_nsa, wb_conv, wb_fox, w_out, layer, tm):
    m = x.shape[0]

    def row_spec(width):
        return pl.BlockSpec((tm, width), lambda i: (i, 0))

    def w_spec(rows):
        return pl.BlockSpec((None, rows, D_MODEL), lambda i: (layer, 0, 0))

    return pl.pallas_call(
        _merge_kernel,
        grid=(m // tm,),
        in_specs=[row_spec(D_MODEL), mod.spec(5, tm), row_spec(NSA_Q), row_spec(CONV_CH), row_spec(FOX_W),
                  row_spec(3 * D_MODEL), w_spec(NSA_Q), w_spec(CONV_CH), w_spec(FOX_W), w_spec(D_MODEL)],
        out_specs=row_spec(D_MODEL),
        out_shape=jax.ShapeDtypeStruct((m, D_MODEL), F32),
        compiler_params=_params("parallel"),
        name="merge",
    )(x, mod.arr, o_nsa, y_conv, o_fox, mg, wb_nsa, wb_conv, wb_fox, w_out)


def _final_norm_kernel(x_ref, g_ref, o_ref):
    x = x_ref[...]
    o_ref[...] = x * lax.rsqrt(jnp.mean(x * x, axis=-1, keepdims=True) + RMS_EPS) * g_ref[...]


def final_norm(x, g, tm):
    m = x.shape[0]
    return pl.pallas_call(
        _final_norm_kernel,
        grid=(m // tm,),
        in_specs=[pl.BlockSpec((tm, D_MODEL), lambda i: (i, 0)), pl.BlockSpec((1, D_MODEL), lambda i: (0, 0))],
        out_specs=pl.BlockSpec((tm, D_MODEL), lambda i: (i, 0)),
        out_shape=jax.ShapeDtypeStruct((m, D_MODEL), F32),
        compiler_params=_params("parallel"),
        name="final_norm",
    )(x, g.reshape(1, D_MODEL))


def _rope_tables(pos):
    half = HEAD_DIM // 2
    inv_freq = ROPE_THETA ** (-jnp.arange(half, dtype=F32) / half)
    ang = pos.astype(F32)[:, None] * inv_freq[None, :]
    cos, sin = jnp.cos(ang), jnp.sin(ang)
    cos = jnp.concatenate([cos, cos] * (LANE // HEAD_DIM), axis=1)
    sin = jnp.concatenate([-sin, sin] * (LANE // HEAD_DIM), axis=1)
    return cos, sin


def _aligned_in_proj(w_in, b_in):
    o_gn = NSA_Q + 3 * KV_ROW
    o_glu = o_gn + 3 * NSA_HEADS
    o_qkvf = o_glu + 2 * CONV_CH
    o_lf = o_qkvf + 3 * FOX_W
    o_mg = o_lf + FOX_HEADS

    def cols(a):
        pad_gn = jnp.zeros(a.shape[:-1] + (LANE - 3 * NSA_HEADS,), a.dtype)
        pad_lf = jnp.zeros(a.shape[:-1] + (LANE - FOX_HEADS,), a.dtype)
        return jnp.concatenate([a[..., :o_gn], a[..., o_glu:o_lf], a[..., o_mg:],
                                a[..., o_gn:o_glu], pad_gn, a[..., o_lf:o_mg], pad_lf], axis=-1)

    return cols(w_in).astype(BF16), cols(b_in)[:, None, :]


def _compress_weights(pe, wk, wv):
    pe2 = jnp.concatenate([pe] * NSA_KV_HEADS, axis=-1)

    def block_diag(w):
        z = jnp.zeros_like(w)
        return jnp.concatenate([jnp.concatenate([w, z], axis=-1), jnp.concatenate([z, w], axis=-1)],
                               axis=-2).astype(BF16)

    return pe2, block_diag(wk), block_diag(wv)


def _pad_tokens(a, n_seq, t_new):
    a = a.reshape(n_seq, t_new, a.shape[-1])
    return jnp.pad(a, ((0, 0), (0, TQ_PAD - t_new), (0, 0)))


def kernel(x_prompt, x_sample, cache_nsa_cmp_kv, cache_nsa_slc_kv, cache_fox_kv, cache_fox_logf,
           state_nsa_win_kv, state_conv, page_table, c_prompt, c_sample, w_ada, b_ada, norm_g,
           w_ffn_gu, w_ffn_down, w_in, b_in, nsa_cmp_pe, nsa_cmp_wk, nsa_cmp_wv, conv_dw_w, conv_dw_b,
           conv_ln_g, conv_ln_b, w_branch_nsa, w_branch_conv, w_branch_fox, w_out, final_norm_g):
    batch, seq, _ = x_prompt.shape
    n_seq, t_new, _ = x_sample.shape
    depth = w_ada.shape[0]
    n_pool, page = cache_nsa_cmp_kv.shape[1], cache_nsa_cmp_kv.shape[2]
    n_pages = page_table.shape[1]
    past = n_pages * page
    w_buf = state_nsa_win_kv.shape[2]
    mp, ms = batch * seq, n_seq * t_new
    tm_p = min(512, seq)
    tm_s = min(256, ms)

    w_ada_b = w_ada.astype(BF16)
    w_gu_b = w_ffn_gu.astype(BF16)
    w_down_b = w_ffn_down.astype(BF16)
    w_in_b, b_in_p = _aligned_in_proj(w_in, b_in)
    pe2, wk_bd, wv_bd = _compress_weights(nsa_cmp_pe, nsa_cmp_wk, nsa_cmp_wv)
    wb_nsa_b, wb_conv_b = w_branch_nsa.astype(BF16), w_branch_conv.astype(BF16)
    wb_fox_b, w_out_b = w_branch_fox.astype(BF16), w_out.astype(BF16)
    norm_g4 = norm_g[:, :, None, :]
    dw_b3, ln_g3, ln_b3 = conv_dw_b[:, None, :], conv_ln_g[:, None, :], conv_ln_b[:, None, :]

    cmp_pool = cache_nsa_cmp_kv.reshape(depth, n_pool * page * 2, NSA_KV)
    slc_pool = cache_nsa_slc_kv.reshape(depth, n_pool, page, KV_ROW)
    fox_pool = cache_fox_kv.reshape(depth, n_pool, page, FOX_ROW)
    lf_pool_t = jnp.pad(jnp.swapaxes(cache_fox_logf, 2, 3).astype(F32),
                        ((0, 0), (0, 0), (0, 8 - FOX_HEADS), (0, 0)))
    win_state = state_nsa_win_kv.reshape(depth, n_seq, w_buf, KV_ROW)

    cos_p, sin_p = _rope_tables(jnp.arange(seq, dtype=jnp.int32))
    cos_s, sin_s = _rope_tables(past + jnp.arange(t_new, dtype=jnp.int32))
    cos_s, sin_s = jnp.tile(cos_s, (n_seq, 1)), jnp.tile(sin_s, (n_seq, 1))

    mod_all = ada_all_layers(jnp.concatenate([c_prompt, c_sample], axis=0), w_ada_b, b_ada)

    xp = x_prompt.reshape(mp, D_MODEL)
    xs = x_sample.reshape(ms, D_MODEL)
    out_p = [[] for _ in range(6)]
    out_s = [[] for _ in range(6)]

    for l in range(depth):
        mod_p = Mod(mod_all[l, :batch, None, :], seq)
        mod_s = Mod(jnp.repeat(mod_all[l, batch:], t_new, axis=0), 1)

        xp = ffn_half_step(xp, mod_p, 0, norm_g4, w_gu_b, w_down_b, l, 0, tm_p)
        xs = ffn_half_step(xs, mod_s, 0, norm_g4, w_gu_b, w_down_b, l, 0, tm_s)

        tm_in = min(256, seq)
        (q_p, kvc_p, kvs_p, kvw_p, glu_p, qf_p, kvf_p, mg_p, gn_p, lf_p) = input_projection(
            xp, mod_p, norm_g4, w_in_b, b_in_p, cos_p, sin_p, l, tm_in, seq // tm_in)
        (q_s, kvc_s, kvs_s, kvw_s, glu_s, qf_s, kvf_s, mg_s, gn_s, lf_s) = input_projection(
            xs, mod_s, norm_g4, w_in_b, b_in_p, cos_s, sin_s, l, tm_s, ms // tm_s)

        kcmp_p = nsa_compress_rows(kvc_p.reshape(mp * 2, NSA_KV), pe2, wk_bd, wv_bd, l, 128)
        o_nsa_p = nsa_attend_prompt(q_p, gn_p, kcmp_p, kvs_p, kvw_p, batch, seq)
        lf_pt = jnp.swapaxes(lf_p[:, :8].reshape(batch, seq, 8), 1, 2)
        cum_p = cumsum_lanes(lf_pt)
        cum_q = jnp.swapaxes(cum_p, 1, 2).reshape(mp, 8)
        o_fox_p = fox_attend_prompt(qf_p, kvf_p, cum_p, cum_q, batch, seq)
        y_conv_p = conv_module_prompt(glu_p, conv_dw_w, dw_b3, ln_g3, ln_b3, l, batch, seq)

        kcmp_pool = nsa_compress_rows(cmp_pool, pe2, wk_bd, wv_bd, l, 128)
        kcmp_pool = kcmp_pool.reshape(1, n_pool, page // NSA_BLOCK, KV_ROW)
        o_nsa_s = nsa_attend_decode(page_table, _pad_tokens(q_s, n_seq, t_new), _pad_tokens(gn_s, n_seq, t_new),
                                    _pad_tokens(kvs_s, n_seq, t_new), _pad_tokens(kvw_s, n_seq, t_new),
                                    win_state, kcmp_pool, slc_pool, l, past)
        o_nsa_s = o_nsa_s[:, :t_new].reshape(ms, NSA_Q)
        lf_new_t = jnp.swapaxes(lf_s[:, :FOX_HEADS].reshape(n_seq, t_new, FOX_HEADS), 1, 2)
        lf_new_t = jnp.pad(lf_new_t, ((0, 0), (0, 8 - FOX_HEADS), (0, page - t_new)))
        o_fox_s = fox_attend_decode(page_table, _pad_tokens(qf_s, n_seq, t_new), _pad_tokens(kvf_s, n_seq, t_new),
                                    lf_new_t, fox_pool, lf_pool_t, l, past)
        o_fox_s = o_fox_s[:, :t_new].reshape(ms, FOX_W)
        state_t = jnp.swapaxes(state_conv[l], 0, 1)
        glu_t = jnp.swapaxes(glu_s.reshape(n_seq, t_new, CONV_CH), 0, 1)
        y_conv_s = conv_module_decode(state_t, glu_t, conv_dw_w, dw_b3, ln_g3, ln_b3, l)
        y_conv_s = jnp.swapaxes(y_conv_s, 0, 1).reshape(ms, CONV_CH)

        xp = merge_and_project(xp, mod_p, o_nsa_p, y_conv_p, o_fox_p, mg_p,
                               wb_nsa_b, wb_conv_b, wb_fox_b, w_out_b, l, tm_p)
        xs = merge_and_project(xs, mod_s, o_nsa_s, y_conv_s, o_fox_s, mg_s,
                               wb_nsa_b, wb_conv_b, wb_fox_b, w_out_b, l, tm_s)
        xp = ffn_half_step(xp, mod_p, 6, norm_g4, w_gu_b, w_down_b, l, 1, tm_p)
        xs = ffn_half_step(xs, mod_s, 6, norm_g4, w_gu_b, w_down_b, l, 1, tm_s)

        kv5 = (2, NSA_KV_HEADS, HEAD_DIM)
        kvw_p5 = kvw_p.reshape((batch, seq) + kv5)
        glu_p3 = glu_p.reshape(batch, seq, CONV_CH)
        w_keep = min(NSA_WINDOW, seq)
        out_p[0].append(kvc_p.reshape((batch, seq) + kv5))
        out_p[1].append(kvs_p.reshape((batch, seq) + kv5))
        out_p[2].append(kvf_p.reshape(batch, seq, 2, FOX_HEADS, HEAD_DIM))
        out_p[3].append(lf_p[:, :FOX_HEADS].reshape(batch, seq, FOX_HEADS))
        out_p[4].append(kvw_p5[:, seq - w_keep:])
        out_p[5].append(glu_p3[:, seq - (CONV_WIDTH - 1):])
        kvw_s5 = kvw_s.reshape((n_seq, t_new) + kv5)
        out_s[0].append(kvc_s.reshape((n_seq, t_new) + kv5))
        out_s[1].append(kvs_s.reshape((n_seq, t_new) + kv5))
        out_s[2].append(kvf_s.reshape(n_seq, t_new, 2, FOX_HEADS, HEAD_DIM))
        out_s[3].append(lf_s[:, :FOX_HEADS].reshape(n_seq, t_new, FOX_HEADS))
        out_s[4].append(jnp.concatenate([state_nsa_win_kv[l], kvw_s5], axis=1)[:, t_new:])
        conv_ctx = jnp.concatenate([state_conv[l], glu_s.reshape(n_seq, t_new, CONV_CH)], axis=1)
        out_s[5].append(conv_ctx[:, conv_ctx.shape[1] - (CONV_WIDTH - 1):])

    y_p = final_norm(xp, final_norm_g, tm_p).reshape(batch, seq, D_MODEL)
    y_s = final_norm(xs, final_norm_g, tm_s).reshape(n_seq, t_new, D_MODEL)
    outs = [y_p, y_s]
    for f in range(6):
        outs.append(jnp.stack(out_p[f]))
        outs.append(jnp.stack(out_s[f]))
    return tuple(outs)
```

```python
import functools

import jax
import jax.numpy as jnp
from jax import lax
from jax.experimental import pallas as pl
from jax.experimental.pallas import tpu as pltpu

D_MODEL = 1024
HEAD_DIM = 64
NSA_HEADS = 8
NSA_KV_HEADS = 2
NSA_GROUP = NSA_HEADS // NSA_KV_HEADS
NSA_BLOCK = 64
NSA_TOP_K = 16
NSA_WINDOW = 512
FOX_HEADS = 4
CONV_CH = 256
CONV_WIDTH = 31
D_FF = 2816
ROPE_THETA = 10000.0
RMS_EPS = 1e-6
LN_EPS = 1e-5
NEG_INF = -1e30
POS_INF = 1e30
TINY = 1e-30
MASKED_BELOW = -5e29
SCALE = HEAD_DIM ** -0.5

NSA_Q = NSA_HEADS * HEAD_DIM
NSA_KV = NSA_KV_HEADS * HEAD_DIM
FOX_W = FOX_HEADS * HEAD_DIM
KV_ROW = 2 * NSA_KV
FOX_ROW = 2 * FOX_W

LANE = 128
C_Q = 0
C_KVC = C_Q + NSA_Q
C_KVS = C_KVC + KV_ROW
C_KVW = C_KVS + KV_ROW
C_GLU = C_KVW + KV_ROW
C_QF = C_GLU + 2 * CONV_CH
C_KVF = C_QF + FOX_W
C_MG = C_KVF + FOX_ROW
C_GN = C_MG + 3 * D_MODEL
C_LF = C_GN + LANE
D_IN_PAD = C_LF + LANE

TQ_PAD = 8
VMEM_LIMIT = 56 * 1024 * 1024

F32 = jnp.float32
BF16 = jnp.bfloat16
NT = (((1,), (1,)), ((), ()))


def _params(*sem):
    return pltpu.CompilerParams(dimension_semantics=sem, vmem_limit_bytes=VMEM_LIMIT)


def _dot(a, b):
    return jnp.dot(a, b, preferred_element_type=F32)


def _dot_nt(a, b):
    return lax.dot_general(a, b, NT, preferred_element_type=F32)


def _rms_mod(x, g, shift, scale):
    y = x * lax.rsqrt(jnp.mean(x * x, axis=-1, keepdims=True) + RMS_EPS) * g
    return y * (1.0 + scale) + shift


def _sigmoid(x):
    return 1.0 / (1.0 + jnp.exp(-x))


def _silu(x):
    return x * _sigmoid(x)


def _ada_kernel(c_ref, w_ref, b_ref, o_ref):
    c = _silu(c_ref[...]).astype(BF16)
    o_ref[...] = _dot(c, w_ref[...]) + b_ref[...]


def ada_all_layers(cond, w_ada, b_ada):
    depth, _, n = w_ada.shape
    rows = cond.shape[0]
    tn = n // 8
    return pl.pallas_call(
        _ada_kernel,
        grid=(depth, n // tn),
        in_specs=[pl.BlockSpec((rows, D_MODEL), lambda l, j: (0, 0)),
                  pl.BlockSpec((None, D_MODEL, tn), lambda l, j: (l, 0, j)),
                  pl.BlockSpec((None, 1, tn), lambda l, j: (l, 0, j))],
        out_specs=pl.BlockSpec((None, rows, tn), lambda l, j: (l, 0, j)),
        out_shape=jax.ShapeDtypeStruct((depth, rows, n), F32),
        compiler_params=_params("parallel", "parallel"),
        name="ada",
    )(cond, w_ada, b_ada.reshape(depth, 1, n))


class Mod:
    def __init__(self, arr, tokens_per_cond):
        self.arr = arr
        self.tokens_per_cond = tokens_per_cond

    def spec(self, k, tm):
        if self.arr.ndim == 3:
            per = self.tokens_per_cond // tm
            return pl.BlockSpec((None, 1, D_MODEL), lambda m, *_: (m // per, 0, k))
        return pl.BlockSpec((tm, D_MODEL), lambda m, *_: (m, k))


def _ffn_kernel(x_ref, g_ref, sh_ref, sc_ref, ga_ref, wg_ref, wu_ref, wd_ref, o_ref, h_scr, acc_scr):
    j = pl.program_id(1)

    @pl.when(j == 0)
    def _():
        h_scr[...] = _rms_mod(x_ref[...], g_ref[...], sh_ref[...], sc_ref[...]).astype(BF16)
        acc_scr[...] = jnp.zeros_like(acc_scr)

    h = h_scr[...]
    gate = _dot(h, wg_ref[...])
    up = _dot(h, wu_ref[...])
    a = (_silu(gate) * up).astype(BF16)
    acc_scr[...] += _dot(a, wd_ref[...])

    @pl.when(j == pl.num_programs(1) - 1)
    def _():
        o_ref[...] = x_ref[...] + 0.5 * ga_ref[...] * acc_scr[...]


def ffn_half_step(x, mod, k0, norm_g, w_gu, w_down, layer, idx, tm):
    m = x.shape[0]
    nf = 2
    tf = D_FF // nf
    return pl.pallas_call(
        _ffn_kernel,
        grid=(m // tm, nf),
        in_specs=[pl.BlockSpec((tm, D_MODEL), lambda i, j: (i, 0)),
                  pl.BlockSpec((None, None, 1, D_MODEL), lambda i, j: (layer, 2 * idx, 0, 0)),
                  mod.spec(k0, tm), mod.spec(k0 + 1, tm), mod.spec(k0 + 2, tm),
                  pl.BlockSpec((None, None, D_MODEL, tf), lambda i, j: (layer, idx, 0, j)),
                  pl.BlockSpec((None, None, D_MODEL, tf), lambda i, j: (layer, idx, 0, nf + j)),
                  pl.BlockSpec((None, None, tf, D_MODEL), lambda i, j: (layer, idx, j, 0))],
        out_specs=pl.BlockSpec((tm, D_MODEL), lambda i, j: (i, 0)),
        out_shape=jax.ShapeDtypeStruct((m, D_MODEL), F32),
        scratch_shapes=[pltpu.VMEM((tm, D_MODEL), BF16), pltpu.VMEM((tm, D_MODEL), F32)],
        compiler_params=_params("parallel", "arbitrary"),
        name="ffn",
    )(x, norm_g, mod.arr, mod.arr, mod.arr, w_gu, w_gu, w_down)


def _rope(z, cos, sin):
    n = z.shape[1]
    lane = lax.broadcasted_iota(jnp.int32, z.shape, 1)
    first_half = (lane % HEAD_DIM) < (HEAD_DIM // 2)
    partner = jnp.where(first_half, pltpu.roll(z, n - HEAD_DIM // 2, 1), pltpu.roll(z, HEAD_DIM // 2, 1))
    reps = n // LANE
    if reps > 1:
        cos = jnp.concatenate([cos] * reps, axis=1)
        sin = jnp.concatenate([sin] * reps, axis=1)
    return z * cos + partner * sin


def _inproj_kernel(x_ref, g_ref, sh_ref, sc_ref, cos_ref, sin_ref, w_ref, b_ref,
                   q_ref, kvc_ref, kvs_ref, kvw_ref, glu_ref, qf_ref, kvf_ref, mg_ref, gn_ref, lf_ref,
                   *feature_major_refs):
    u = _rms_mod(x_ref[...], g_ref[...], sh_ref[...], sc_ref[...]).astype(BF16)
    cos = cos_ref[...]
    sin = sin_ref[...]

    def proj(lo, hi):
        return _dot(u, w_ref[:, lo:hi]) + b_ref[:, lo:hi]

    q_ref[...] = _rope(proj(C_Q, C_KVC), cos, sin)
    kv_rows = []
    for ref, lo in ((kvc_ref, C_KVC), (kvs_ref, C_KVS), (kvw_ref, C_KVW)):
        z = proj(lo, lo + KV_ROW)
        z = jnp.concatenate([_rope(z[:, :NSA_KV], cos, sin), z[:, NSA_KV:]], axis=1)
        ref[...] = z
        kv_rows.append(z)
    z = proj(C_GLU, C_QF)
    glu_ref[...] = z[:, :CONV_CH] * _sigmoid(z[:, CONV_CH:])
    qf_ref[...] = proj(C_QF, C_KVF)
    kvf = proj(C_KVF, C_MG)
    kvf_ref[...] = kvf
    mg_ref[...] = _sigmoid(proj(C_MG, C_GN))
    gn_ref[...] = _sigmoid(proj(C_GN, C_LF))
    z = proj(C_LF, D_IN_PAD)
    lf = jnp.minimum(z, 0.0) - jnp.log(1.0 + jnp.exp(-jnp.abs(z)))
    lf_ref[...] = lf
    if feature_major_refs:
        kvc_t, kvs_t, kvw_t, kvf_t, lf_t = feature_major_refs
        for ref, rows in ((kvc_t, kv_rows[0]), (kvs_t, kv_rows[1]), (kvw_t, kv_rows[2]), (kvf_t, kvf)):
            ref[...] = rows.T
        lf_t[...] = lf.T[:lf_t.shape[0], :]


def input_projection(x, mod, norm_g, w_in, b_in, cos, sin, layer, tm, pos_tiles, feature_major_batch=None):
    m = x.shape[0]
    widths = (NSA_Q, KV_ROW, KV_ROW, KV_ROW, CONV_CH, FOX_W, FOX_ROW, 3 * D_MODEL, LANE, LANE)
    out_specs = [pl.BlockSpec((tm, w), lambda i: (i, 0)) for w in widths]
    out_shape = [jax.ShapeDtypeStruct((m, w), F32) for w in widths]
    if feature_major_batch:
        t = m // feature_major_batch
        for rows in (KV_ROW, KV_ROW, KV_ROW, FOX_ROW, 8):
            out_specs.append(pl.BlockSpec((None, rows, tm), lambda i: (i // pos_tiles, 0, i % pos_tiles)))
            out_shape.append(jax.ShapeDtypeStruct((feature_major_batch, rows, t), F32))
    return pl.pallas_call(
        _inproj_kernel,
        grid=(m // tm,),
        in_specs=[pl.BlockSpec((tm, D_MODEL), lambda i: (i, 0)),
                  pl.BlockSpec((None, None, 1, D_MODEL), lambda i: (layer, 1, 0, 0)),
                  mod.spec(3, tm), mod.spec(4, tm),
                  pl.BlockSpec((tm, LANE), lambda i: (i % pos_tiles, 0)),
                  pl.BlockSpec((tm, LANE), lambda i: (i % pos_tiles, 0)),
                  pl.BlockSpec((None, D_MODEL, D_IN_PAD), lambda i: (layer, 0, 0)),
                  pl.BlockSpec((None, 1, D_IN_PAD), lambda i: (layer, 0, 0))],
        out_specs=out_specs,
        out_shape=out_shape,
        compiler_params=_params("parallel"),
        name="inproj",
    )(x, norm_g, mod.arr, mod.arr, cos, sin, w_in, b_in)


def _compress_kernel(x_ref, pe_ref, wk_ref, wv_ref, o_ref):
    nblk = o_ref.shape[0]
    acc_k = jnp.zeros((nblk, NSA_KV), F32)
    acc_v = jnp.zeros((nblk, NSA_KV), F32)
    for l in range(NSA_BLOCK):
        pe = pe_ref[l:l + 1, :]
        k_rows = x_ref[pl.ds(2 * l, nblk, stride=2 * NSA_BLOCK), :] + pe
        v_rows = x_ref[pl.ds(2 * l + 1, nblk, stride=2 * NSA_BLOCK), :] + pe
        acc_k = acc_k + _dot(k_rows.astype(BF16), wk_ref[l])
        acc_v = acc_v + _dot(v_rows.astype(BF16), wv_ref[l])
    o_ref[:, :NSA_KV] = acc_k
    o_ref[:, NSA_KV:] = acc_v


def nsa_compress_rows(rows, pe2, wk_bd, wv_bd, layer, blocks_per_step):
    lead = rows.shape[:-2]
    nblk = rows.shape[-2] // (2 * NSA_BLOCK)
    nb = max(d for d in range(8, blocks_per_step + 1, 8) if nblk % d == 0)
    if lead:
        x_spec = pl.BlockSpec((None, nb * 2 * NSA_BLOCK, NSA_KV), lambda i: (layer, i, 0))
    else:
        x_spec = pl.BlockSpec((nb * 2 * NSA_BLOCK, NSA_KV), lambda i: (i, 0))
    w_spec = pl.BlockSpec((None, NSA_BLOCK, NSA_KV, NSA_KV), lambda i: (layer, 0, 0, 0))
    return pl.pallas_call(
        _compress_kernel,
        grid=(nblk // nb,),
        in_specs=[x_spec, pl.BlockSpec((None, NSA_BLOCK, NSA_KV), lambda i: (layer, 0, 0)), w_spec, w_spec],
        out_specs=pl.BlockSpec((nb, KV_ROW), lambda i: (i, 0)),
        out_shape=jax.ShapeDtypeStruct((nblk, KV_ROW), F32),
        compiler_params=_params("parallel"),
        name="nsa_compress",
    )(rows, pe2, wk_bd, wv_bd)


def _compress_pool_kernel(x_ref, pe_ref, wk_ref, wv_ref, o_ref, *, blocks_per_page):
    pages = o_ref.shape[0]
    page = x_ref.shape[1]
    for c in range(2 * NSA_KV_HEADS):
        w_ref = wk_ref if c < NSA_KV_HEADS else wv_ref
        acc = jnp.zeros((pages, page), F32)
        for d in range(HEAD_DIM):
            rows = x_ref[pl.ds(c * HEAD_DIM + d, pages, stride=KV_ROW), :] + pe_ref[d:d + 1, :]
            acc = acc + _dot(rows.astype(BF16), w_ref[d])
        for j in range(blocks_per_page):
            o_ref[:, j * KV_ROW + c * HEAD_DIM:j * KV_ROW + (c + 1) * HEAD_DIM] = acc[:, j * HEAD_DIM:(j + 1) * HEAD_DIM]


def nsa_compress_pool(pool_t, pe_t, wk_t, wv_t, layer, pages_per_step):
    page = pool_t.shape[-1]
    n_pool = pool_t.shape[1] // KV_ROW
    bpp = page // NSA_BLOCK
    assert page == LANE and bpp * HEAD_DIM == page
    pp = max(d for d in range(8, pages_per_step + 1, 8) if n_pool % d == 0)
    w_spec = pl.BlockSpec((None, HEAD_DIM, page, page), lambda i: (layer, 0, 0, 0))
    out = pl.pallas_call(
        functools.partial(_compress_pool_kernel, blocks_per_page=bpp),
        grid=(n_pool // pp,),
        in_specs=[pl.BlockSpec((None, pp * KV_ROW, page), lambda i: (layer, i, 0)),
                  pl.BlockSpec((None, HEAD_DIM, page), lambda i: (layer, 0, 0)), w_spec, w_spec],
        out_specs=pl.BlockSpec((pp, bpp * KV_ROW), lambda i: (i, 0)),
        out_shape=jax.ShapeDtypeStruct((n_pool, bpp * KV_ROW), F32),
        compiler_params=_params("parallel"),
        name="nsa_compress_pool",
    )(pool_t, pe_t, wk_t, wv_t)
    return out.reshape(n_pool, 1, bpp * KV_ROW)


def _softmax_rows(logits):
    m = jnp.max(logits, axis=-1, keepdims=True)
    p = jnp.where(logits > MASKED_BELOW, jnp.exp(logits - m), 0.0)
    return p / jnp.maximum(jnp.sum(p, axis=-1, keepdims=True), TINY)


def _online_step(carry, s, v):
    m, l, acc = carry
    m_new = jnp.maximum(m, jnp.max(s, axis=-1, keepdims=True))
    alpha = jnp.exp(m - m_new)
    p = jnp.where(s > MASKED_BELOW, jnp.exp(s - m_new), 0.0)
    l = alpha * l + jnp.sum(p, axis=-1, keepdims=True)
    acc = alpha * acc + _dot(p.astype(BF16), v)
    return m_new, l, acc


def _online_init(rows, d):
    return (jnp.full((rows, 1), NEG_INF, F32), jnp.zeros((rows, 1), F32), jnp.zeros((rows, d), F32))


def _online_finish(carry):
    _, l, acc = carry
    return acc / jnp.maximum(l, TINY)


def _select_blocks(score, t, nb):
    blk = lax.broadcasted_iota(jnp.int32, score.shape, 1)
    cur = t // NSA_BLOCK
    forced = jnp.where(blk == 0, 1, jnp.where(blk == cur, 1, jnp.where(blk == cur - 1, 1, 0)))
    s = jnp.where(forced == 1, POS_INF, jnp.where(blk > cur, NEG_INF, score))
    rank = jnp.zeros(score.shape, jnp.int32)
    for i in range(nb):
        col = s[:, i:i + 1]
        ahead = jnp.where(col > s, 1, jnp.where(col == s, jnp.where(blk > i, 1, 0), 0))
        rank = rank + ahead
    k_sel = min(NSA_TOP_K, nb)
    return jnp.where(rank < k_sel, jnp.where(blk < nb, 1.0, 0.0), 0.0)


def _expand_blocks(sel, k0, kc):
    width = sel.shape[1]
    kblk = (k0 + lax.broadcasted_iota(jnp.int32, (width, kc), 1)) // NSA_BLOCK
    e = jnp.where(kblk == lax.broadcasted_iota(jnp.int32, (width, kc), 0), 1.0, 0.0).astype(BF16)
    return _dot(sel.astype(BF16), e)


def _nsa_prompt_kernel(q_ref, gate_ref, kcmp_ref, kvs_ref, kvw_ref, o_ref, *, tq, kc_slc, kc_win):
    i = pl.program_id(1)
    t0 = i * tq
    t = t0 + lax.broadcasted_iota(jnp.int32, (tq, 1), 0)
    t4 = jnp.concatenate([t] * NSA_GROUP, axis=0)
    rows = NSA_GROUP * tq
    nc = kcmp_ref.shape[0]
    gates = gate_ref[...]

    def tile_rows(a):
        return jnp.concatenate([a] * NSA_GROUP, axis=0)

    for g in range(NSA_KV_HEADS):
        k_lo, v_lo = g * HEAD_DIM, NSA_KV + g * HEAD_DIM
        q4 = jnp.concatenate(
            [q_ref[:, (g * NSA_GROUP + r) * HEAD_DIM:(g * NSA_GROUP + r + 1) * HEAD_DIM] for r in range(NSA_GROUP)],
            axis=0)
        q4 = (q4 * SCALE).astype(BF16)

        kcm = kcmp_ref[:, k_lo:k_lo + HEAD_DIM].astype(BF16)
        vcm = kcmp_ref[:, v_lo:v_lo + HEAD_DIM].astype(BF16)
        blk = lax.broadcasted_iota(jnp.int32, (1, nc), 1)
        lc = _dot_nt(q4, kcm) + jnp.where((blk + 1) * NSA_BLOCK - 1 <= t4, 0.0, NEG_INF)
        pc = _softmax_rows(lc)
        o_cmp = _dot(pc.astype(BF16), vcm)
        score = pc[0:tq] + pc[tq:2 * tq] + pc[2 * tq:3 * tq] + pc[3 * tq:4 * tq]
        sel = _select_blocks(score, t, nc)

        def slc_body(j, carry):
            k0 = pl.multiple_of(j * kc_slc, kc_slc)
            kpos = k0 + lax.broadcasted_iota(jnp.int32, (1, kc_slc), 1)
            ok = jnp.where(kpos <= t, _expand_blocks(sel, k0, kc_slc), 0.0)
            bias = tile_rows(jnp.where(ok > 0.5, 0.0, NEG_INF))
            kb = kvs_ref[pl.ds(k0, kc_slc), k_lo:k_lo + HEAD_DIM].astype(BF16)
            vb = kvs_ref[pl.ds(k0, kc_slc), v_lo:v_lo + HEAD_DIM].astype(BF16)
            return _online_step(carry, _dot_nt(q4, kb) + bias, vb)

        n_slc = (t0 + tq + kc_slc - 1) // kc_slc
        o_slc = _online_finish(lax.fori_loop(0, n_slc, slc_body, _online_init(rows, HEAD_DIM)))

        def win_body(j, carry):
            k0 = pl.multiple_of(j * kc_win, kc_win)
            kpos = k0 + lax.broadcasted_iota(jnp.int32, (1, kc_win), 1)
            ok = jnp.where(kpos <= t, jnp.where(kpos > t - NSA_WINDOW, 1.0, 0.0), 0.0)
            bias = tile_rows(jnp.where(ok > 0.5, 0.0, NEG_INF))
            kb = kvw_ref[pl.ds(k0, kc_win), k_lo:k_lo + HEAD_DIM].astype(BF16)
            vb = kvw_ref[pl.ds(k0, kc_win), v_lo:v_lo + HEAD_DIM].astype(BF16)
            return _online_step(carry, _dot_nt(q4, kb) + bias, vb)

        j_lo = jnp.maximum(t0 - NSA_WINDOW, 0) // kc_win
        j_hi = (t0 + tq + kc_win - 1) // kc_win
        o_win = _online_finish(lax.fori_loop(j_lo, j_hi, win_body, _online_init(rows, HEAD_DIM)))

        for r in range(NSA_GROUP):
            h = g * NSA_GROUP + r
            rs = slice(r * tq, (r + 1) * tq)
            o = (gates[:, 3 * h:3 * h + 1] * o_cmp[rs] + gates[:, 3 * h + 1:3 * h + 2] * o_slc[rs]
                 + gates[:, 3 * h + 2:3 * h + 3] * o_win[rs])
            o_ref[:, h * HEAD_DIM:(h + 1) * HEAD_DIM] = o


def nsa_attend_prompt(q, gates, kcmp, kvs, kvw, batch, seq, tq=128):
    nq = seq // tq
    nc = seq // NSA_BLOCK
    kern = functools.partial(_nsa_prompt_kernel, tq=tq, kc_slc=min(256, seq), kc_win=tq)
    return pl.pallas_call(
        kern,
        grid=(batch, nq),
        in_specs=[pl.BlockSpec((tq, NSA_Q), lambda b, i: (b * nq + i, 0)),
                  pl.BlockSpec((tq, LANE), lambda b, i: (b * nq + i, 0)),
                  pl.BlockSpec((nc, KV_ROW), lambda b, i: (b, 0)),
                  pl.BlockSpec((seq, KV_ROW), lambda b, i: (b, 0)),
                  pl.BlockSpec((seq, KV_ROW), lambda b, i: (b, 0))],
        out_specs=pl.BlockSpec((tq, NSA_Q), lambda b, i: (b * nq + i, 0)),
        out_shape=jax.ShapeDtypeStruct((batch * seq, NSA_Q), F32),
        compiler_params=_params("parallel", "arbitrary"),
        name="nsa_prompt",
    )(q, gates, kcmp, kvs, kvw)


def _nsa_decode_kernel(pt_ref, q_ref, gate_ref, kvs_new_ref, kvw_new_ref, win_ref, *rest, n_pages, page, past):
    kcmp_refs = rest[:n_pages]
    kvs_refs = rest[n_pages:2 * n_pages]
    o_ref = rest[2 * n_pages]
    kcmp_scr = rest[2 * n_pages + 1]
    del pt_ref
    tq = TQ_PAD
    rows = NSA_HEADS * tq
    blocks_per_page = page // NSA_BLOCK
    nc = n_pages * blocks_per_page
    nb = nc + 1
    for p in range(n_pages):
        for j in range(blocks_per_page):
            kcmp_scr[p * blocks_per_page + j:p * blocks_per_page + j + 1, :] = (
                kcmp_refs[p][:, j * KV_ROW:(j + 1) * KV_ROW])

    row = lax.broadcasted_iota(jnp.int32, (rows, 1), 0)
    t_rows = past + row % tq
    t1 = past + lax.broadcasted_iota(jnp.int32, (tq, 1), 0)
    lane = lax.broadcasted_iota(jnp.int32, (rows, LANE), 1)
    own_group = (lane // HEAD_DIM) == (row // (NSA_GROUP * tq))

    q = q_ref[...] * SCALE
    pieces = []
    for h in range(NSA_HEADS):
        qh = q[:, h * HEAD_DIM:(h + 1) * HEAD_DIM]
        zero = jnp.zeros_like(qh)
        pieces.append(jnp.concatenate([qh, zero] if h < NSA_GROUP else [zero, qh], axis=1))
    qbd = jnp.concatenate(pieces, axis=0).astype(BF16)

    def own_half(o):
        o = jnp.where(own_group, o, 0.0)
        return o[:, :HEAD_DIM] + o[:, HEAD_DIM:]

    kcm = kcmp_scr[:, :NSA_KV].astype(BF16)
    vcm = kcmp_scr[:, NSA_KV:].astype(BF16)
    blk = lax.broadcasted_iota(jnp.int32, (1, nc), 1)
    lc = _dot_nt(qbd, kcm) + jnp.where((blk + 1) * NSA_BLOCK - 1 <= t_rows, 0.0, NEG_INF)
    pc = _softmax_rows(lc)
    o_cmp = own_half(_dot(pc.astype(BF16), vcm))

    gates = gate_ref[...]
    new_k_pos = past + lax.broadcasted_iota(jnp.int32, (1, page), 1)
    pad = jnp.zeros((page - tq, KV_ROW), F32)
    kvs_new = jnp.concatenate([kvs_new_ref[...], pad], axis=0).astype(BF16)
    kvw_new = jnp.concatenate([kvw_new_ref[...], pad], axis=0).astype(BF16)

    n_keys = (n_pages + 1) * page
    key_pos = lax.broadcasted_iota(jnp.int32, (1, n_keys), 1)
    sels = []
    for g in range(NSA_KV_HEADS):
        pcg = pc[g * NSA_GROUP * tq:(g + 1) * NSA_GROUP * tq]
        score = pcg[0:tq] + pcg[tq:2 * tq] + pcg[2 * tq:3 * tq] + pcg[3 * tq:4 * tq]
        score = jnp.concatenate([score, jnp.zeros((tq, LANE - nc), F32)], axis=1)
        sels.append(_select_blocks(score, t1, nb))
    ok = _expand_blocks(jnp.concatenate(sels, axis=0), 0, n_keys)
    t2 = jnp.concatenate([t1] * NSA_KV_HEADS, axis=0)
    bias2 = jnp.where(key_pos <= t2, jnp.where(ok > 0.5, 0.0, NEG_INF), NEG_INF)
    bias = jnp.concatenate(
        [bias2[g * tq:(g + 1) * tq] for g in range(NSA_KV_HEADS) for _ in range(NSA_GROUP)], axis=0)
    s = jnp.concatenate([_dot(qbd, kvs_refs[p][:NSA_KV, :].astype(BF16)) for p in range(n_pages)]
                        + [_dot_nt(qbd, kvs_new[:, :NSA_KV])], axis=1) + bias
    m = jnp.max(s, axis=-1, keepdims=True)
    pr = jnp.where(s > MASKED_BELOW, jnp.exp(s - m), 0.0)
    den = jnp.maximum(jnp.sum(pr, axis=-1, keepdims=True), TINY)
    pr = pr.astype(BF16)
    acc = _dot(pr[:, n_pages * page:], kvs_new[:, NSA_KV:])
    for p in range(n_pages):
        acc = acc + _dot_nt(pr[:, p * page:(p + 1) * page], kvs_refs[p][NSA_KV:, :].astype(BF16))
    o_slc = own_half(acc / den)

    w_buf = win_ref.shape[1]
    wpos = jnp.concatenate([past - w_buf + lax.broadcasted_iota(jnp.int32, (1, w_buf), 1), new_k_pos], axis=1)
    ok = jnp.where(wpos <= t_rows, jnp.where(wpos > t_rows - NSA_WINDOW, jnp.where(wpos >= 0, 1.0, 0.0), 0.0), 0.0)
    s = jnp.concatenate([_dot(qbd, win_ref[:NSA_KV, :].astype(BF16)), _dot_nt(qbd, kvw_new[:, :NSA_KV])], axis=1)
    s = s + jnp.where(ok > 0.5, 0.0, NEG_INF)
    m = jnp.max(s, axis=-1, keepdims=True)
    pr = jnp.where(s > MASKED_BELOW, jnp.exp(s - m), 0.0)
    den = jnp.maximum(jnp.sum(pr, axis=-1, keepdims=True), TINY)
    pr = pr.astype(BF16)
    acc = _dot_nt(pr[:, :w_buf], win_ref[NSA_KV:, :].astype(BF16)) + _dot(pr[:, w_buf:], kvw_new[:, NSA_KV:])
    o_win = own_half(acc / den)

    for h in range(NSA_HEADS):
        rs = slice(h * tq, (h + 1) * tq)
        o = (gates[:, 3 * h:3 * h + 1] * o_cmp[rs] + gates[:, 3 * h + 1:3 * h + 2] * o_slc[rs]
             + gates[:, 3 * h + 2:3 * h + 3] * o_win[rs])
        o_ref[:, h * HEAD_DIM:(h + 1) * HEAD_DIM] = o


def nsa_attend_decode(page_table, q, gates, kvs_new, kvw_new, win_state, kcmp_pool, slc_pool, layer, past):
    n_seq, n_pages = page_table.shape
    page = slc_pool.shape[3]
    bpp = page // NSA_BLOCK
    w_buf = win_state.shape[3]

    def seq_spec(width):
        return pl.BlockSpec((None, TQ_PAD, width), lambda b, pt: (b, 0, 0))

    def page_spec(shape, p, lead):
        return pl.BlockSpec((None, None) + shape, lambda b, pt: (lead, pt[b, p], 0, 0))

    assert page % NSA_BLOCK == 0 and TQ_PAD <= NSA_BLOCK and n_pages * bpp + 1 <= LANE
    kern = functools.partial(_nsa_decode_kernel, n_pages=n_pages, page=page, past=past)
    grid_spec = pltpu.PrefetchScalarGridSpec(
        num_scalar_prefetch=1,
        grid=(n_seq,),
        in_specs=([seq_spec(NSA_Q), seq_spec(LANE), seq_spec(KV_ROW), seq_spec(KV_ROW),
                   pl.BlockSpec((None, None, KV_ROW, w_buf), lambda b, pt: (layer, b, 0, 0))]
                  + [pl.BlockSpec((None, 1, bpp * KV_ROW), lambda b, pt, p=p: (pt[b, p], 0, 0)) for p in range(n_pages)]
                  + [page_spec((KV_ROW, page), p, layer) for p in range(n_pages)]),
        out_specs=seq_spec(NSA_Q),
        scratch_shapes=[pltpu.VMEM((n_pages * bpp, KV_ROW), F32)],
    )
    return pl.pallas_call(
        kern,
        grid_spec=grid_spec,
        out_shape=jax.ShapeDtypeStruct((n_seq, TQ_PAD, NSA_Q), F32),
        compiler_params=_params("arbitrary"),
        name="nsa_decode",
    )(page_table, q, gates, kvs_new, kvw_new, win_state, *([kcmp_pool] * n_pages), *([slc_pool] * n_pages))


def _lane_cumsum(x):
    n = x.shape[-1]
    lane = lax.broadcasted_iota(jnp.int32, x.shape, x.ndim - 1)
    s = 1
    while s < n:
        x = x + jnp.where(lane >= s, pltpu.roll(x, s, x.ndim - 1), 0.0)
        s *= 2
    return x


def _cumsum_kernel(x_ref, o_ref):
    o_ref[...] = _lane_cumsum(x_ref[...])


def cumsum_lanes(x):
    b, r, t = x.shape
    return pl.pallas_call(
        _cumsum_kernel,
        grid=(b,),
        in_specs=[pl.BlockSpec((None, r, t), lambda i: (i, 0, 0))],
        out_specs=pl.BlockSpec((None, r, t), lambda i: (i, 0, 0)),
        out_shape=jax.ShapeDtypeStruct(x.shape, F32),
        compiler_params=_params("parallel"),
        name="logf_cumsum",
    )(x)


def _fox_prompt_kernel(q_ref, kv_ref, cum_ref, cumq_ref, o_ref, *, tq, kc):
    i = pl.program_id(1)
    t0 = i * tq
    t = t0 + lax.broadcasted_iota(jnp.int32, (tq, 1), 0)
    n_chunks = (t0 + tq + kc - 1) // kc
    for h in range(FOX_HEADS):
        qh = (q_ref[:, h * HEAD_DIM:(h + 1) * HEAD_DIM] * SCALE).astype(BF16)
        cq = cumq_ref[:, h:h + 1]

        def body(j, carry):
            k0 = pl.multiple_of(j * kc, kc)
            kpos = k0 + lax.broadcasted_iota(jnp.int32, (1, kc), 1)
            kb = kv_ref[pl.ds(k0, kc), h * HEAD_DIM:(h + 1) * HEAD_DIM].astype(BF16)
            vb = kv_ref[pl.ds(k0, kc), FOX_W + h * HEAD_DIM:FOX_W + (h + 1) * HEAD_DIM].astype(BF16)
            decay = cq - cum_ref[j][h:h + 1, :]
            s = _dot_nt(qh, kb) + decay + jnp.where(kpos <= t, 0.0, NEG_INF)
            return _online_step(carry, s, vb)

        o = _online_finish(lax.fori_loop(0, n_chunks, body, _online_init(tq, HEAD_DIM)))
        o_ref[:, h * HEAD_DIM:(h + 1) * HEAD_DIM] = o


def fox_attend_prompt(qf, kvf, cum, cum_q, batch, seq, tq=256):
    tq = min(tq, seq)
    nq = seq // tq
    kern = functools.partial(_fox_prompt_kernel, tq=tq, kc=tq)
    return pl.pallas_call(
        kern,
        grid=(batch, nq),
        in_specs=[pl.BlockSpec((tq, FOX_W), lambda b, i: (b * nq + i, 0)),
                  pl.BlockSpec((seq, FOX_ROW), lambda b, i: (b, 0)),
                  pl.BlockSpec((None, nq, 8, tq), lambda b, i: (b, 0, 0, 0)),
                  pl.BlockSpec((tq, 8), lambda b, i: (b * nq + i, 0))],
        out_specs=pl.BlockSpec((tq, FOX_W), lambda b, i: (b * nq + i, 0)),
        out_shape=jax.ShapeDtypeStruct((batch * seq, FOX_W), F32),
        compiler_params=_params("parallel", "arbitrary"),
        name="fox_prompt",
    )(qf, kvf, jnp.swapaxes(cum.reshape(batch, 8, nq, tq), 1, 2), cum_q)


def _fox_decode_kernel(pt_ref, q_ref, kv_new_ref, lf_new_ref, *rest, n_pages, page, past):
    kv_refs = rest[:n_pages]
    lf_refs = rest[n_pages:2 * n_pages]
    o_ref = rest[2 * n_pages]
    del pt_ref
    tq = TQ_PAD
    rows = FOX_HEADS * tq
    row = lax.broadcasted_iota(jnp.int32, (rows, 1), 0)
    lane = lax.broadcasted_iota(jnp.int32, (rows, FOX_W), 1)
    own_head = (lane // HEAD_DIM) == (row // tq)
    t_rows = past + row % tq

    q = q_ref[...] * SCALE
    qbd = jnp.where(own_head, jnp.concatenate([q] * FOX_HEADS, axis=0), 0.0).astype(BF16)

    def head_rows(a):
        return jnp.concatenate(
            [jnp.broadcast_to(a[h:h + 1, :], (tq, a.shape[1])) for h in range(FOX_HEADS)], axis=0)

    lf_past = jnp.concatenate([r[...] for r in lf_refs], axis=1)
    cum_past = _lane_cumsum(lf_past)
    cum_new = cum_past[:, past - 1:past] + _lane_cumsum(lf_new_ref[...])
    ck_past = head_rows(cum_past)
    ck_new = head_rows(cum_new)
    new_lane = lax.broadcasted_iota(jnp.int32, (rows, page), 1)
    cq = jnp.sum(jnp.where(new_lane == row % tq, ck_new, 0.0), axis=-1, keepdims=True)

    kv_new = jnp.concatenate([kv_new_ref[...], jnp.zeros((page - tq, FOX_ROW), F32)], axis=0).astype(BF16)
    key_pos = lax.broadcasted_iota(jnp.int32, (1, past + page), 1)
    s = jnp.concatenate([_dot(qbd, kv_refs[p][:FOX_W, :].astype(BF16)) for p in range(n_pages)]
                        + [_dot_nt(qbd, kv_new[:, :FOX_W])], axis=1)
    s = s + (cq - jnp.concatenate([ck_past, ck_new], axis=1)) + jnp.where(key_pos <= t_rows, 0.0, NEG_INF)
    m = jnp.max(s, axis=-1, keepdims=True)
    pr = jnp.where(s > MASKED_BELOW, jnp.exp(s - m), 0.0)
    den = jnp.maximum(jnp.sum(pr, axis=-1, keepdims=True), TINY)
    pr = pr.astype(BF16)
    acc = _dot(pr[:, past:], kv_new[:, FOX_W:])
    for p in range(n_pages):
        acc = acc + _dot_nt(pr[:, p * page:(p + 1) * page], kv_refs[p][FOX_W:, :].astype(BF16))
    o = jnp.where(own_head, acc / den, 0.0)
    o_ref[...] = o[0:tq] + o[tq:2 * tq] + o[2 * tq:3 * tq] + o[3 * tq:4 * tq]


def fox_attend_decode(page_table, qf, kvf_new, lf_new_t, fox_pool, lf_pool_t, layer, past):
    n_seq, n_pages = page_table.shape
    page = fox_pool.shape[3]

    def seq_spec(r, width):
        return pl.BlockSpec((None, r, width), lambda b, pt: (b, 0, 0))

    def page_spec(shape, p):
        return pl.BlockSpec((None, None) + shape, lambda b, pt: (layer, pt[b, p], 0, 0))

    assert TQ_PAD <= page
    kern = functools.partial(_fox_decode_kernel, n_pages=n_pages, page=page, past=past)
    grid_spec = pltpu.PrefetchScalarGridSpec(
        num_scalar_prefetch=1,
        grid=(n_seq,),
        in_specs=([seq_spec(TQ_PAD, FOX_W), seq_spec(TQ_PAD, FOX_ROW), seq_spec(8, page)]
                  + [page_spec((FOX_ROW, page), p) for p in range(n_pages)]
                  + [page_spec((8, page), p) for p in range(n_pages)]),
        out_specs=seq_spec(TQ_PAD, FOX_W),
    )
    return pl.pallas_call(
        kern,
        grid_spec=grid_spec,
        out_shape=jax.ShapeDtypeStruct((n_seq, TQ_PAD, FOX_W), F32),
        compiler_params=_params("arbitrary"),
        name="fox_decode",
    )(page_table, qf, kvf_new, lf_new_t, *([fox_pool] * n_pages), *([lf_pool_t] * n_pages))


CONV_PAD = 32
CONV_SUB = 64


def _ln_silu(y, g, b):
    yc = y - jnp.mean(y, axis=-1, keepdims=True)
    var = jnp.mean(yc * yc, axis=-1, keepdims=True)
    return _silu(yc * lax.rsqrt(var + LN_EPS) * g + b)


def _conv_prompt_kernel(prev_ref, cur_ref, w_ref, b_ref, g_ref, bb_ref, o_ref, ctx_scr, *, tc):
    k = pl.program_id(1)
    ctx_scr[0:CONV_PAD, :] = jnp.where(k > 0, prev_ref[...], 0.0)
    ctx_scr[CONV_PAD:, :] = cur_ref[...]
    lead = CONV_PAD - (CONV_WIDTH - 1)
    for sub in range(tc // CONV_SUB):
        acc = jnp.zeros((CONV_SUB, CONV_CH), F32) + b_ref[...]
        for w in range(CONV_WIDTH):
            acc = acc + ctx_scr[pl.ds(sub * CONV_SUB + lead + w, CONV_SUB), :] * w_ref[w:w + 1, :]
        o_ref[sub * CONV_SUB:(sub + 1) * CONV_SUB, :] = _ln_silu(acc, g_ref[...], bb_ref[...])


def conv_module_prompt(glu, dw_w, dw_b, ln_g, ln_b, layer, batch, seq, tc=256):
    tc = min(tc, seq)
    nt = seq // tc
    per = tc // CONV_PAD

    def vec_spec():
        return pl.BlockSpec((None, 1, CONV_CH), lambda b, k: (layer, 0, 0))

    return pl.pallas_call(
        functools.partial(_conv_prompt_kernel, tc=tc),
        grid=(batch, nt),
        in_specs=[pl.BlockSpec((CONV_PAD, CONV_CH), lambda b, k: (jnp.maximum((b * nt + k) * per - 1, 0), 0)),
                  pl.BlockSpec((tc, CONV_CH), lambda b, k: (b * nt + k, 0)),
                  pl.BlockSpec((None, CONV_WIDTH, CONV_CH), lambda b, k: (layer, 0, 0)),
                  vec_spec(), vec_spec(), vec_spec()],
        out_specs=pl.BlockSpec((tc, CONV_CH), lambda b, k: (b * nt + k, 0)),
        out_shape=jax.ShapeDtypeStruct((batch * seq, CONV_CH), F32),
        scratch_shapes=[pltpu.VMEM((CONV_PAD + tc, CONV_CH), F32)],
        compiler_params=_params("parallel", "arbitrary"),
        name="conv_prompt",
    )(glu, glu, dw_w, dw_b, ln_g, ln_b)


def _conv_decode_kernel(state_ref, glu_ref, w_ref, b_ref, g_ref, bb_ref, o_ref):
    n_state = state_ref.shape[0]
    t_new = glu_ref.shape[0]
    ctx = [state_ref[i] for i in range(n_state)] + [glu_ref[i] for i in range(t_new)]
    for i in range(t_new):
        acc = jnp.zeros(ctx[0].shape, F32) + b_ref[...]
        for w in range(CONV_WIDTH):
            acc = acc + ctx[i + w] * w_ref[w:w + 1, :]
        o_ref[i] = _ln_silu(acc, g_ref[...], bb_ref[...])


def conv_module_decode(state_t, glu_t, dw_w, dw_b, ln_g, ln_b, layer):
    t_new, n_seq, _ = glu_t.shape

    def vec_spec():
        return pl.BlockSpec((None, 1, CONV_CH), lambda i: (layer, 0, 0))

    return pl.pallas_call(
        _conv_decode_kernel,
        grid=(1,),
        in_specs=[pl.BlockSpec(state_t.shape, lambda i: (0, 0, 0)),
                  pl.BlockSpec(glu_t.shape, lambda i: (0, 0, 0)),
                  pl.BlockSpec((None, CONV_WIDTH, CONV_CH), lambda i: (layer, 0, 0)),
                  vec_spec(), vec_spec(), vec_spec()],
        out_specs=pl.BlockSpec((t_new, n_seq, CONV_CH), lambda i: (0, 0, 0)),
        out_shape=jax.ShapeDtypeStruct((t_new, n_seq, CONV_CH), F32),
        compiler_params=_params("arbitrary"),
        name="conv_decode",
    )(state_t, glu_t, dw_w, dw_b, ln_g, ln_b)


def _merge_kernel(x_ref, ga_ref, on_ref, yc_ref, of_ref, mg_ref, wn_ref, wc_ref, wf_ref, wo_ref, o_ref):
    merged = (mg_ref[:, 0:D_MODEL] * _dot(on_ref[...].astype(BF16), wn_ref[...])
              + mg_ref[:, D_MODEL:2 * D_MODEL] * _dot(yc_ref[...].astype(BF16), wc_ref[...])
              + mg_ref[:, 2 * D_MODEL:] * _dot(of_ref[...].astype(BF16), wf_ref[...]))
    o_ref[...] = x_ref[...] + ga_ref[...] * _dot(merged.astype(BF16), wo_ref[...])


def merge_and_project(x, mod, o_nsa, y_conv, o_fox, mg, wb_nsa, wb_conv, wb_fox, w_out, layer, tm):
    m = x.shape[0]

    def row_spec(width):
        return pl.BlockSpec((tm, width), lambda i: (i, 0))

    def w_spec(rows):
        return pl.BlockSpec((None, rows, D_MODEL), lambda i: (layer, 0, 0))

    return pl.pallas_call(
        _merge_kernel,
        grid=(m // tm,),
        in_specs=[row_spec(D_MODEL), mod.spec(5, tm), row_spec(NSA_Q), row_spec(CONV_CH), row_spec(FOX_W),
                  row_spec(3 * D_MODEL), w_spec(NSA_Q), w_spec(CONV_CH), w_spec(FOX_W), w_spec(D_MODEL)],
        out_specs=row_spec(D_MODEL),
        out_shape=jax.ShapeDtypeStruct((m, D_MODEL), F32),
        compiler_params=_params("parallel"),
        name="merge",
    )(x, mod.arr, o_nsa, y_conv, o_fox, mg, wb_nsa, wb_conv, wb_fox, w_out)


def _final_norm_kernel(x_ref, g_ref, o_ref):
    x = x_ref[...]
    o_ref[...] = x * lax.rsqrt(jnp.mean(x * x, axis=-1, keepdims=True) + RMS_EPS) * g_ref[...]


def final_norm(x, g, tm):
    m = x.shape[0]
    return pl.pallas_call(
        _final_norm_kernel,
        grid=(m // tm,),
        in_specs=[pl.BlockSpec((tm, D_MODEL), lambda i: (i, 0)), pl.BlockSpec((1, D_MODEL), lambda i: (0, 0))],
        out_specs=pl.BlockSpec((tm, D_MODEL), lambda i: (i, 0)),
        out_shape=jax.ShapeDtypeStruct((m, D_MODEL), F32),
        compiler_params=_params("parallel"),
        name="final_norm",
    )(x, g.reshape(1, D_MODEL))


def _rope_tables(pos):
    half = HEAD_DIM // 2
    inv_freq = ROPE_THETA ** (-jnp.arange(half, dtype=F32) / half)
    ang = pos.astype(F32)[:, None] * inv_freq[None, :]
    cos, sin = jnp.cos(ang), jnp.sin(ang)
    cos = jnp.concatenate([cos, cos] * (LANE // HEAD_DIM), axis=1)
    sin = jnp.concatenate([-sin, sin] * (LANE // HEAD_DIM), axis=1)
    return cos, sin


def _aligned_in_proj(w_in, b_in):
    o_gn = NSA_Q + 3 * KV_ROW
    o_glu = o_gn + 3 * NSA_HEADS
    o_qkvf = o_glu + 2 * CONV_CH
    o_lf = o_qkvf + 3 * FOX_W
    o_mg = o_lf + FOX_HEADS

    def cols(a):
        pad_gn = jnp.zeros(a.shape[:-1] + (LANE - 3 * NSA_HEADS,), a.dtype)
        pad_lf = jnp.zeros(a.shape[:-1] + (LANE - FOX_HEADS,), a.dtype)
        return jnp.concatenate([a[..., :o_gn], a[..., o_glu:o_lf], a[..., o_mg:],
                                a[..., o_gn:o_glu], pad_gn, a[..., o_lf:o_mg], pad_lf], axis=-1)

    return cols(w_in).astype(BF16), cols(b_in)[:, None, :]


def _compress_weights(pe, wk, wv):
    pe2 = jnp.concatenate([pe, pe], axis=-1)

    def block_diag(w):
        z = jnp.zeros_like(w)
        return jnp.concatenate([jnp.concatenate([w, z], axis=-1), jnp.concatenate([z, w], axis=-1)],
                               axis=-2).astype(BF16)

    return pe2, block_diag(wk), block_diag(wv)


def _pad_tokens(a, n_seq, t_new):
    a = a.reshape(n_seq, t_new, a.shape[-1])
    return jnp.pad(a, ((0, 0), (0, TQ_PAD - t_new), (0, 0)))


def kernel(x_prompt, x_sample, cache_nsa_cmp_kv, cache_nsa_slc_kv, cache_fox_kv, cache_fox_logf,
           state_nsa_win_kv, state_conv, page_table, c_prompt, c_sample, w_ada, b_ada, norm_g,
           w_ffn_gu, w_ffn_down, w_in, b_in, nsa_cmp_pe, nsa_cmp_wk, nsa_cmp_wv, conv_dw_w, conv_dw_b,
           conv_ln_g, conv_ln_b, w_branch_nsa, w_branch_conv, w_branch_fox, w_out, final_norm_g):
    batch, seq, _ = x_prompt.shape
    n_seq, t_new, _ = x_sample.shape
    depth = w_ada.shape[0]
    n_pool, page = cache_nsa_cmp_kv.shape[1], cache_nsa_cmp_kv.shape[2]
    n_pages = page_table.shape[1]
    past = n_pages * page
    w_buf = state_nsa_win_kv.shape[2]
    mp, ms = batch * seq, n_seq * t_new
    tm_p = min(512, seq)
    tm_s = min(256, ms)

    w_ada_b = w_ada.astype(BF16)
    w_gu_b = w_ffn_gu.astype(BF16)
    w_down_b = w_ffn_down.astype(BF16)
    w_in_b, b_in_p = _aligned_in_proj(w_in, b_in)
    pe2, wk_bd, wv_bd = _compress_weights(nsa_cmp_pe, nsa_cmp_wk, nsa_cmp_wv)
    pe_t, wk_t, wv_t = _compress_weights(jnp.swapaxes(nsa_cmp_pe, 1, 2), jnp.swapaxes(nsa_cmp_wk, 1, 2),
                                         jnp.swapaxes(nsa_cmp_wv, 1, 2))
    wb_nsa_b, wb_conv_b = w_branch_nsa.astype(BF16), w_branch_conv.astype(BF16)
    wb_fox_b, w_out_b = w_branch_fox.astype(BF16), w_out.astype(BF16)
    norm_g4 = norm_g[:, :, None, :]
    dw_b3, ln_g3, ln_b3 = conv_dw_b[:, None, :], conv_ln_g[:, None, :], conv_ln_b[:, None, :]

    to_feature_major = (0, 1, 3, 4, 5, 2)
    cmp_pool = jnp.transpose(cache_nsa_cmp_kv, to_feature_major).reshape(depth, n_pool * KV_ROW, page)
    slc_pool = jnp.transpose(cache_nsa_slc_kv, to_feature_major).reshape(depth, n_pool, KV_ROW, page)
    fox_pool = jnp.transpose(cache_fox_kv, to_feature_major).reshape(depth, n_pool, FOX_ROW, page)
    lf_pool_t = jnp.pad(jnp.swapaxes(cache_fox_logf, 2, 3).astype(F32),
                        ((0, 0), (0, 0), (0, 8 - FOX_HEADS), (0, 0)))
    win_state = jnp.transpose(state_nsa_win_kv, to_feature_major).reshape(depth, n_seq, KV_ROW, w_buf)

    cos_p, sin_p = _rope_tables(jnp.arange(seq, dtype=jnp.int32))
    cos_s, sin_s = _rope_tables(past + jnp.arange(t_new, dtype=jnp.int32))
    cos_s, sin_s = jnp.tile(cos_s, (n_seq, 1)), jnp.tile(sin_s, (n_seq, 1))

    mod_all = ada_all_layers(jnp.concatenate([c_prompt, c_sample], axis=0), w_ada_b, b_ada)

    xp = x_prompt.reshape(mp, D_MODEL)
    xs = x_sample.reshape(ms, D_MODEL)
    out_p = [[] for _ in range(6)]
    out_s = [[] for _ in range(6)]

    for l in range(depth):
        mod_p = Mod(mod_all[l, :batch, None, :], seq)
        mod_s = Mod(jnp.repeat(mod_all[l, batch:], t_new, axis=0), 1)

        xp = ffn_half_step(xp, mod_p, 0, norm_g4, w_gu_b, w_down_b, l, 0, tm_p)
        xs = ffn_half_step(xs, mod_s, 0, norm_g4, w_gu_b, w_down_b, l, 0, tm_s)

        tm_in = min(256, seq)
        (q_p, kvc_p, kvs_p, kvw_p, glu_p, qf_p, kvf_p, mg_p, gn_p, lf_p,
         kvc_pt, kvs_pt, kvw_pt, kvf_pt, lf_pt) = input_projection(
            xp, mod_p, norm_g4, w_in_b, b_in_p, cos_p, sin_p, l, tm_in, seq // tm_in, feature_major_batch=batch)
        (q_s, kvc_s, kvs_s, kvw_s, glu_s, qf_s, kvf_s, mg_s, gn_s, lf_s) = input_projection(
            xs, mod_s, norm_g4, w_in_b, b_in_p, cos_s, sin_s, l, tm_s, ms // tm_s)

        kcmp_p = nsa_compress_rows(kvc_p.reshape(mp * 2, NSA_KV), pe2, wk_bd, wv_bd, l, 128)
        o_nsa_p = nsa_attend_prompt(q_p, gn_p, kcmp_p, kvs_p, kvw_p, batch, seq)
        cum_p = cumsum_lanes(lf_pt)
        cum_q = jnp.swapaxes(cum_p, 1, 2).reshape(mp, 8)
        o_fox_p = fox_attend_prompt(qf_p, kvf_p, cum_p, cum_q, batch, seq)
        y_conv_p = conv_module_prompt(glu_p, conv_dw_w, dw_b3, ln_g3, ln_b3, l, batch, seq)

        kcmp_pool = nsa_compress_pool(cmp_pool, pe_t, wk_t, wv_t, l, 128)
        o_nsa_s = nsa_attend_decode(page_table, _pad_tokens(q_s, n_seq, t_new), _pad_tokens(gn_s, n_seq, t_new),
                                    _pad_tokens(kvs_s, n_seq, t_new), _pad_tokens(kvw_s, n_seq, t_new),
                                    win_state, kcmp_pool, slc_pool, l, past)
        o_nsa_s = o_nsa_s[:, :t_new].reshape(ms, NSA_Q)
        lf_new_t = jnp.swapaxes(lf_s[:, :FOX_HEADS].reshape(n_seq, t_new, FOX_HEADS), 1, 2)
        lf_new_t = jnp.pad(lf_new_t, ((0, 0), (0, 8 - FOX_HEADS), (0, page - t_new)))
        o_fox_s = fox_attend_decode(page_table, _pad_tokens(qf_s, n_seq, t_new), _pad_tokens(kvf_s, n_seq, t_new),
                                    lf_new_t, fox_pool, lf_pool_t, l, past)
        o_fox_s = o_fox_s[:, :t_new].reshape(ms, FOX_W)
        state_t = jnp.swapaxes(state_conv[l], 0, 1)
        glu_t = jnp.swapaxes(glu_s.reshape(n_seq, t_new, CONV_CH), 0, 1)
        y_conv_s = conv_module_decode(state_t, glu_t, conv_dw_w, dw_b3, ln_g3, ln_b3, l)
        y_conv_s = jnp.swapaxes(y_conv_s, 0, 1).reshape(ms, CONV_CH)

        xp = merge_and_project(xp, mod_p, o_nsa_p, y_conv_p, o_fox_p, mg_p,
                               wb_nsa_b, wb_conv_b, wb_fox_b, w_out_b, l, tm_p)
        xs = merge_and_project(xs, mod_s, o_nsa_s, y_conv_s, o_fox_s, mg_s,
                               wb_nsa_b, wb_conv_b, wb_fox_b, w_out_b, l, tm_s)
        xp = ffn_half_step(xp, mod_p, 6, norm_g4, w_gu_b, w_down_b, l, 1, tm_p)
        xs = ffn_half_step(xs, mod_s, 6, norm_g4, w_gu_b, w_down_b, l, 1, tm_s)

        kv5 = (2, NSA_KV_HEADS, HEAD_DIM)
        glu_p3 = glu_p.reshape(batch, seq, CONV_CH)
        w_keep = min(NSA_WINDOW, seq)

        def token_major(a, heads):
            a = a.reshape(batch, 2, heads, HEAD_DIM, a.shape[-1])
            return jnp.transpose(a, (0, 4, 1, 2, 3))

        out_p[0].append(token_major(kvc_pt, NSA_KV_HEADS))
        out_p[1].append(token_major(kvs_pt, NSA_KV_HEADS))
        out_p[2].append(token_major(kvf_pt, FOX_HEADS))
        out_p[3].append(jnp.swapaxes(lf_pt[:, :FOX_HEADS], 1, 2))
        out_p[4].append(token_major(kvw_pt[:, :, seq - w_keep:], NSA_KV_HEADS))
        out_p[5].append(glu_p3[:, seq - (CONV_WIDTH - 1):])
        kvw_s5 = kvw_s.reshape((n_seq, t_new) + kv5)
        out_s[0].append(kvc_s.reshape((n_seq, t_new) + kv5))
        out_s[1].append(kvs_s.reshape((n_seq, t_new) + kv5))
        out_s[2].append(kvf_s.reshape(n_seq, t_new, 2, FOX_HEADS, HEAD_DIM))
        out_s[3].append(lf_s[:, :FOX_HEADS].reshape(n_seq, t_new, FOX_HEADS))
        out_s[4].append(jnp.concatenate([state_nsa_win_kv[l], kvw_s5], axis=1)[:, t_new:])
        conv_ctx = jnp.concatenate([state_conv[l], glu_s.reshape(n_seq, t_new, CONV_CH)], axis=1)
        out_s[5].append(conv_ctx[:, conv_ctx.shape[1] - (CONV_WIDTH - 1):])

    y_p = final_norm(xp, final_norm_g, tm_p).reshape(batch, seq, D_MODEL)
    y_s = final_norm(xs, final_norm_g, tm_s).reshape(n_seq, t_new, D_MODEL)
    outs = [y_p, y_s]
    for f in range(6):
        outs.append(jnp.stack(out_p[f]))
        outs.append(jnp.stack(out_s[f]))
    return tuple(outs)
```

```python
import functools

import jax
import jax.numpy as jnp
from jax import lax
from jax.experimental import pallas as pl
from jax.experimental.pallas import tpu as pltpu

D_MODEL = 1024
HEAD_DIM = 64
NSA_HEADS = 8
NSA_KV_HEADS = 2
NSA_GROUP = NSA_HEADS // NSA_KV_HEADS
NSA_BLOCK = 64
NSA_TOP_K = 16
NSA_WINDOW = 512
FOX_HEADS = 4
CONV_CH = 256
CONV_WIDTH = 31
D_FF = 2816
ROPE_THETA = 10000.0
RMS_EPS = 1e-6
LN_EPS = 1e-5
NEG_INF = -1e30
POS_INF = 1e30
TINY = 1e-30
MASKED_BELOW = -5e29
SCALE = HEAD_DIM ** -0.5

NSA_Q = NSA_HEADS * HEAD_DIM
NSA_KV = NSA_KV_HEADS * HEAD_DIM
FOX_W = FOX_HEADS * HEAD_DIM
KV_ROW = 2 * NSA_KV
FOX_ROW = 2 * FOX_W

LANE = 128
C_Q = 0
C_KVC = C_Q + NSA_Q
C_KVS = C_KVC + KV_ROW
C_KVW = C_KVS + KV_ROW
C_GLU = C_KVW + KV_ROW
C_QF = C_GLU + 2 * CONV_CH
C_KVF = C_QF + FOX_W
C_MG = C_KVF + FOX_ROW
C_GN = C_MG + 3 * D_MODEL
C_LF = C_GN + LANE
D_IN_PAD = C_LF + LANE

TQ_PAD = 8
VMEM_LIMIT = 56 * 1024 * 1024

F32 = jnp.float32
BF16 = jnp.bfloat16
NT = (((1,), (1,)), ((), ()))


def _params(*sem):
    return pltpu.CompilerParams(dimension_semantics=sem, vmem_limit_bytes=VMEM_LIMIT)


def _dot(a, b):
    return jnp.dot(a, b, preferred_element_type=F32)


def _dot_nt(a, b):
    return lax.dot_general(a, b, NT, preferred_element_type=F32)


def _rms_mod(x, g, shift, scale):
    y = x * lax.rsqrt(jnp.mean(x * x, axis=-1, keepdims=True) + RMS_EPS) * g
    return y * (1.0 + scale) + shift


def _sigmoid(x):
    return 1.0 / (1.0 + jnp.exp(-x))


def _silu(x):
    return x * _sigmoid(x)


def _ada_kernel(c_ref, w_ref, b_ref, o_ref):
    c = _silu(c_ref[...]).astype(BF16)
    o_ref[...] = _dot(c, w_ref[...]) + b_ref[...]


def ada_all_layers(cond, w_ada, b_ada):
    depth, _, n = w_ada.shape
    rows = cond.shape[0]
    tn = n // 8
    return pl.pallas_call(
        _ada_kernel,
        grid=(depth, n // tn),
        in_specs=[pl.BlockSpec((rows, D_MODEL), lambda l, j: (0, 0)),
                  pl.BlockSpec((None, D_MODEL, tn), lambda l, j: (l, 0, j)),
                  pl.BlockSpec((None, 1, tn), lambda l, j: (l, 0, j))],
        out_specs=pl.BlockSpec((None, rows, tn), lambda l, j: (l, 0, j)),
        out_shape=jax.ShapeDtypeStruct((depth, rows, n), F32),
        compiler_params=_params("parallel", "parallel"),
        name="ada",
    )(cond, w_ada, b_ada.reshape(depth, 1, n))


class Mod:
    def __init__(self, arr, tokens_per_cond):
        self.arr = arr
        self.tokens_per_cond = tokens_per_cond

    def spec(self, k, tm):
        if self.arr.ndim == 3:
            per = self.tokens_per_cond // tm
            return pl.BlockSpec((None, 1, D_MODEL), lambda m, *_: (m // per, 0, k))
        return pl.BlockSpec((tm, D_MODEL), lambda m, *_: (m, k))


def _ffn_kernel(x_ref, g_ref, sh_ref, sc_ref, ga_ref, wg_ref, wu_ref, wd_ref, o_ref, h_scr, acc_scr):
    j = pl.program_id(1)

    @pl.when(j == 0)
    def _():
        h_scr[...] = _rms_mod(x_ref[...], g_ref[...], sh_ref[...], sc_ref[...]).astype(BF16)
        acc_scr[...] = jnp.zeros_like(acc_scr)

    h = h_scr[...]
    gate = _dot(h, wg_ref[...])
    up = _dot(h, wu_ref[...])
    a = (_silu(gate) * up).astype(BF16)
    acc_scr[...] += _dot(a, wd_ref[...])

    @pl.when(j == pl.num_programs(1) - 1)
    def _():
        o_ref[...] = x_ref[...] + 0.5 * ga_ref[...] * acc_scr[...]


def ffn_half_step(x, mod, k0, norm_g, w_gu, w_down, layer, idx, tm):
    m = x.shape[0]
    nf = 2
    tf = D_FF // nf
    return pl.pallas_call(
        _ffn_kernel,
        grid=(m // tm, nf),
        in_specs=[pl.BlockSpec((tm, D_MODEL), lambda i, j: (i, 0)),
                  pl.BlockSpec((None, None, 1, D_MODEL), lambda i, j: (layer, 2 * idx, 0, 0)),
                  mod.spec(k0, tm), mod.spec(k0 + 1, tm), mod.spec(k0 + 2, tm),
                  pl.BlockSpec((None, None, D_MODEL, tf), lambda i, j: (layer, idx, 0, j)),
                  pl.BlockSpec((None, None, D_MODEL, tf), lambda i, j: (layer, idx, 0, nf + j)),
                  pl.BlockSpec((None, None, tf, D_MODEL), lambda i, j: (layer, idx, j, 0))],
        out_specs=pl.BlockSpec((tm, D_MODEL), lambda i, j: (i, 0)),
        out_shape=jax.ShapeDtypeStruct((m, D_MODEL), F32),
        scratch_shapes=[pltpu.VMEM((tm, D_MODEL), BF16), pltpu.VMEM((tm, D_MODEL), F32)],
        compiler_params=_params("parallel", "arbitrary"),
        name="ffn",
    )(x, norm_g, mod.arr, mod.arr, mod.arr, w_gu, w_gu, w_down)


def _rope(z, cos, sin):
    n = z.shape[1]
    lane = lax.broadcasted_iota(jnp.int32, z.shape, 1)
    first_half = (lane % HEAD_DIM) < (HEAD_DIM // 2)
    partner = jnp.where(first_half, pltpu.roll(z, n - HEAD_DIM // 2, 1), pltpu.roll(z, HEAD_DIM // 2, 1))
    reps = n // LANE
    if reps > 1:
        cos = jnp.concatenate([cos] * reps, axis=1)
        sin = jnp.concatenate([sin] * reps, axis=1)
    return z * cos + partner * sin


def _inproj_kernel(x_ref, g_ref, sh_ref, sc_ref, cos_ref, sin_ref, w_ref, b_ref,
                   q_ref, kvc_ref, kvs_ref, kvw_ref, glu_ref, qf_ref, kvf_ref, mg_ref, gn_ref, lf_ref,
                   *feature_major_refs):
    u = _rms_mod(x_ref[...], g_ref[...], sh_ref[...], sc_ref[...]).astype(BF16)
    cos = cos_ref[...]
    sin = sin_ref[...]

    def proj(lo, hi):
        return _dot(u, w_ref[:, lo:hi]) + b_ref[:, lo:hi]

    def as_operand(ref, val, scale=None):
        if ref.dtype == BF16 and scale is not None:
            val = val * scale
        ref[...] = val.astype(ref.dtype)

    as_operand(q_ref, _rope(proj(C_Q, C_KVC), cos, sin), SCALE)
    kv_rows = []
    for ref, lo in ((kvc_ref, C_KVC), (kvs_ref, C_KVS), (kvw_ref, C_KVW)):
        z = proj(lo, lo + KV_ROW)
        z = jnp.concatenate([_rope(z[:, :NSA_KV], cos, sin), z[:, NSA_KV:]], axis=1)
        as_operand(ref, z)
        kv_rows.append(z)
    z = proj(C_GLU, C_QF)
    glu_ref[...] = z[:, :CONV_CH] * _sigmoid(z[:, CONV_CH:])
    as_operand(qf_ref, proj(C_QF, C_KVF), SCALE)
    kvf = proj(C_KVF, C_MG)
    as_operand(kvf_ref, kvf)
    mg_ref[...] = _sigmoid(proj(C_MG, C_GN))
    gn_ref[...] = _sigmoid(proj(C_GN, C_LF))
    z = proj(C_LF, D_IN_PAD)
    lf = jnp.minimum(z, 0.0) - jnp.log(1.0 + jnp.exp(-jnp.abs(z)))
    lf_ref[...] = lf
    if feature_major_refs:
        kvc_t, kvs_t, kvw_t, kvf_t, lf_t, kvs_c, kvw_c, kvf_c = feature_major_refs
        for ref, rows, chunked in ((kvc_t, kv_rows[0], None), (kvs_t, kv_rows[1], kvs_c),
                                   (kvw_t, kv_rows[2], kvw_c), (kvf_t, kvf, kvf_c)):
            rows_t = rows.T
            ref[...] = rows_t
            if chunked is not None:
                for c in range(chunked.shape[0]):
                    chunked[c] = rows_t[:, c * LANE:(c + 1) * LANE].astype(BF16)
        lf_t[...] = lf.T[:lf_t.shape[0], :]


def input_projection(x, mod, norm_g, w_in, b_in, cos, sin, layer, tm, pos_tiles, feature_major_batch=None):
    m = x.shape[0]
    widths = (NSA_Q, KV_ROW, KV_ROW, KV_ROW, CONV_CH, FOX_W, FOX_ROW, 3 * D_MODEL, LANE, LANE)
    out_specs = [pl.BlockSpec((tm, w), lambda i: (i, 0)) for w in widths]
    operand = (0, 2, 3, 5, 6) if feature_major_batch else ()
    out_shape = [jax.ShapeDtypeStruct((m, w), BF16 if k in operand else F32) for k, w in enumerate(widths)]
    if feature_major_batch:
        t = m // feature_major_batch
        for rows in (KV_ROW, KV_ROW, KV_ROW, FOX_ROW, 8):
            out_specs.append(pl.BlockSpec((None, rows, tm), lambda i: (i // pos_tiles, 0, i % pos_tiles)))
            out_shape.append(jax.ShapeDtypeStruct((feature_major_batch, rows, t), F32))
        for rows in (KV_ROW, KV_ROW, FOX_ROW):
            out_specs.append(pl.BlockSpec((None, tm // LANE, rows, LANE),
                                          lambda i: (i // pos_tiles, i % pos_tiles, 0, 0)))
            out_shape.append(jax.ShapeDtypeStruct((feature_major_batch, t // LANE, rows, LANE), BF16))
    return pl.pallas_call(
        _inproj_kernel,
        grid=(m // tm,),
        in_specs=[pl.BlockSpec((tm, D_MODEL), lambda i: (i, 0)),
                  pl.BlockSpec((None, None, 1, D_MODEL), lambda i: (layer, 1, 0, 0)),
                  mod.spec(3, tm), mod.spec(4, tm),
                  pl.BlockSpec((tm, LANE), lambda i: (i % pos_tiles, 0)),
                  pl.BlockSpec((tm, LANE), lambda i: (i % pos_tiles, 0)),
                  pl.BlockSpec((None, D_MODEL, D_IN_PAD), lambda i: (layer, 0, 0)),
                  pl.BlockSpec((None, 1, D_IN_PAD), lambda i: (layer, 0, 0))],
        out_specs=out_specs,
        out_shape=out_shape,
        compiler_params=_params("parallel"),
        name="inproj",
    )(x, norm_g, mod.arr, mod.arr, cos, sin, w_in, b_in)


def _compress_kernel(x_ref, pe_ref, wk_ref, wv_ref, o_ref):
    nblk = o_ref.shape[0]
    acc_k = jnp.zeros((nblk, NSA_KV), F32)
    acc_v = jnp.zeros((nblk, NSA_KV), F32)
    for l in range(NSA_BLOCK):
        pe = pe_ref[l:l + 1, :]
        k_rows = x_ref[pl.ds(2 * l, nblk, stride=2 * NSA_BLOCK), :] + pe
        v_rows = x_ref[pl.ds(2 * l + 1, nblk, stride=2 * NSA_BLOCK), :] + pe
        acc_k = acc_k + _dot(k_rows.astype(BF16), wk_ref[l])
        acc_v = acc_v + _dot(v_rows.astype(BF16), wv_ref[l])
    o_ref[:, :NSA_KV] = acc_k
    o_ref[:, NSA_KV:] = acc_v


def nsa_compress_rows(rows, pe2, wk_bd, wv_bd, layer, blocks_per_step):
    lead = rows.shape[:-2]
    nblk = rows.shape[-2] // (2 * NSA_BLOCK)
    nb = max(d for d in range(8, blocks_per_step + 1, 8) if nblk % d == 0)
    if lead:
        x_spec = pl.BlockSpec((None, nb * 2 * NSA_BLOCK, NSA_KV), lambda i: (layer, i, 0))
    else:
        x_spec = pl.BlockSpec((nb * 2 * NSA_BLOCK, NSA_KV), lambda i: (i, 0))
    w_spec = pl.BlockSpec((None, NSA_BLOCK, NSA_KV, NSA_KV), lambda i: (layer, 0, 0, 0))
    return pl.pallas_call(
        _compress_kernel,
        grid=(nblk // nb,),
        in_specs=[x_spec, pl.BlockSpec((None, NSA_BLOCK, NSA_KV), lambda i: (layer, 0, 0)), w_spec, w_spec],
        out_specs=pl.BlockSpec((nb, KV_ROW), lambda i: (i, 0)),
        out_shape=jax.ShapeDtypeStruct((nblk, KV_ROW), F32),
        compiler_params=_params("parallel"),
        name="nsa_compress",
    )(rows, pe2, wk_bd, wv_bd)


def _compress_pool_kernel(x_ref, pe_ref, wk_ref, wv_ref, o_ref, *, blocks_per_page):
    pages = o_ref.shape[0]
    page = x_ref.shape[1]
    for c in range(2 * NSA_KV_HEADS):
        w_ref = wk_ref if c < NSA_KV_HEADS else wv_ref
        acc = jnp.zeros((pages, page), F32)
        for d in range(HEAD_DIM):
            rows = x_ref[pl.ds(c * HEAD_DIM + d, pages, stride=KV_ROW), :] + pe_ref[d:d + 1, :]
            acc = acc + _dot(rows.astype(BF16), w_ref[d])
        for j in range(blocks_per_page):
            o_ref[:, j * KV_ROW + c * HEAD_DIM:j * KV_ROW + (c + 1) * HEAD_DIM] = acc[:, j * HEAD_DIM:(j + 1) * HEAD_DIM]


def nsa_compress_pool(pool_t, pe_t, wk_t, wv_t, layer, pages_per_step):
    page = pool_t.shape[-1]
    n_pool = pool_t.shape[1] // KV_ROW
    bpp = page // NSA_BLOCK
    assert page == LANE and bpp * HEAD_DIM == page
    pp = max(d for d in range(8, pages_per_step + 1, 8) if n_pool % d == 0)
    w_spec = pl.BlockSpec((None, HEAD_DIM, page, page), lambda i: (layer, 0, 0, 0))
    out = pl.pallas_call(
        functools.partial(_compress_pool_kernel, blocks_per_page=bpp),
        grid=(n_pool // pp,),
        in_specs=[pl.BlockSpec((None, pp * KV_ROW, page), lambda i: (layer, i, 0)),
                  pl.BlockSpec((None, HEAD_DIM, page), lambda i: (layer, 0, 0)), w_spec, w_spec],
        out_specs=pl.BlockSpec((pp, bpp * KV_ROW), lambda i: (i, 0)),
        out_shape=jax.ShapeDtypeStruct((n_pool, bpp * KV_ROW), F32),
        compiler_params=_params("parallel"),
        name="nsa_compress_pool",
    )(pool_t, pe_t, wk_t, wv_t)
    return out.reshape(n_pool, 1, bpp * KV_ROW)


def _softmax_rows(logits):
    m = jnp.max(logits, axis=-1, keepdims=True)
    p = jnp.where(logits > MASKED_BELOW, jnp.exp(logits - m), 0.0)
    return p / jnp.maximum(jnp.sum(p, axis=-1, keepdims=True), TINY)


STAT_ROWS = 8


def _flash_init(m_scr, l_scr, acc_scr):
    m_scr[...] = jnp.full(m_scr.shape, NEG_INF, F32)
    l_scr[...] = jnp.zeros(l_scr.shape, F32)
    acc_scr[...] = jnp.zeros(acc_scr.shape, F32)


def _flash_stats(s_t, m_scr, l_scr):
    m = m_scr[0:1, :]
    m_new = jnp.maximum(m, jnp.max(s_t, axis=0, keepdims=True))
    alpha = jnp.exp(m - m_new)
    p = jnp.exp(s_t - jnp.where(m_new > MASKED_BELOW, m_new, 0.0))
    l_scr[0:1, :] = alpha * l_scr[0:1, :] + jnp.sum(p, axis=0, keepdims=True)
    m_scr[0:1, :] = m_new
    return p.astype(BF16), alpha


def _select_blocks(score, t, nb):
    blk = lax.broadcasted_iota(jnp.int32, score.shape, 1)
    cur = t // NSA_BLOCK
    forced = jnp.where(blk == 0, 1, jnp.where(blk == cur, 1, jnp.where(blk == cur - 1, 1, 0)))
    s = jnp.where(forced == 1, POS_INF, jnp.where(blk > cur, NEG_INF, score))
    rank = jnp.zeros(score.shape, jnp.int32)
    for i in range(nb):
        col = s[:, i:i + 1]
        ahead = jnp.where(col > s, 1, jnp.where(col == s, jnp.where(blk > i, 1, 0), 0))
        rank = rank + ahead
    k_sel = min(NSA_TOP_K, nb)
    return jnp.where(rank < k_sel, jnp.where(blk < nb, 1.0, 0.0), 0.0)


def _expand_blocks(sel, k0, kc):
    width = sel.shape[1]
    kblk = (k0 + lax.broadcasted_iota(jnp.int32, (width, kc), 1)) // NSA_BLOCK
    e = jnp.where(kblk == lax.broadcasted_iota(jnp.int32, (width, kc), 0), 1.0, 0.0).astype(BF16)
    return _dot(sel.astype(BF16), e)


def _select_blocks_t(score_t, t_row, nb):
    blk = lax.broadcasted_iota(jnp.int32, score_t.shape, 0)
    cur = t_row // NSA_BLOCK
    forced = jnp.where(blk == 0, 1, jnp.where(blk == cur, 1, jnp.where(blk == cur - 1, 1, 0)))
    s = jnp.where(forced == 1, POS_INF, jnp.where(blk > cur, NEG_INF, score_t))
    rank = jnp.zeros(score_t.shape, jnp.int32)
    for i in range(nb):
        other = s[i:i + 1, :]
        rank = rank + jnp.where(other > s, 1, jnp.where(other == s, jnp.where(blk > i, 1, 0), 0))
    return jnp.where(rank < min(NSA_TOP_K, nb), 1.0, 0.0)


def _nsa_prompt_kernel(q_ref, gate_ref, kcmp_ref, kvs_ref, kvs_t_ref, kvw_ref, kvw_t_ref, o_ref,
                       qbd_scr, expand_scr, bias_scr, m_scr, l_scr, acc_scr, out_scr, *, tq):
    i = pl.program_id(1)
    t0 = i * tq
    t_row = t0 + lax.broadcasted_iota(jnp.int32, (1, tq), 1)
    nc = kcmp_ref.shape[0]
    lane = lax.broadcasted_iota(jnp.int32, (tq, LANE), 1)
    low_half = lane < HEAD_DIM

    pieces = []
    for h in range(NSA_HEADS):
        g = h // NSA_GROUP
        pair = q_ref[:, (h // 2) * LANE:(h // 2 + 1) * LANE].astype(F32)
        if h % 2 != g:
            pair = pltpu.roll(pair, HEAD_DIM, 1)
        pieces.append(jnp.where(low_half, pair, 0.0) if g == 0 else jnp.where(low_half, 0.0, pair))
    qbd_scr[...] = jnp.concatenate(pieces, axis=0).astype(BF16)
    qbd = qbd_scr[...]
    gates_t = gate_ref[...].T

    kcm = kcmp_ref[:, :NSA_KV].astype(BF16)
    vcm_t = jnp.concatenate([kcmp_ref[:, NSA_KV:], jnp.zeros((LANE - nc, NSA_KV), F32)], axis=0).T
    blk_t = lax.broadcasted_iota(jnp.int32, (nc, 1), 0)
    cmp_bias_t = jnp.where((blk_t + 1) * NSA_BLOCK - 1 <= t_row, 0.0, NEG_INF)
    lc_t = _dot_nt(kcm, qbd) + jnp.concatenate([cmp_bias_t] * NSA_HEADS, axis=1)
    m_t = jnp.max(lc_t, axis=0, keepdims=True)
    p_t = jnp.where(lc_t > MASKED_BELOW, jnp.exp(lc_t - m_t), 0.0)
    pc_t = p_t / jnp.maximum(jnp.sum(p_t, axis=0, keepdims=True), TINY)
    o_cmp_t = _dot(vcm_t[:, :nc].astype(BF16), pc_t.astype(BF16))
    for h in range(NSA_HEADS):
        g = h // NSA_GROUP
        out_scr[h * HEAD_DIM:(h + 1) * HEAD_DIM, :] = (
            gates_t[3 * h:3 * h + 1, :] * o_cmp_t[g * HEAD_DIM:(g + 1) * HEAD_DIM, h * tq:(h + 1) * tq])

    r_key = lax.broadcasted_iota(jnp.int32, (tq, tq), 0)
    c_qry = lax.broadcasted_iota(jnp.int32, (tq, tq), 1)
    sels = []
    for g in range(NSA_KV_HEADS):
        base = g * NSA_GROUP * tq
        score_t = (pc_t[:, base:base + tq] + pc_t[:, base + tq:base + 2 * tq]
                   + pc_t[:, base + 2 * tq:base + 3 * tq] + pc_t[:, base + 3 * tq:base + 4 * tq])
        sel_t = _select_blocks_t(score_t, t_row, nc)
        sels.append(jnp.concatenate([sel_t, jnp.zeros((LANE - nc, tq), F32)], axis=0))
    sel_t = jnp.concatenate(sels, axis=1).astype(BF16)

    n_chunks = bias_scr.shape[1]
    span = min(4, n_chunks)

    @pl.when(i == 0)
    def _():
        kblk = lax.broadcasted_iota(jnp.int32, expand_scr.shape, 0) // NSA_BLOCK
        expand_scr[...] = jnp.where(kblk == lax.broadcasted_iota(jnp.int32, expand_scr.shape, 1),
                                    1.0, 0.0).astype(BF16)

    t_rows = jnp.concatenate([t_row] * NSA_KV_HEADS, axis=1)
    for k in range(n_chunks // span):
        @pl.when(i + 1 >= k * span)
        def _(k=k):
            first = k * span * tq
            ok = _dot(expand_scr[first:first + span * tq, :], sel_t)
            kpos = first + lax.broadcasted_iota(jnp.int32, (span * tq, 1), 0)
            bias = jnp.where(kpos <= t_rows, jnp.where(ok > 0.5, 0.0, NEG_INF), NEG_INF)
            for jj in range(span):
                for g in range(NSA_KV_HEADS):
                    bias_scr[g, k * span + jj] = bias[jj * tq:(jj + 1) * tq, g * tq:(g + 1) * tq]

    group_w = NSA_GROUP * tq

    def attend_chunk(keys, vals_t, bias_per_group):
        s_t = _dot_nt(keys, qbd_scr[...])
        if bias_per_group is not None:
            s_t = s_t + jnp.concatenate([b for b in bias_per_group for _ in range(NSA_GROUP)], axis=1)
        p, alpha = _flash_stats(s_t, m_scr, l_scr)
        for g in range(NSA_KV_HEADS):
            cols = slice(g * group_w, (g + 1) * group_w)
            rows = slice(NSA_KV + g * HEAD_DIM, NSA_KV + (g + 1) * HEAD_DIM)
            v_t = jnp.concatenate([v[rows, :] for v in vals_t], axis=1)
            acc_scr[g] = alpha[:, cols] * acc_scr[g] + _dot(v_t, p[:, cols])

    def finish_branch(branch):
        inv_l = 1.0 / jnp.maximum(l_scr[0:1, :], TINY)
        for h in range(NSA_HEADS):
            g, r = divmod(h, NSA_GROUP)
            rs = slice(h * HEAD_DIM, (h + 1) * HEAD_DIM)
            o_t = acc_scr[g, :, r * tq:(r + 1) * tq] * inv_l[:, h * tq:(h + 1) * tq]
            out_scr[rs, :] = out_scr[rs, :] + gates_t[3 * h + branch:3 * h + branch + 1, :] * o_t

    _flash_init(m_scr, l_scr, acc_scr)
    pair = min(2, n_chunks)

    def slc_body(jp, _):
        j = jp * pair
        k0 = pl.multiple_of(j * tq, pair * tq)
        attend_chunk(kvs_ref[pl.ds(k0, pair * tq), :NSA_KV], [kvs_t_ref.at[j + d] for d in range(pair)],
                     [jnp.concatenate([bias_scr[g, j + d] for d in range(pair)], axis=0)
                      for g in range(NSA_KV_HEADS)])
        return 0

    lax.fori_loop(0, i // pair + 1, slc_body, 0)
    finish_branch(1)

    _flash_init(m_scr, l_scr, acc_scr)
    n_win = NSA_WINDOW // tq + 1
    for c in range(n_win):
        jc = i - (n_win - 1) + c

        @pl.when(jc >= 0)
        def _(c=c, jc=jc):
            k0 = pl.multiple_of(jc * tq, tq)
            bias = None
            if c == 0:
                bias = jnp.where(r_key > c_qry, 0.0, NEG_INF)
            if c == n_win - 1:
                tail = jnp.where(r_key <= c_qry, 0.0, NEG_INF)
                bias = tail if bias is None else bias + tail
            attend_chunk(kvw_ref[pl.ds(k0, tq), :NSA_KV], [kvw_t_ref.at[jc]],
                         None if bias is None else [bias] * NSA_KV_HEADS)

    finish_branch(2)
    o_ref[...] = out_scr[...].T


def nsa_attend_prompt(q, gates, kcmp, kvs, kvs_t, kvw, kvw_t, batch, seq, tq=128):
    nq = seq // tq
    nc = seq // NSA_BLOCK
    assert nc <= LANE and NSA_WINDOW % tq == 0 and tq == LANE and nq % min(4, nq) == 0
    kern = functools.partial(_nsa_prompt_kernel, tq=tq)
    return pl.pallas_call(
        kern,
        grid=(batch, nq),
        in_specs=[pl.BlockSpec((tq, NSA_Q), lambda b, i: (b * nq + i, 0)),
                  pl.BlockSpec((tq, LANE), lambda b, i: (b * nq + i, 0)),
                  pl.BlockSpec((nc, KV_ROW), lambda b, i: (b, 0)),
                  pl.BlockSpec((seq, KV_ROW), lambda b, i: (b, 0)),
                  pl.BlockSpec((None, nq, KV_ROW, tq), lambda b, i: (b, 0, 0, 0)),
                  pl.BlockSpec((seq, KV_ROW), lambda b, i: (b, 0)),
                  pl.BlockSpec((None, nq, KV_ROW, tq), lambda b, i: (b, 0, 0, 0))],
        out_specs=pl.BlockSpec((tq, NSA_Q), lambda b, i: (b * nq + i, 0)),
        out_shape=jax.ShapeDtypeStruct((batch * seq, NSA_Q), F32),
        scratch_shapes=[pltpu.VMEM((NSA_HEADS * tq, LANE), BF16),
                        pltpu.VMEM((seq, LANE), BF16),
                        pltpu.VMEM((NSA_KV_HEADS, nq, tq, tq), F32),
                        pltpu.VMEM((STAT_ROWS, NSA_HEADS * tq), F32),
                        pltpu.VMEM((STAT_ROWS, NSA_HEADS * tq), F32),
                        pltpu.VMEM((NSA_KV_HEADS, HEAD_DIM, NSA_GROUP * tq), F32),
                        pltpu.VMEM((NSA_Q, tq), F32)],
        compiler_params=_params("parallel", "arbitrary"),
        name="nsa_prompt",
    )(q, gates, kcmp, kvs, kvs_t, kvw, kvw_t)


def _nsa_decode_kernel(pt_ref, q_ref, gate_ref, kvs_new_ref, kvw_new_ref, win_ref, *rest, n_pages, page, past):
    kcmp_refs = rest[:n_pages]
    kvs_refs = rest[n_pages:2 * n_pages]
    o_ref = rest[2 * n_pages]
    kcmp_scr = rest[2 * n_pages + 1]
    del pt_ref
    tq = TQ_PAD
    rows = NSA_HEADS * tq
    blocks_per_page = page // NSA_BLOCK
    nc = n_pages * blocks_per_page
    nb = nc + 1
    for p in range(n_pages):
        for j in range(blocks_per_page):
            kcmp_scr[p * blocks_per_page + j:p * blocks_per_page + j + 1, :] = (
                kcmp_refs[p][:, j * KV_ROW:(j + 1) * KV_ROW])

    row = lax.broadcasted_iota(jnp.int32, (rows, 1), 0)
    t_rows = past + row % tq
    t1 = past + lax.broadcasted_iota(jnp.int32, (tq, 1), 0)
    lane = lax.broadcasted_iota(jnp.int32, (rows, LANE), 1)
    own_group = (lane // HEAD_DIM) == (row // (NSA_GROUP * tq))

    q = q_ref[...] * SCALE
    pieces = []
    for h in range(NSA_HEADS):
        qh = q[:, h * HEAD_DIM:(h + 1) * HEAD_DIM]
        zero = jnp.zeros_like(qh)
        pieces.append(jnp.concatenate([qh, zero] if h < NSA_GROUP else [zero, qh], axis=1))
    qbd = jnp.concatenate(pieces, axis=0).astype(BF16)

    def own_half(o):
        o = jnp.where(own_group, o, 0.0)
        return o[:, :HEAD_DIM] + o[:, HEAD_DIM:]

    kcm = kcmp_scr[:, :NSA_KV].astype(BF16)
    vcm = kcmp_scr[:, NSA_KV:].astype(BF16)
    blk = lax.broadcasted_iota(jnp.int32, (1, nc), 1)
    lc = _dot_nt(qbd, kcm) + jnp.where((blk + 1) * NSA_BLOCK - 1 <= t_rows, 0.0, NEG_INF)
    pc = _softmax_rows(lc)
    o_cmp = own_half(_dot(pc.astype(BF16), vcm))

    gates = gate_ref[...]
    new_k_pos = past + lax.broadcasted_iota(jnp.int32, (1, page), 1)
    pad = jnp.zeros((page - tq, KV_ROW), F32)
    kvs_new = jnp.concatenate([kvs_new_ref[...], pad], axis=0).astype(BF16)
    kvw_new = jnp.concatenate([kvw_new_ref[...], pad], axis=0).astype(BF16)

    n_keys = (n_pages + 1) * page
    key_pos = lax.broadcasted_iota(jnp.int32, (1, n_keys), 1)
    sels = []
    for g in range(NSA_KV_HEADS):
        pcg = pc[g * NSA_GROUP * tq:(g + 1) * NSA_GROUP * tq]
        score = pcg[0:tq] + pcg[tq:2 * tq] + pcg[2 * tq:3 * tq] + pcg[3 * tq:4 * tq]
        score = jnp.concatenate([score, jnp.zeros((tq, LANE - nc), F32)], axis=1)
        sels.append(_select_blocks(score, t1, nb))
    ok = _expand_blocks(jnp.concatenate(sels, axis=0), 0, n_keys)
    t2 = jnp.concatenate([t1] * NSA_KV_HEADS, axis=0)
    bias2 = jnp.where(key_pos <= t2, jnp.where(ok > 0.5, 0.0, NEG_INF), NEG_INF)
    bias = jnp.concatenate(
        [bias2[g * tq:(g + 1) * tq] for g in range(NSA_KV_HEADS) for _ in range(NSA_GROUP)], axis=0)
    s = jnp.concatenate([_dot(qbd, kvs_refs[p][:NSA_KV, :].astype(BF16)) for p in range(n_pages)]
                        + [_dot_nt(qbd, kvs_new[:, :NSA_KV])], axis=1) + bias
    m = jnp.max(s, axis=-1, keepdims=True)
    pr = jnp.where(s > MASKED_BELOW, jnp.exp(s - m), 0.0)
    den = jnp.maximum(jnp.sum(pr, axis=-1, keepdims=True), TINY)
    pr = pr.astype(BF16)
    acc = _dot(pr[:, n_pages * page:], kvs_new[:, NSA_KV:])
    for p in range(n_pages):
        acc = acc + _dot_nt(pr[:, p * page:(p + 1) * page], kvs_refs[p][NSA_KV:, :].astype(BF16))
    o_slc = own_half(acc / den)

    w_buf = win_ref.shape[1]
    wpos = jnp.concatenate([past - w_buf + lax.broadcasted_iota(jnp.int32, (1, w_buf), 1), new_k_pos], axis=1)
    ok = jnp.where(wpos <= t_rows, jnp.where(wpos > t_rows - NSA_WINDOW, jnp.where(wpos >= 0, 1.0, 0.0), 0.0), 0.0)
    s = jnp.concatenate([_dot(qbd, win_ref[:NSA_KV, :].astype(BF16)), _dot_nt(qbd, kvw_new[:, :NSA_KV])], axis=1)
    s = s + jnp.where(ok > 0.5, 0.0, NEG_INF)
    m = jnp.max(s, axis=-1, keepdims=True)
    pr = jnp.where(s > MASKED_BELOW, jnp.exp(s - m), 0.0)
    den = jnp.maximum(jnp.sum(pr, axis=-1, keepdims=True), TINY)
    pr = pr.astype(BF16)
    acc = _dot_nt(pr[:, :w_buf], win_ref[NSA_KV:, :].astype(BF16)) + _dot(pr[:, w_buf:], kvw_new[:, NSA_KV:])
    o_win = own_half(acc / den)

    for h in range(NSA_HEADS):
        rs = slice(h * tq, (h + 1) * tq)
        o = (gates[:, 3 * h:3 * h + 1] * o_cmp[rs] + gates[:, 3 * h + 1:3 * h + 2] * o_slc[rs]
             + gates[:, 3 * h + 2:3 * h + 3] * o_win[rs])
        o_ref[:, h * HEAD_DIM:(h + 1) * HEAD_DIM] = o


def nsa_attend_decode(page_table, q, gates, kvs_new, kvw_new, win_state, kcmp_pool, slc_pool, layer, past):
    n_seq, n_pages = page_table.shape
    page = slc_pool.shape[3]
    bpp = page // NSA_BLOCK
    w_buf = win_state.shape[3]

    def seq_spec(width):
        return pl.BlockSpec((None, TQ_PAD, width), lambda b, pt: (b, 0, 0))

    def page_spec(shape, p, lead):
        return pl.BlockSpec((None, None) + shape, lambda b, pt: (lead, pt[b, p], 0, 0))

    assert page % NSA_BLOCK == 0 and TQ_PAD <= NSA_BLOCK and n_pages * bpp + 1 <= LANE
    kern = functools.partial(_nsa_decode_kernel, n_pages=n_pages, page=page, past=past)
    grid_spec = pltpu.PrefetchScalarGridSpec(
        num_scalar_prefetch=1,
        grid=(n_seq,),
        in_specs=([seq_spec(NSA_Q), seq_spec(LANE), seq_spec(KV_ROW), seq_spec(KV_ROW),
                   pl.BlockSpec((None, None, KV_ROW, w_buf), lambda b, pt: (layer, b, 0, 0))]
                  + [pl.BlockSpec((None, 1, bpp * KV_ROW), lambda b, pt, p=p: (pt[b, p], 0, 0)) for p in range(n_pages)]
                  + [page_spec((KV_ROW, page), p, layer) for p in range(n_pages)]),
        out_specs=seq_spec(NSA_Q),
        scratch_shapes=[pltpu.VMEM((n_pages * bpp, KV_ROW), F32)],
    )
    return pl.pallas_call(
        kern,
        grid_spec=grid_spec,
        out_shape=jax.ShapeDtypeStruct((n_seq, TQ_PAD, NSA_Q), F32),
        compiler_params=_params("arbitrary"),
        name="nsa_decode",
    )(page_table, q, gates, kvs_new, kvw_new, win_state, *([kcmp_pool] * n_pages), *([slc_pool] * n_pages))


def _lane_cumsum(x):
    n = x.shape[-1]
    lane = lax.broadcasted_iota(jnp.int32, x.shape, x.ndim - 1)
    s = 1
    while s < n:
        x = x + jnp.where(lane >= s, pltpu.roll(x, s, x.ndim - 1), 0.0)
        s *= 2
    return x


def _cumsum_kernel(x_ref, o_ref):
    o_ref[...] = _lane_cumsum(x_ref[...])


def cumsum_lanes(x):
    b, r, t = x.shape
    return pl.pallas_call(
        _cumsum_kernel,
        grid=(b,),
        in_specs=[pl.BlockSpec((None, r, t), lambda i: (i, 0, 0))],
        out_specs=pl.BlockSpec((None, r, t), lambda i: (i, 0, 0)),
        out_shape=jax.ShapeDtypeStruct(x.shape, F32),
        compiler_params=_params("parallel"),
        name="logf_cumsum",
    )(x)


def _fox_prompt_kernel(q_ref, kv_ref, kv_t_ref, cum_ref, cumq_ref, o_ref,
                       qbd_scr, m_scr, l_scr, acc_scr, out_scr, *, tq):
    i = pl.program_id(1)
    lane = lax.broadcasted_iota(jnp.int32, (tq, FOX_W), 1)
    q = q_ref[...].astype(F32)
    qbd_scr[...] = jnp.concatenate(
        [jnp.where(lane // HEAD_DIM == h, q, 0.0) for h in range(FOX_HEADS)], axis=0).astype(BF16)
    cum_q = cum_ref[i]
    cum_q = jnp.concatenate([cum_q[h:h + 1, :] for h in range(FOX_HEADS)], axis=1)
    r_key = lax.broadcasted_iota(jnp.int32, (tq, tq), 0)
    c_qry = lax.broadcasted_iota(jnp.int32, (tq, tq), 1)
    causal_bias = jnp.concatenate([jnp.where(r_key <= c_qry, 0.0, NEG_INF)] * FOX_HEADS, axis=1)
    _flash_init(m_scr, l_scr, acc_scr)

    def chunks(j, n, causal_last):
        k0 = pl.multiple_of(j * tq, tq)
        cum_k = cumq_ref[pl.ds(k0, n * tq), :]
        cum_k = jnp.concatenate(
            [jnp.broadcast_to(cum_k[:, h:h + 1], (n * tq, tq)) for h in range(FOX_HEADS)], axis=1)
        s_t = _dot_nt(kv_ref[pl.ds(k0, n * tq), :FOX_W], qbd_scr[...]) + (cum_q - cum_k)
        if causal_last:
            visible = [jnp.zeros(((n - 1) * tq, FOX_HEADS * tq), F32)] if n > 1 else []
            s_t = s_t + jnp.concatenate(visible + [causal_bias], axis=0)
        p, alpha = _flash_stats(s_t, m_scr, l_scr)
        for h in range(FOX_HEADS):
            cols = slice(h * tq, (h + 1) * tq)
            rows = slice(FOX_W + h * HEAD_DIM, FOX_W + (h + 1) * HEAD_DIM)
            v_t = jnp.concatenate([kv_t_ref[j + d, rows, :] for d in range(n)], axis=1)
            acc_scr[h] = alpha[:, cols] * acc_scr[h] + _dot(v_t, p[:, cols])

    def body(jp, _):
        chunks(2 * jp, 2, False)
        return 0

    lax.fori_loop(0, i // 2, body, 0)

    @pl.when(i % 2 == 1)
    def _():
        chunks(i - 1, 2, True)

    @pl.when(i % 2 == 0)
    def _():
        chunks(i, 1, True)
    inv_l = 1.0 / jnp.maximum(l_scr[0:1, :], TINY)
    for h in range(FOX_HEADS):
        out_scr[h * HEAD_DIM:(h + 1) * HEAD_DIM, :] = acc_scr[h] * inv_l[:, h * tq:(h + 1) * tq]
    o_ref[...] = out_scr[...].T


def fox_attend_prompt(qf, kvf, kvf_t, cum, cum_q, batch, seq, tq=128):
    nq = seq // tq
    kern = functools.partial(_fox_prompt_kernel, tq=tq)
    return pl.pallas_call(
        kern,
        grid=(batch, nq),
        in_specs=[pl.BlockSpec((tq, FOX_W), lambda b, i: (b * nq + i, 0)),
                  pl.BlockSpec((seq, FOX_ROW), lambda b, i: (b, 0)),
                  pl.BlockSpec((None, nq, FOX_ROW, tq), lambda b, i: (b, 0, 0, 0)),
                  pl.BlockSpec((None, nq, 8, tq), lambda b, i: (b, 0, 0, 0)),
                  pl.BlockSpec((seq, 8), lambda b, i: (b, 0))],
        out_specs=pl.BlockSpec((tq, FOX_W), lambda b, i: (b * nq + i, 0)),
        out_shape=jax.ShapeDtypeStruct((batch * seq, FOX_W), F32),
        scratch_shapes=[pltpu.VMEM((FOX_HEADS * tq, FOX_W), BF16),
                        pltpu.VMEM((STAT_ROWS, FOX_HEADS * tq), F32),
                        pltpu.VMEM((STAT_ROWS, FOX_HEADS * tq), F32),
                        pltpu.VMEM((FOX_HEADS, HEAD_DIM, tq), F32),
                        pltpu.VMEM((FOX_W, tq), F32)],
        compiler_params=_params("parallel", "arbitrary"),
        name="fox_prompt",
    )(qf, kvf, kvf_t, jnp.swapaxes(cum.reshape(batch, 8, nq, tq), 1, 2), cum_q)


def _fox_decode_kernel(pt_ref, q_ref, kv_new_ref, lf_new_ref, *rest, n_pages, page, past):
    kv_refs = rest[:n_pages]
    lf_refs = rest[n_pages:2 * n_pages]
    o_ref = rest[2 * n_pages]
    del pt_ref
    tq = TQ_PAD
    rows = FOX_HEADS * tq
    row = lax.broadcasted_iota(jnp.int32, (rows, 1), 0)
    lane = lax.broadcasted_iota(jnp.int32, (rows, FOX_W), 1)
    own_head = (lane // HEAD_DIM) == (row // tq)
    t_rows = past + row % tq

    q = q_ref[...] * SCALE
    qbd = jnp.where(own_head, jnp.concatenate([q] * FOX_HEADS, axis=0), 0.0).astype(BF16)

    def head_rows(a):
        return jnp.concatenate(
            [jnp.broadcast_to(a[h:h + 1, :], (tq, a.shape[1])) for h in range(FOX_HEADS)], axis=0)

    lf_past = jnp.concatenate([r[...] for r in lf_refs], axis=1)
    cum_past = _lane_cumsum(lf_past)
    cum_new = cum_past[:, past - 1:past] + _lane_cumsum(lf_new_ref[...])
    ck_past = head_rows(cum_past)
    ck_new = head_rows(cum_new)
    new_lane = lax.broadcasted_iota(jnp.int32, (rows, page), 1)
    cq = jnp.sum(jnp.where(new_lane == row % tq, ck_new, 0.0), axis=-1, keepdims=True)

    kv_new = jnp.concatenate([kv_new_ref[...], jnp.zeros((page - tq, FOX_ROW), F32)], axis=0).astype(BF16)
    key_pos = lax.broadcasted_iota(jnp.int32, (1, past + page), 1)
    s = jnp.concatenate([_dot(qbd, kv_refs[p][:FOX_W, :].astype(BF16)) for p in range(n_pages)]
                        + [_dot_nt(qbd, kv_new[:, :FOX_W])], axis=1)
    s = s + (cq - jnp.concatenate([ck_past, ck_new], axis=1)) + jnp.where(key_pos <= t_rows, 0.0, NEG_INF)
    m = jnp.max(s, axis=-1, keepdims=True)
    pr = jnp.where(s > MASKED_BELOW, jnp.exp(s - m), 0.0)
    den = jnp.maximum(jnp.sum(pr, axis=-1, keepdims=True), TINY)
    pr = pr.astype(BF16)
    acc = _dot(pr[:, past:], kv_new[:, FOX_W:])
    for p in range(n_pages):
        acc = acc + _dot_nt(pr[:, p * page:(p + 1) * page], kv_refs[p][FOX_W:, :].astype(BF16))
    o = jnp.where(own_head, acc / den, 0.0)
    o_ref[...] = o[0:tq] + o[tq:2 * tq] + o[2 * tq:3 * tq] + o[3 * tq:4 * tq]


def fox_attend_decode(page_table, qf, kvf_new, lf_new_t, fox_pool, lf_pool_t, layer, past):
    n_seq, n_pages = page_table.shape
    page = fox_pool.shape[3]

    def seq_spec(r, width):
        return pl.BlockSpec((None, r, width), lambda b, pt: (b, 0, 0))

    def page_spec(shape, p):
        return pl.BlockSpec((None, None) + shape, lambda b, pt: (layer, pt[b, p], 0, 0))

    assert TQ_PAD <= page
    kern = functools.partial(_fox_decode_kernel, n_pages=n_pages, page=page, past=past)
    grid_spec = pltpu.PrefetchScalarGridSpec(
        num_scalar_prefetch=1,
        grid=(n_seq,),
        in_specs=([seq_spec(TQ_PAD, FOX_W), seq_spec(TQ_PAD, FOX_ROW), seq_spec(8, page)]
                  + [page_spec((FOX_ROW, page), p) for p in range(n_pages)]
                  + [page_spec((8, page), p) for p in range(n_pages)]),
        out_specs=seq_spec(TQ_PAD, FOX_W),
    )
    return pl.pallas_call(
        kern,
        grid_spec=grid_spec,
        out_shape=jax.ShapeDtypeStruct((n_seq, TQ_PAD, FOX_W), F32),
        compiler_params=_params("arbitrary"),
        name="fox_decode",
    )(page_table, qf, kvf_new, lf_new_t, *([fox_pool] * n_pages), *([lf_pool_t] * n_pages))


CONV_PAD = 32
CONV_SUB = 64


def _ln_silu(y, g, b):
    yc = y - jnp.mean(y, axis=-1, keepdims=True)
    var = jnp.mean(yc * yc, axis=-1, keepdims=True)
    return _silu(yc * lax.rsqrt(var + LN_EPS) * g + b)


def _conv_prompt_kernel(prev_ref, cur_ref, w_ref, b_ref, g_ref, bb_ref, o_ref, ctx_scr, *, tc):
    k = pl.program_id(1)
    ctx_scr[0:CONV_PAD, :] = jnp.where(k > 0, prev_ref[...], 0.0)
    ctx_scr[CONV_PAD:, :] = cur_ref[...]
    lead = CONV_PAD - (CONV_WIDTH - 1)
    for sub in range(tc // CONV_SUB):
        acc = jnp.zeros((CONV_SUB, CONV_CH), F32) + b_ref[...]
        for w in range(CONV_WIDTH):
            acc = acc + ctx_scr[pl.ds(sub * CONV_SUB + lead + w, CONV_SUB), :] * w_ref[w:w + 1, :]
        o_ref[sub * CONV_SUB:(sub + 1) * CONV_SUB, :] = _ln_silu(acc, g_ref[...], bb_ref[...])


def conv_module_prompt(glu, dw_w, dw_b, ln_g, ln_b, layer, batch, seq, tc=256):
    tc = min(tc, seq)
    nt = seq // tc
    per = tc // CONV_PAD

    def vec_spec():
        return pl.BlockSpec((None, 1, CONV_CH), lambda b, k: (layer, 0, 0))

    return pl.pallas_call(
        functools.partial(_conv_prompt_kernel, tc=tc),
        grid=(batch, nt),
        in_specs=[pl.BlockSpec((CONV_PAD, CONV_CH), lambda b, k: (jnp.maximum((b * nt + k) * per - 1, 0), 0)),
                  pl.BlockSpec((tc, CONV_CH), lambda b, k: (b * nt + k, 0)),
                  pl.BlockSpec((None, CONV_WIDTH, CONV_CH), lambda b, k: (layer, 0, 0)),
                  vec_spec(), vec_spec(), vec_spec()],
        out_specs=pl.BlockSpec((tc, CONV_CH), lambda b, k: (b * nt + k, 0)),
        out_shape=jax.ShapeDtypeStruct((batch * seq, CONV_CH), F32),
        scratch_shapes=[pltpu.VMEM((CONV_PAD + tc, CONV_CH), F32)],
        compiler_params=_params("parallel", "arbitrary"),
        name="conv_prompt",
    )(glu, glu, dw_w, dw_b, ln_g, ln_b)


def _conv_decode_kernel(state_ref, glu_ref, w_ref, b_ref, g_ref, bb_ref, o_ref):
    n_state = state_ref.shape[0]
    t_new = glu_ref.shape[0]
    ctx = [state_ref[i] for i in range(n_state)] + [glu_ref[i] for i in range(t_new)]
    for i in range(t_new):
        acc = jnp.zeros(ctx[0].shape, F32) + b_ref[...]
        for w in range(CONV_WIDTH):
            acc = acc + ctx[i + w] * w_ref[w:w + 1, :]
        o_ref[i] = _ln_silu(acc, g_ref[...], bb_ref[...])


def conv_module_decode(state_t, glu_t, dw_w, dw_b, ln_g, ln_b, layer):
    t_new, n_seq, _ = glu_t.shape

    def vec_spec():
        return pl.BlockSpec((None, 1, CONV_CH), lambda i: (layer, 0, 0))

    return pl.pallas_call(
        _conv_decode_kernel,
        grid=(1,),
        in_specs=[pl.BlockSpec(state_t.shape, lambda i: (0, 0, 0)),
                  pl.BlockSpec(glu_t.shape, lambda i: (0, 0, 0)),
                  pl.BlockSpec((None, CONV_WIDTH, CONV_CH), lambda i: (layer, 0, 0)),
                  vec_spec(), vec_spec(), vec_spec()],
        out_specs=pl.BlockSpec((t_new, n_seq, CONV_CH), lambda i: (0, 0, 0)),
        out_shape=jax.ShapeDtypeStruct((t_new, n_seq, CONV_CH), F32),
        compiler_params=_params("arbitrary"),
        name="conv_decode",
    )(state_t, glu_t, dw_w, dw_b, ln_g, ln_b)


def _merge_kernel(x_ref, ga_ref, on_ref, yc_ref, of_ref, mg_ref, wn_ref, wc_ref, wf_ref, wo_ref, o_ref):
    merged = (mg_ref[:, 0:D_MODEL] * _dot(on_ref[...].astype(BF16), wn_ref[...])
              + mg_ref[:, D_MODEL:2 * D_MODEL] * _dot(yc_ref[...].astype(BF16), wc_ref[...])
              + mg_ref[:, 2 * D_MODEL:] * _dot(of_ref[...].astype(BF16), wf_ref[...]))
    o_ref[...] = x_ref[...] + ga_ref[...] * _dot(merged.astype(BF16), wo_ref[...])


def merge_and_project(x, mod, o_nsa, y_conv, o_fox, mg, wb_nsa, wb_conv, wb_fox, w_out, layer, tm):
    m = x.shape[0]

    def row_spec(width):
        return pl.BlockSpec((tm, width), lambda i: (i, 0))

    def w_spec(rows):
        return pl.BlockSpec((None, rows, D_MODEL), lambda i: (layer, 0, 0))

    return pl.pallas_call(
        _merge_kernel,
        grid=(m // tm,),
        in_specs=[row_spec(D_MODEL), mod.spec(5, tm), row_spec(NSA_Q), row_spec(CONV_CH), row_spec(FOX_W),
                  row_spec(3 * D_MODEL), w_spec(NSA_Q), w_spec(CONV_CH), w_spec(FOX_W), w_spec(D_MODEL)],
        out_specs=row_spec(D_MODEL),
        out_shape=jax.ShapeDtypeStruct((m, D_MODEL), F32),
        compiler_params=_params("parallel"),
        name="merge",
    )(x, mod.arr, o_nsa, y_conv, o_fox, mg, wb_nsa, wb_conv, wb_fox, w_out)


def _final_norm_kernel(x_ref, g_ref, o_ref):
    x = x_ref[...]
    o_ref[...] = x * lax.rsqrt(jnp.mean(x * x, axis=-1, keepdims=True) + RMS_EPS) * g_ref[...]


def final_norm(x, g, tm):
    m = x.shape[0]
    return pl.pallas_call(
        _final_norm_kernel,
        grid=(m // tm,),
        in_specs=[pl.BlockSpec((tm, D_MODEL), lambda i: (i, 0)), pl.BlockSpec((1, D_MODEL), lambda i: (0, 0))],
        out_specs=pl.BlockSpec((tm, D_MODEL), lambda i: (i, 0)),
        out_shape=jax.ShapeDtypeStruct((m, D_MODEL), F32),
        compiler_params=_params("parallel"),
        name="final_norm",
    )(x, g.reshape(1, D_MODEL))


def _rope_tables(pos):
    half = HEAD_DIM // 2
    inv_freq = ROPE_THETA ** (-jnp.arange(half, dtype=F32) / half)
    ang = pos.astype(F32)[:, None] * inv_freq[None, :]
    cos, sin = jnp.cos(ang), jnp.sin(ang)
    cos = jnp.concatenate([cos, cos] * (LANE // HEAD_DIM), axis=1)
    sin = jnp.concatenate([-sin, sin] * (LANE // HEAD_DIM), axis=1)
    return cos, sin


def _aligned_in_proj(w_in, b_in):
    o_gn = NSA_Q + 3 * KV_ROW
    o_glu = o_gn + 3 * NSA_HEADS
    o_qkvf = o_glu + 2 * CONV_CH
    o_lf = o_qkvf + 3 * FOX_W
    o_mg = o_lf + FOX_HEADS

    def cols(a):
        pad_gn = jnp.zeros(a.shape[:-1] + (LANE - 3 * NSA_HEADS,), a.dtype)
        pad_lf = jnp.zeros(a.shape[:-1] + (LANE - FOX_HEADS,), a.dtype)
        return jnp.concatenate([a[..., :o_gn], a[..., o_glu:o_lf], a[..., o_mg:],
                                a[..., o_gn:o_glu], pad_gn, a[..., o_lf:o_mg], pad_lf], axis=-1)

    return cols(w_in).astype(BF16), cols(b_in)[:, None, :]


def _compress_weights(pe, wk, wv):
    pe2 = jnp.concatenate([pe, pe], axis=-1)

    def block_diag(w):
        z = jnp.zeros_like(w)
        return jnp.concatenate([jnp.concatenate([w, z], axis=-1), jnp.concatenate([z, w], axis=-1)],
                               axis=-2).astype(BF16)

    return pe2, block_diag(wk), block_diag(wv)


def _pad_tokens(a, n_seq, t_new):
    a = a.reshape(n_seq, t_new, a.shape[-1])
    return jnp.pad(a, ((0, 0), (0, TQ_PAD - t_new), (0, 0)))


def kernel(x_prompt, x_sample, cache_nsa_cmp_kv, cache_nsa_slc_kv, cache_fox_kv, cache_fox_logf,
           state_nsa_win_kv, state_conv, page_table, c_prompt, c_sample, w_ada, b_ada, norm_g,
           w_ffn_gu, w_ffn_down, w_in, b_in, nsa_cmp_pe, nsa_cmp_wk, nsa_cmp_wv, conv_dw_w, conv_dw_b,
           conv_ln_g, conv_ln_b, w_branch_nsa, w_branch_conv, w_branch_fox, w_out, final_norm_g):
    batch, seq, _ = x_prompt.shape
    n_seq, t_new, _ = x_sample.shape
    depth = w_ada.shape[0]
    n_pool, page = cache_nsa_cmp_kv.shape[1], cache_nsa_cmp_kv.shape[2]
    n_pages = page_table.shape[1]
    past = n_pages * page
    w_buf = state_nsa_win_kv.shape[2]
    mp, ms = batch * seq, n_seq * t_new
    tm_p = min(512, seq)
    tm_s = min(256, ms)

    w_ada_b = w_ada.astype(BF16)
    w_gu_b = w_ffn_gu.astype(BF16)
    w_down_b = w_ffn_down.astype(BF16)
    w_in_b, b_in_p = _aligned_in_proj(w_in, b_in)
    pe2, wk_bd, wv_bd = _compress_weights(nsa_cmp_pe, nsa_cmp_wk, nsa_cmp_wv)
    pe_t, wk_t, wv_t = _compress_weights(jnp.swapaxes(nsa_cmp_pe, 1, 2), jnp.swapaxes(nsa_cmp_wk, 1, 2),
                                         jnp.swapaxes(nsa_cmp_wv, 1, 2))
    wb_nsa_b, wb_conv_b = w_branch_nsa.astype(BF16), w_branch_conv.astype(BF16)
    wb_fox_b, w_out_b = w_branch_fox.astype(BF16), w_out.astype(BF16)
    norm_g4 = norm_g[:, :, None, :]
    dw_b3, ln_g3, ln_b3 = conv_dw_b[:, None, :], conv_ln_g[:, None, :], conv_ln_b[:, None, :]

    to_feature_major = (0, 1, 3, 4, 5, 2)
    cmp_pool = jnp.transpose(cache_nsa_cmp_kv, to_feature_major).reshape(depth, n_pool * KV_ROW, page)
    slc_pool = jnp.transpose(cache_nsa_slc_kv, to_feature_major).reshape(depth, n_pool, KV_ROW, page)
    fox_pool = jnp.transpose(cache_fox_kv, to_feature_major).reshape(depth, n_pool, FOX_ROW, page)
    lf_pool_t = jnp.pad(jnp.swapaxes(cache_fox_logf, 2, 3).astype(F32),
                        ((0, 0), (0, 0), (0, 8 - FOX_HEADS), (0, 0)))
    win_state = jnp.transpose(state_nsa_win_kv, to_feature_major).reshape(depth, n_seq, KV_ROW, w_buf)

    cos_p, sin_p = _rope_tables(jnp.arange(seq, dtype=jnp.int32))
    cos_s, sin_s = _rope_tables(past + jnp.arange(t_new, dtype=jnp.int32))
    cos_s, sin_s = jnp.tile(cos_s, (n_seq, 1)), jnp.tile(sin_s, (n_seq, 1))

    mod_all = ada_all_layers(jnp.concatenate([c_prompt, c_sample], axis=0), w_ada_b, b_ada)

    xp = x_prompt.reshape(mp, D_MODEL)
    xs = x_sample.reshape(ms, D_MODEL)
    out_p = [[] for _ in range(6)]
    out_s = [[] for _ in range(6)]

    for l in range(depth):
        mod_p = Mod(mod_all[l, :batch, None, :], seq)
        mod_s = Mod(jnp.repeat(mod_all[l, batch:], t_new, axis=0), 1)

        xp = ffn_half_step(xp, mod_p, 0, norm_g4, w_gu_b, w_down_b, l, 0, tm_p)
        xs = ffn_half_step(xs, mod_s, 0, norm_g4, w_gu_b, w_down_b, l, 0, tm_s)

        tm_in = min(256, seq)
        (q_p, kvc_p, kvs_p, kvw_p, glu_p, qf_p, kvf_p, mg_p, gn_p, lf_p,
         kvc_pt, kvs_pt, kvw_pt, kvf_pt, lf_pt, kvs_pc, kvw_pc, kvf_pc) = input_projection(
            xp, mod_p, norm_g4, w_in_b, b_in_p, cos_p, sin_p, l, tm_in, seq // tm_in, feature_major_batch=batch)
        (q_s, kvc_s, kvs_s, kvw_s, glu_s, qf_s, kvf_s, mg_s, gn_s, lf_s) = input_projection(
            xs, mod_s, norm_g4, w_in_b, b_in_p, cos_s, sin_s, l, tm_s, ms // tm_s)

        kcmp_p = nsa_compress_rows(kvc_p.reshape(mp * 2, NSA_KV), pe2, wk_bd, wv_bd, l, 128)
        o_nsa_p = nsa_attend_prompt(q_p, gn_p, kcmp_p, kvs_p, kvs_pc, kvw_p, kvw_pc, batch, seq)
        cum_p = cumsum_lanes(lf_pt)
        cum_q = jnp.swapaxes(cum_p, 1, 2).reshape(mp, 8)
        o_fox_p = fox_attend_prompt(qf_p, kvf_p, kvf_pc, cum_p, cum_q, batch, seq)
        y_conv_p = conv_module_prompt(glu_p, conv_dw_w, dw_b3, ln_g3, ln_b3, l, batch, seq)

        kcmp_pool = nsa_compress_pool(cmp_pool, pe_t, wk_t, wv_t, l, 128)
        o_nsa_s = nsa_attend_decode(page_table, _pad_tokens(q_s, n_seq, t_new), _pad_tokens(gn_s, n_seq, t_new),
                                    _pad_tokens(kvs_s, n_seq, t_new), _pad_tokens(kvw_s, n_seq, t_new),
                                    win_state, kcmp_pool, slc_pool, l, past)
        o_nsa_s = o_nsa_s[:, :t_new].reshape(ms, NSA_Q)
        lf_new_t = jnp.swapaxes(lf_s[:, :FOX_HEADS].reshape(n_seq, t_new, FOX_HEADS), 1, 2)
        lf_new_t = jnp.pad(lf_new_t, ((0, 0), (0, 8 - FOX_HEADS), (0, page - t_new)))
        o_fox_s = fox_attend_decode(page_table, _pad_tokens(qf_s, n_seq, t_new), _pad_tokens(kvf_s, n_seq, t_new),
                                    lf_new_t, fox_pool, lf_pool_t, l, past)
        o_fox_s = o_fox_s[:, :t_new].reshape(ms, FOX_W)
        state_t = jnp.swapaxes(state_conv[l], 0, 1)
        glu_t = jnp.swapaxes(glu_s.reshape(n_seq, t_new, CONV_CH), 0, 1)
        y_conv_s = conv_module_decode(state_t, glu_t, conv_dw_w, dw_b3, ln_g3, ln_b3, l)
        y_conv_s = jnp.swapaxes(y_conv_s, 0, 1).reshape(ms, CONV_CH)

        xp = merge_and_project(xp, mod_p, o_nsa_p, y_conv_p, o_fox_p, mg_p,
                               wb_nsa_b, wb_conv_b, wb_fox_b, w_out_b, l, tm_p)
        xs = merge_and_project(xs, mod_s, o_nsa_s, y_conv_s, o_fox_s, mg_s,
                               wb_nsa_b, wb_conv_b, wb_fox_b, w_out_b, l, tm_s)
        xp = ffn_half_step(xp, mod_p, 6, norm_g4, w_gu_b, w_down_b, l, 1, tm_p)
        xs = ffn_half_step(xs, mod_s, 6, norm_g4, w_gu_b, w_down_b, l, 1, tm_s)

        kv5 = (2, NSA_KV_HEADS, HEAD_DIM)
        glu_p3 = glu_p.reshape(batch, seq, CONV_CH)
        w_keep = min(NSA_WINDOW, seq)

        def token_major(a, heads):
            a = a.reshape(batch, 2, heads, HEAD_DIM, a.shape[-1])
            return jnp.transpose(a, (0, 4, 1, 2, 3))

        out_p[0].append(token_major(kvc_pt, NSA_KV_HEADS))
        out_p[1].append(token_major(kvs_pt, NSA_KV_HEADS))
        out_p[2].append(token_major(kvf_pt, FOX_HEADS))
        out_p[3].append(jnp.swapaxes(lf_pt[:, :FOX_HEADS], 1, 2))
        out_p[4].append(token_major(kvw_pt[:, :, seq - w_keep:], NSA_KV_HEADS))
        out_p[5].append(glu_p3[:, seq - (CONV_WIDTH - 1):])
        kvw_s5 = kvw_s.reshape((n_seq, t_new) + kv5)
        out_s[0].append(kvc_s.reshape((n_seq, t_new) + kv5))
        out_s[1].append(kvs_s.reshape((n_seq, t_new) + kv5))
        out_s[2].append(kvf_s.reshape(n_seq, t_new, 2, FOX_HEADS, HEAD_DIM))
        out_s[3].append(lf_s[:, :FOX_HEADS].reshape(n_seq, t_new, FOX_HEADS))
        out_s[4].append(jnp.concatenate([state_nsa_win_kv[l], kvw_s5], axis=1)[:, t_new:])
        conv_ctx = jnp.concatenate([state_conv[l], glu_s.reshape(n_seq, t_new, CONV_CH)], axis=1)
        out_s[5].append(conv_ctx[:, conv_ctx.shape[1] - (CONV_WIDTH - 1):])

    y_p = final_norm(xp, final_norm_g, tm_p).reshape(batch, seq, D_MODEL)
    y_s = final_norm(xs, final_norm_g, tm_s).reshape(n_seq, t_new, D_MODEL)
    outs = [y_p, y_s]
    for f in range(6):
        outs.append(jnp.stack(out_p[f]))
        outs.append(jnp.stack(out_s[f]))
    return tuple(outs)
```

```python
import functools

import jax
import jax.numpy as jnp
from jax import lax
from jax.experimental import pallas as pl
from jax.experimental.pallas import tpu as pltpu

D_MODEL = 1024
HEAD_DIM = 64
NSA_HEADS = 8
NSA_KV_HEADS = 2
NSA_GROUP = NSA_HEADS // NSA_KV_HEADS
NSA_BLOCK = 64
NSA_TOP_K = 16
NSA_WINDOW = 512
FOX_HEADS = 4
CONV_CH = 256
CONV_WIDTH = 31
D_FF = 2816
ROPE_THETA = 10000.0
RMS_EPS = 1e-6
LN_EPS = 1e-5
NEG_INF = -1e30
POS_INF = 1e30
TINY = 1e-30
MASKED_BELOW = -5e29
SCALE = HEAD_DIM ** -0.5

NSA_Q = NSA_HEADS * HEAD_DIM
NSA_KV = NSA_KV_HEADS * HEAD_DIM
FOX_W = FOX_HEADS * HEAD_DIM
KV_ROW = 2 * NSA_KV
FOX_ROW = 2 * FOX_W

LANE = 128
C_Q = 0
C_KVC = C_Q + NSA_Q
C_KVS = C_KVC + KV_ROW
C_KVW = C_KVS + KV_ROW
C_GLU = C_KVW + KV_ROW
C_QF = C_GLU + 2 * CONV_CH
C_KVF = C_QF + FOX_W
C_MG = C_KVF + FOX_ROW
C_GN = C_MG + 3 * D_MODEL
C_LF = C_GN + LANE
D_IN_PAD = C_LF + LANE

TQ_PAD = 8
DECODE_SEQS = 2
VMEM_LIMIT = 56 * 1024 * 1024

F32 = jnp.float32
BF16 = jnp.bfloat16
NT = (((1,), (1,)), ((), ()))


def _params(*sem):
    return pltpu.CompilerParams(dimension_semantics=sem, vmem_limit_bytes=VMEM_LIMIT)


def _dot(a, b):
    return jnp.dot(a, b, preferred_element_type=F32)


def _dot_nt(a, b):
    return lax.dot_general(a, b, NT, preferred_element_type=F32)


def _rms_mod(x, g, shift, scale):
    y = x * lax.rsqrt(jnp.mean(x * x, axis=-1, keepdims=True) + RMS_EPS) * g
    return y * (1.0 + scale) + shift


def _sigmoid(x):
    return 1.0 / (1.0 + jnp.exp(-x))


def _silu(x):
    return x * _sigmoid(x)


def _ada_kernel(c_ref, w_ref, b_ref, o_ref):
    c = _silu(c_ref[...]).astype(BF16)
    o_ref[...] = _dot(c, w_ref[...]) + b_ref[...]


def ada_all_layers(cond, w_ada, b_ada):
    depth, _, n = w_ada.shape
    rows = cond.shape[0]
    tn = n // 8
    return pl.pallas_call(
        _ada_kernel,
        grid=(depth, n // tn),
        in_specs=[pl.BlockSpec((rows, D_MODEL), lambda l, j: (0, 0)),
                  pl.BlockSpec((None, D_MODEL, tn), lambda l, j: (l, 0, j)),
                  pl.BlockSpec((None, 1, tn), lambda l, j: (l, 0, j))],
        out_specs=pl.BlockSpec((None, rows, tn), lambda l, j: (l, 0, j)),
        out_shape=jax.ShapeDtypeStruct((depth, rows, n), F32),
        compiler_params=_params("parallel", "parallel"),
        name="ada",
    )(cond, w_ada, b_ada.reshape(depth, 1, n))


class Mod:
    def __init__(self, arr, tokens_per_cond):
        self.arr = arr
        self.tokens_per_cond = tokens_per_cond

    def spec(self, k, tm):
        if self.arr.ndim == 3:
            per = self.tokens_per_cond // tm
            return pl.BlockSpec((None, 1, D_MODEL), lambda m, *_: (m // per, 0, k))
        return pl.BlockSpec((tm, D_MODEL), lambda m, *_: (m, k))


def _ffn_kernel(x_ref, g_ref, sh_ref, sc_ref, ga_ref, wg_ref, wu_ref, wd_ref, o_ref, h_scr, acc_scr):
    j = pl.program_id(1)

    @pl.when(j == 0)
    def _():
        h_scr[...] = _rms_mod(x_ref[...], g_ref[...], sh_ref[...], sc_ref[...]).astype(BF16)
        acc_scr[...] = jnp.zeros_like(acc_scr)

    h = h_scr[...]
    gate = _dot(h, wg_ref[...])
    up = _dot(h, wu_ref[...])
    a = (_silu(gate) * up).astype(BF16)
    acc_scr[...] += _dot(a, wd_ref[...])

    @pl.when(j == pl.num_programs(1) - 1)
    def _():
        o_ref[...] = x_ref[...] + 0.5 * ga_ref[...] * acc_scr[...]


def ffn_half_step(x, mod, k0, norm_g, w_gu, w_down, layer, idx, tm):
    m = x.shape[0]
    nf = 2
    tf = D_FF // nf
    return pl.pallas_call(
        _ffn_kernel,
        grid=(m // tm, nf),
        in_specs=[pl.BlockSpec((tm, D_MODEL), lambda i, j: (i, 0)),
                  pl.BlockSpec((None, None, 1, D_MODEL), lambda i, j: (layer, 2 * idx, 0, 0)),
                  mod.spec(k0, tm), mod.spec(k0 + 1, tm), mod.spec(k0 + 2, tm),
                  pl.BlockSpec((None, None, D_MODEL, tf), lambda i, j: (layer, idx, 0, j)),
                  pl.BlockSpec((None, None, D_MODEL, tf), lambda i, j: (layer, idx, 0, nf + j)),
                  pl.BlockSpec((None, None, tf, D_MODEL), lambda i, j: (layer, idx, j, 0))],
        out_specs=pl.BlockSpec((tm, D_MODEL), lambda i, j: (i, 0)),
        out_shape=jax.ShapeDtypeStruct((m, D_MODEL), F32),
        scratch_shapes=[pltpu.VMEM((tm, D_MODEL), BF16), pltpu.VMEM((tm, D_MODEL), F32)],
        compiler_params=_params("parallel", "arbitrary"),
        name="ffn",
    )(x, norm_g, mod.arr, mod.arr, mod.arr, w_gu, w_gu, w_down)


def _rope(z, cos, sin):
    n = z.shape[1]
    lane = lax.broadcasted_iota(jnp.int32, z.shape, 1)
    first_half = (lane % HEAD_DIM) < (HEAD_DIM // 2)
    partner = jnp.where(first_half, pltpu.roll(z, n - HEAD_DIM // 2, 1), pltpu.roll(z, HEAD_DIM // 2, 1))
    reps = n // LANE
    if reps > 1:
        cos = jnp.concatenate([cos] * reps, axis=1)
        sin = jnp.concatenate([sin] * reps, axis=1)
    return z * cos + partner * sin


def _inproj_kernel(x_ref, g_ref, sh_ref, sc_ref, cos_ref, sin_ref, w_ref, b_ref,
                   q_ref, kvc_ref, kvs_ref, kvw_ref, glu_ref, qf_ref, kvf_ref, mg_ref, gn_ref, lf_ref,
                   *feature_major_refs):
    u = _rms_mod(x_ref[...], g_ref[...], sh_ref[...], sc_ref[...]).astype(BF16)
    cos = cos_ref[...]
    sin = sin_ref[...]

    def proj(lo, hi):
        return _dot(u, w_ref[:, lo:hi]) + b_ref[:, lo:hi]

    def as_operand(ref, val, scale=None):
        if ref.dtype == BF16 and scale is not None:
            val = val * scale
        ref[...] = val.astype(ref.dtype)

    as_operand(q_ref, _rope(proj(C_Q, C_KVC), cos, sin), SCALE)
    kv_rows = []
    for ref, lo in ((kvc_ref, C_KVC), (kvs_ref, C_KVS), (kvw_ref, C_KVW)):
        z = proj(lo, lo + KV_ROW)
        z = jnp.concatenate([_rope(z[:, :NSA_KV], cos, sin), z[:, NSA_KV:]], axis=1)
        as_operand(ref, z)
        kv_rows.append(z)
    z = proj(C_GLU, C_QF)
    glu_ref[...] = z[:, :CONV_CH] * _sigmoid(z[:, CONV_CH:])
    as_operand(qf_ref, proj(C_QF, C_KVF), SCALE)
    kvf = proj(C_KVF, C_MG)
    as_operand(kvf_ref, kvf)
    mg_ref[...] = _sigmoid(proj(C_MG, C_GN))
    gn_ref[...] = _sigmoid(proj(C_GN, C_LF))
    z = proj(C_LF, D_IN_PAD)
    lf = jnp.minimum(z, 0.0) - jnp.log(1.0 + jnp.exp(-jnp.abs(z)))
    lf_ref[...] = lf
    if feature_major_refs:
        kvc_t, kvs_t, kvw_t, kvf_t, lf_t, kvs_c, kvw_c, kvf_c = feature_major_refs
        for ref, rows, chunked in ((kvc_t, kv_rows[0], None), (kvs_t, kv_rows[1], kvs_c),
                                   (kvw_t, kv_rows[2], kvw_c), (kvf_t, kvf, kvf_c)):
            rows_t = rows.T
            ref[...] = rows_t
            if chunked is not None:
                for c in range(chunked.shape[0]):
                    chunked[c] = rows_t[:, c * LANE:(c + 1) * LANE].astype(BF16)
        lf_t[...] = lf.T[:lf_t.shape[0], :]


def input_projection(x, mod, norm_g, w_in, b_in, cos, sin, layer, tm, pos_tiles, feature_major_batch=None):
    m = x.shape[0]
    widths = (NSA_Q, KV_ROW, KV_ROW, KV_ROW, CONV_CH, FOX_W, FOX_ROW, 3 * D_MODEL, LANE, LANE)
    out_specs = [pl.BlockSpec((tm, w), lambda i: (i, 0)) for w in widths]
    operand = (0, 2, 3, 5, 6) if feature_major_batch else ()
    out_shape = [jax.ShapeDtypeStruct((m, w), BF16 if k in operand else F32) for k, w in enumerate(widths)]
    if feature_major_batch:
        t = m // feature_major_batch
        for rows in (KV_ROW, KV_ROW, KV_ROW, FOX_ROW, 8):
            out_specs.append(pl.BlockSpec((None, rows, tm), lambda i: (i // pos_tiles, 0, i % pos_tiles)))
            out_shape.append(jax.ShapeDtypeStruct((feature_major_batch, rows, t), F32))
        for rows in (KV_ROW, KV_ROW, FOX_ROW):
            out_specs.append(pl.BlockSpec((None, tm // LANE, rows, LANE),
                                          lambda i: (i // pos_tiles, i % pos_tiles, 0, 0)))
            out_shape.append(jax.ShapeDtypeStruct((feature_major_batch, t // LANE, rows, LANE), BF16))
    return pl.pallas_call(
        _inproj_kernel,
        grid=(m // tm,),
        in_specs=[pl.BlockSpec((tm, D_MODEL), lambda i: (i, 0)),
                  pl.BlockSpec((None, None, 1, D_MODEL), lambda i: (layer, 1, 0, 0)),
                  mod.spec(3, tm), mod.spec(4, tm),
                  pl.BlockSpec((tm, LANE), lambda i: (i % pos_tiles, 0)),
                  pl.BlockSpec((tm, LANE), lambda i: (i % pos_tiles, 0)),
                  pl.BlockSpec((None, D_MODEL, D_IN_PAD), lambda i: (layer, 0, 0)),
                  pl.BlockSpec((None, 1, D_IN_PAD), lambda i: (layer, 0, 0))],
        out_specs=out_specs,
        out_shape=out_shape,
        compiler_params=_params("parallel"),
        name="inproj",
    )(x, norm_g, mod.arr, mod.arr, cos, sin, w_in, b_in)


def _compress_kernel(x_ref, pe_ref, wk_ref, wv_ref, o_ref):
    nblk = o_ref.shape[0]
    acc_k = jnp.zeros((nblk, NSA_KV), F32)
    acc_v = jnp.zeros((nblk, NSA_KV), F32)
    for l in range(NSA_BLOCK):
        pe = pe_ref[l:l + 1, :]
        k_rows = x_ref[pl.ds(2 * l, nblk, stride=2 * NSA_BLOCK), :] + pe
        v_rows = x_ref[pl.ds(2 * l + 1, nblk, stride=2 * NSA_BLOCK), :] + pe
        acc_k = acc_k + _dot(k_rows.astype(BF16), wk_ref[l])
        acc_v = acc_v + _dot(v_rows.astype(BF16), wv_ref[l])
    o_ref[:, :NSA_KV] = acc_k
    o_ref[:, NSA_KV:] = acc_v


def nsa_compress_rows(rows, pe2, wk_bd, wv_bd, layer, blocks_per_step):
    lead = rows.shape[:-2]
    nblk = rows.shape[-2] // (2 * NSA_BLOCK)
    nb = max(d for d in range(8, blocks_per_step + 1, 8) if nblk % d == 0)
    if lead:
        x_spec = pl.BlockSpec((None, nb * 2 * NSA_BLOCK, NSA_KV), lambda i: (layer, i, 0))
    else:
        x_spec = pl.BlockSpec((nb * 2 * NSA_BLOCK, NSA_KV), lambda i: (i, 0))
    w_spec = pl.BlockSpec((None, NSA_BLOCK, NSA_KV, NSA_KV), lambda i: (layer, 0, 0, 0))
    return pl.pallas_call(
        _compress_kernel,
        grid=(nblk // nb,),
        in_specs=[x_spec, pl.BlockSpec((None, NSA_BLOCK, NSA_KV), lambda i: (layer, 0, 0)), w_spec, w_spec],
        out_specs=pl.BlockSpec((nb, KV_ROW), lambda i: (i, 0)),
        out_shape=jax.ShapeDtypeStruct((nblk, KV_ROW), F32),
        compiler_params=_params("parallel"),
        name="nsa_compress",
    )(rows, pe2, wk_bd, wv_bd)


def _compress_pool_kernel(x_ref, pe_ref, wk_ref, wv_ref, o_ref, *, blocks_per_page):
    pages = o_ref.shape[0]
    page = x_ref.shape[1]
    for c in range(2 * NSA_KV_HEADS):
        w_ref = wk_ref if c < NSA_KV_HEADS else wv_ref
        acc = jnp.zeros((pages, page), F32)
        for d in range(HEAD_DIM):
            rows = x_ref[pl.ds(c * HEAD_DIM + d, pages, stride=KV_ROW), :] + pe_ref[d:d + 1, :]
            acc = acc + _dot(rows.astype(BF16), w_ref[d])
        for j in range(blocks_per_page):
            o_ref[:, j * KV_ROW + c * HEAD_DIM:j * KV_ROW + (c + 1) * HEAD_DIM] = acc[:, j * HEAD_DIM:(j + 1) * HEAD_DIM]


def nsa_compress_pool(pool_t, pe_t, wk_t, wv_t, layer, pages_per_step):
    page = pool_t.shape[-1]
    n_pool = pool_t.shape[1] // KV_ROW
    bpp = page // NSA_BLOCK
    assert page == LANE and bpp * HEAD_DIM == page
    pp = max(d for d in range(8, pages_per_step + 1, 8) if n_pool % d == 0)
    w_spec = pl.BlockSpec((None, HEAD_DIM, page, page), lambda i: (layer, 0, 0, 0))
    out = pl.pallas_call(
        functools.partial(_compress_pool_kernel, blocks_per_page=bpp),
        grid=(n_pool // pp,),
        in_specs=[pl.BlockSpec((None, pp * KV_ROW, page), lambda i: (layer, i, 0)),
                  pl.BlockSpec((None, HEAD_DIM, page), lambda i: (layer, 0, 0)), w_spec, w_spec],
        out_specs=pl.BlockSpec((pp, bpp * KV_ROW), lambda i: (i, 0)),
        out_shape=jax.ShapeDtypeStruct((n_pool, bpp * KV_ROW), F32),
        compiler_params=_params("parallel"),
        name="nsa_compress_pool",
    )(pool_t, pe_t, wk_t, wv_t)
    return out.reshape(n_pool, 1, bpp * KV_ROW)


def _softmax_rows(logits):
    m = jnp.max(logits, axis=-1, keepdims=True)
    p = jnp.where(logits > MASKED_BELOW, jnp.exp(logits - m), 0.0)
    return p / jnp.maximum(jnp.sum(p, axis=-1, keepdims=True), TINY)


STAT_ROWS = 8


def _flash_init(m_scr, l_scr, acc_scr):
    m_scr[...] = jnp.full(m_scr.shape, NEG_INF, F32)
    l_scr[...] = jnp.zeros(l_scr.shape, F32)
    acc_scr[...] = jnp.zeros(acc_scr.shape, F32)


def _flash_stats(s_t, m_scr, l_scr):
    m = m_scr[0:1, :]
    m_new = jnp.maximum(m, jnp.max(s_t, axis=0, keepdims=True))
    alpha = jnp.exp(m - m_new)
    p = jnp.exp(s_t - jnp.where(m_new > MASKED_BELOW, m_new, 0.0))
    l_scr[0:1, :] = alpha * l_scr[0:1, :] + jnp.sum(p, axis=0, keepdims=True)
    m_scr[0:1, :] = m_new
    return p.astype(BF16), alpha


def _select_blocks(score, t, nb):
    blk = lax.broadcasted_iota(jnp.int32, score.shape, 1)
    cur = t // NSA_BLOCK
    forced = jnp.where(blk == 0, 1, jnp.where(blk == cur, 1, jnp.where(blk == cur - 1, 1, 0)))
    s = jnp.where(forced == 1, POS_INF, jnp.where(blk > cur, NEG_INF, score))
    rank = jnp.zeros(score.shape, jnp.int32)
    for i in range(nb):
        col = s[:, i:i + 1]
        ahead = jnp.where(col > s, 1, jnp.where(col == s, jnp.where(blk > i, 1, 0), 0))
        rank = rank + ahead
    k_sel = min(NSA_TOP_K, nb)
    return jnp.where(rank < k_sel, jnp.where(blk < nb, 1.0, 0.0), 0.0)


def _expand_blocks(sel, k0, kc):
    width = sel.shape[1]
    kblk = (k0 + lax.broadcasted_iota(jnp.int32, (width, kc), 1)) // NSA_BLOCK
    e = jnp.where(kblk == lax.broadcasted_iota(jnp.int32, (width, kc), 0), 1.0, 0.0).astype(BF16)
    return _dot(sel.astype(BF16), e)


def _select_blocks_t(score_t, t_row, nb):
    blk = lax.broadcasted_iota(jnp.int32, score_t.shape, 0)
    cur = t_row // NSA_BLOCK
    forced = jnp.where(blk == 0, 1, jnp.where(blk == cur, 1, jnp.where(blk == cur - 1, 1, 0)))
    s = jnp.where(forced == 1, POS_INF, jnp.where(blk > cur, NEG_INF, score_t))
    rank = jnp.zeros(score_t.shape, jnp.int32)
    for i in range(nb):
        other = s[i:i + 1, :]
        rank = rank + jnp.where(other > s, 1, jnp.where(other == s, jnp.where(blk > i, 1, 0), 0))
    return jnp.where(rank < min(NSA_TOP_K, nb), 1.0, 0.0)


def _nsa_prompt_kernel(q_ref, gate_ref, kcmp_ref, kvs_ref, kvs_t_ref, kvw_ref, kvw_t_ref, o_ref,
                       qbd_scr, expand_scr, bias_scr, m_scr, l_scr, acc_scr, out_scr, *, tq):
    i = pl.program_id(1)
    t0 = i * tq
    t_row = t0 + lax.broadcasted_iota(jnp.int32, (1, tq), 1)
    nc = kcmp_ref.shape[0]
    lane = lax.broadcasted_iota(jnp.int32, (tq, LANE), 1)
    low_half = lane < HEAD_DIM

    pieces = []
    for h in range(NSA_HEADS):
        g = h // NSA_GROUP
        pair = q_ref[:, (h // 2) * LANE:(h // 2 + 1) * LANE].astype(F32)
        if h % 2 != g:
            pair = pltpu.roll(pair, HEAD_DIM, 1)
        pieces.append(jnp.where(low_half, pair, 0.0) if g == 0 else jnp.where(low_half, 0.0, pair))
    qbd_scr[...] = jnp.concatenate(pieces, axis=0).astype(BF16)
    qbd = qbd_scr[...]
    gates_t = gate_ref[...].T

    kcm = kcmp_ref[:, :NSA_KV].astype(BF16)
    vcm_t = jnp.concatenate([kcmp_ref[:, NSA_KV:], jnp.zeros((LANE - nc, NSA_KV), F32)], axis=0).T
    blk_t = lax.broadcasted_iota(jnp.int32, (nc, 1), 0)
    cmp_bias_t = jnp.where((blk_t + 1) * NSA_BLOCK - 1 <= t_row, 0.0, NEG_INF)
    lc_t = _dot_nt(kcm, qbd) + jnp.concatenate([cmp_bias_t] * NSA_HEADS, axis=1)
    m_t = jnp.max(lc_t, axis=0, keepdims=True)
    p_t = jnp.where(lc_t > MASKED_BELOW, jnp.exp(lc_t - m_t), 0.0)
    pc_t = p_t / jnp.maximum(jnp.sum(p_t, axis=0, keepdims=True), TINY)
    o_cmp_t = _dot(vcm_t[:, :nc].astype(BF16), pc_t.astype(BF16))
    for h in range(NSA_HEADS):
        g = h // NSA_GROUP
        out_scr[h * HEAD_DIM:(h + 1) * HEAD_DIM, :] = (
            gates_t[3 * h:3 * h + 1, :] * o_cmp_t[g * HEAD_DIM:(g + 1) * HEAD_DIM, h * tq:(h + 1) * tq])

    r_key = lax.broadcasted_iota(jnp.int32, (tq, tq), 0)
    c_qry = lax.broadcasted_iota(jnp.int32, (tq, tq), 1)
    sels = []
    for g in range(NSA_KV_HEADS):
        base = g * NSA_GROUP * tq
        score_t = (pc_t[:, base:base + tq] + pc_t[:, base + tq:base + 2 * tq]
                   + pc_t[:, base + 2 * tq:base + 3 * tq] + pc_t[:, base + 3 * tq:base + 4 * tq])
        sel_t = _select_blocks_t(score_t, t_row, nc)
        sels.append(jnp.concatenate([sel_t, jnp.zeros((LANE - nc, tq), F32)], axis=0))
    sel_t = jnp.concatenate(sels, axis=1).astype(BF16)

    n_chunks = bias_scr.shape[1]
    span = min(4, n_chunks)

    @pl.when(i == 0)
    def _():
        kblk = lax.broadcasted_iota(jnp.int32, expand_scr.shape, 0) // NSA_BLOCK
        expand_scr[...] = jnp.where(kblk == lax.broadcasted_iota(jnp.int32, expand_scr.shape, 1),
                                    1.0, 0.0).astype(BF16)

    t_rows = jnp.concatenate([t_row] * NSA_KV_HEADS, axis=1)
    for k in range(n_chunks // span):
        @pl.when(i + 1 >= k * span)
        def _(k=k):
            first = k * span * tq
            ok = _dot(expand_scr[first:first + span * tq, :], sel_t)
            kpos = first + lax.broadcasted_iota(jnp.int32, (span * tq, 1), 0)
            bias = jnp.where(kpos <= t_rows, jnp.where(ok > 0.5, 0.0, NEG_INF), NEG_INF)
            for jj in range(span):
                for g in range(NSA_KV_HEADS):
                    bias_scr[g, k * span + jj] = bias[jj * tq:(jj + 1) * tq, g * tq:(g + 1) * tq]

    group_w = NSA_GROUP * tq

    def attend_chunk(keys, vals_t, bias_per_group):
        s_t = _dot_nt(keys, qbd_scr[...])
        if bias_per_group is not None:
            s_t = s_t + jnp.concatenate([b for b in bias_per_group for _ in range(NSA_GROUP)], axis=1)
        p, alpha = _flash_stats(s_t, m_scr, l_scr)
        for g in range(NSA_KV_HEADS):
            cols = slice(g * group_w, (g + 1) * group_w)
            rows = slice(NSA_KV + g * HEAD_DIM, NSA_KV + (g + 1) * HEAD_DIM)
            v_t = jnp.concatenate([v[rows, :] for v in vals_t], axis=1)
            acc_scr[g] = alpha[:, cols] * acc_scr[g] + _dot(v_t, p[:, cols])

    def finish_branch(branch):
        inv_l = 1.0 / jnp.maximum(l_scr[0:1, :], TINY)
        for h in range(NSA_HEADS):
            g, r = divmod(h, NSA_GROUP)
            rs = slice(h * HEAD_DIM, (h + 1) * HEAD_DIM)
            o_t = acc_scr[g, :, r * tq:(r + 1) * tq] * inv_l[:, h * tq:(h + 1) * tq]
            out_scr[rs, :] = out_scr[rs, :] + gates_t[3 * h + branch:3 * h + branch + 1, :] * o_t

    _flash_init(m_scr, l_scr, acc_scr)
    pair = min(2, n_chunks)

    def slc_body(jp, _):
        j = jp * pair
        k0 = pl.multiple_of(j * tq, pair * tq)
        attend_chunk(kvs_ref[pl.ds(k0, pair * tq), :NSA_KV], [kvs_t_ref.at[j + d] for d in range(pair)],
                     [jnp.concatenate([bias_scr[g, j + d] for d in range(pair)], axis=0)
                      for g in range(NSA_KV_HEADS)])
        return 0

    lax.fori_loop(0, i // pair + 1, slc_body, 0)
    finish_branch(1)

    _flash_init(m_scr, l_scr, acc_scr)
    n_win = NSA_WINDOW // tq + 1

    def win_chunks(c_first, n, j_first):
        k0 = pl.multiple_of(j_first * tq, tq)
        biases = []
        for c in range(c_first, c_first + n):
            bias = jnp.zeros((tq, tq), F32)
            if c == 0:
                bias = bias + jnp.where(r_key > c_qry, 0.0, NEG_INF)
            if c == n_win - 1:
                bias = bias + jnp.where(r_key <= c_qry, 0.0, NEG_INF)
            biases.append(bias)
        masked = c_first == 0 or c_first + n == n_win
        attend_chunk(kvw_ref[pl.ds(k0, n * tq), :NSA_KV], [kvw_t_ref.at[j_first + d] for d in range(n)],
                     [jnp.concatenate(biases, axis=0)] * NSA_KV_HEADS if masked else None)

    c = 0
    while c < n_win:
        n = 2 if c + 1 < n_win - 1 else 1
        jc = i - (n_win - 1) + c
        pl.when(jc >= 0)(functools.partial(win_chunks, c, n, jc))
        if n == 2:
            pl.when(jc == -1)(functools.partial(win_chunks, c + 1, 1, 0))
        c += n

    finish_branch(2)
    o_ref[...] = out_scr[...].T


def nsa_attend_prompt(q, gates, kcmp, kvs, kvs_t, kvw, kvw_t, batch, seq, tq=128):
    nq = seq // tq
    nc = seq // NSA_BLOCK
    assert nc <= LANE and NSA_WINDOW % tq == 0 and tq == LANE and nq % min(4, nq) == 0
    kern = functools.partial(_nsa_prompt_kernel, tq=tq)
    return pl.pallas_call(
        kern,
        grid=(batch, nq),
        in_specs=[pl.BlockSpec((tq, NSA_Q), lambda b, i: (b * nq + i, 0)),
                  pl.BlockSpec((tq, LANE), lambda b, i: (b * nq + i, 0)),
                  pl.BlockSpec((nc, KV_ROW), lambda b, i: (b, 0)),
                  pl.BlockSpec((seq, KV_ROW), lambda b, i: (b, 0)),
                  pl.BlockSpec((None, nq, KV_ROW, tq), lambda b, i: (b, 0, 0, 0)),
                  pl.BlockSpec((seq, KV_ROW), lambda b, i: (b, 0)),
                  pl.BlockSpec((None, nq, KV_ROW, tq), lambda b, i: (b, 0, 0, 0))],
        out_specs=pl.BlockSpec((tq, NSA_Q), lambda b, i: (b * nq + i, 0)),
        out_shape=jax.ShapeDtypeStruct((batch * seq, NSA_Q), F32),
        scratch_shapes=[pltpu.VMEM((NSA_HEADS * tq, LANE), BF16),
                        pltpu.VMEM((seq, LANE), BF16),
                        pltpu.VMEM((NSA_KV_HEADS, nq, tq, tq), F32),
                        pltpu.VMEM((STAT_ROWS, NSA_HEADS * tq), F32),
                        pltpu.VMEM((STAT_ROWS, NSA_HEADS * tq), F32),
                        pltpu.VMEM((NSA_KV_HEADS, HEAD_DIM, NSA_GROUP * tq), F32),
                        pltpu.VMEM((NSA_Q, tq), F32)],
        compiler_params=_params("parallel", "arbitrary"),
        name="nsa_prompt",
    )(q, gates, kcmp, kvs, kvs_t, kvw, kvw_t)


def _nsa_decode_kernel(pt_ref, q_ref, gate_ref, kvs_new_ref, kvw_new_ref, win_ref, *rest,
                       n_pages, page, past, seqs):
    del pt_ref
    kcmp_refs = rest[:seqs * n_pages]
    kvs_refs = rest[seqs * n_pages:2 * seqs * n_pages]
    o_ref, kcmp_scr = rest[2 * seqs * n_pages:]
    for s in range(seqs):
        _nsa_decode_one(q_ref.at[s], gate_ref.at[s], kvs_new_ref.at[s], kvw_new_ref.at[s], win_ref.at[s],
                        kcmp_refs[s * n_pages:(s + 1) * n_pages], kvs_refs[s * n_pages:(s + 1) * n_pages],
                        o_ref.at[s], kcmp_scr.at[s], n_pages=n_pages, page=page, past=past)


def _nsa_decode_one(q_ref, gate_ref, kvs_new_ref, kvw_new_ref, win_ref, kcmp_refs, kvs_refs, o_ref, kcmp_scr,
                    *, n_pages, page, past):
    tq = TQ_PAD
    rows = NSA_HEADS * tq
    blocks_per_page = page // NSA_BLOCK
    nc = n_pages * blocks_per_page
    nb = nc + 1
    for p in range(n_pages):
        for j in range(blocks_per_page):
            kcmp_scr[p * blocks_per_page + j:p * blocks_per_page + j + 1, :] = (
                kcmp_refs[p][:, j * KV_ROW:(j + 1) * KV_ROW])

    row = lax.broadcasted_iota(jnp.int32, (rows, 1), 0)
    t_rows = past + row % tq
    t1 = past + lax.broadcasted_iota(jnp.int32, (tq, 1), 0)
    lane = lax.broadcasted_iota(jnp.int32, (rows, LANE), 1)
    own_group = (lane // HEAD_DIM) == (row // (NSA_GROUP * tq))

    q = q_ref[...] * SCALE
    pieces = []
    for h in range(NSA_HEADS):
        qh = q[:, h * HEAD_DIM:(h + 1) * HEAD_DIM]
        zero = jnp.zeros_like(qh)
        pieces.append(jnp.concatenate([qh, zero] if h < NSA_GROUP else [zero, qh], axis=1))
    qbd = jnp.concatenate(pieces, axis=0).astype(BF16)

    def own_half(o):
        o = jnp.where(own_group, o, 0.0)
        return o[:, :HEAD_DIM] + o[:, HEAD_DIM:]

    kcm = kcmp_scr[:, :NSA_KV].astype(BF16)
    vcm = kcmp_scr[:, NSA_KV:].astype(BF16)
    blk = lax.broadcasted_iota(jnp.int32, (1, nc), 1)
    lc = _dot_nt(qbd, kcm) + jnp.where((blk + 1) * NSA_BLOCK - 1 <= t_rows, 0.0, NEG_INF)
    pc = _softmax_rows(lc)
    o_cmp = own_half(_dot(pc.astype(BF16), vcm))

    gates = gate_ref[...]
    new_k_pos = past + lax.broadcasted_iota(jnp.int32, (1, page), 1)
    pad = jnp.zeros((page - tq, KV_ROW), F32)
    kvs_new = jnp.concatenate([kvs_new_ref[...], pad], axis=0).astype(BF16)
    kvw_new = jnp.concatenate([kvw_new_ref[...], pad], axis=0).astype(BF16)

    n_keys = (n_pages + 1) * page
    key_pos = lax.broadcasted_iota(jnp.int32, (1, n_keys), 1)
    sels = []
    for g in range(NSA_KV_HEADS):
        pcg = pc[g * NSA_GROUP * tq:(g + 1) * NSA_GROUP * tq]
        score = pcg[0:tq] + pcg[tq:2 * tq] + pcg[2 * tq:3 * tq] + pcg[3 * tq:4 * tq]
        score = jnp.concatenate([score, jnp.zeros((tq, LANE - nc), F32)], axis=1)
        sels.append(_select_blocks(score, t1, nb))
    ok = _expand_blocks(jnp.concatenate(sels, axis=0), 0, n_keys)
    t2 = jnp.concatenate([t1] * NSA_KV_HEADS, axis=0)
    bias2 = jnp.where(key_pos <= t2, jnp.where(ok > 0.5, 0.0, NEG_INF), NEG_INF)
    bias = jnp.concatenate(
        [bias2[g * tq:(g + 1) * tq] for g in range(NSA_KV_HEADS) for _ in range(NSA_GROUP)], axis=0)
    s = jnp.concatenate([_dot(qbd, kvs_refs[p][:NSA_KV, :].astype(BF16)) for p in range(n_pages)]
                        + [_dot_nt(qbd, kvs_new[:, :NSA_KV])], axis=1) + bias
    m = jnp.max(s, axis=-1, keepdims=True)
    pr = jnp.where(s > MASKED_BELOW, jnp.exp(s - m), 0.0)
    den = jnp.maximum(jnp.sum(pr, axis=-1, keepdims=True), TINY)
    pr = pr.astype(BF16)
    acc = _dot(pr[:, n_pages * page:], kvs_new[:, NSA_KV:])
    for p in range(n_pages):
        acc = acc + _dot_nt(pr[:, p * page:(p + 1) * page], kvs_refs[p][NSA_KV:, :].astype(BF16))
    o_slc = own_half(acc / den)

    w_buf = win_ref.shape[1]
    wpos = jnp.concatenate([past - w_buf + lax.broadcasted_iota(jnp.int32, (1, w_buf), 1), new_k_pos], axis=1)
    ok = jnp.where(wpos <= t_rows, jnp.where(wpos > t_rows - NSA_WINDOW, jnp.where(wpos >= 0, 1.0, 0.0), 0.0), 0.0)
    s = jnp.concatenate([_dot(qbd, win_ref[:NSA_KV, :].astype(BF16)), _dot_nt(qbd, kvw_new[:, :NSA_KV])], axis=1)
    s = s + jnp.where(ok > 0.5, 0.0, NEG_INF)
    m = jnp.max(s, axis=-1, keepdims=True)
    pr = jnp.where(s > MASKED_BELOW, jnp.exp(s - m), 0.0)
    den = jnp.maximum(jnp.sum(pr, axis=-1, keepdims=True), TINY)
    pr = pr.astype(BF16)
    acc = _dot_nt(pr[:, :w_buf], win_ref[NSA_KV:, :].astype(BF16)) + _dot(pr[:, w_buf:], kvw_new[:, NSA_KV:])
    o_win = own_half(acc / den)

    for h in range(NSA_HEADS):
        rs = slice(h * tq, (h + 1) * tq)
        o = (gates[:, 3 * h:3 * h + 1] * o_cmp[rs] + gates[:, 3 * h + 1:3 * h + 2] * o_slc[rs]
             + gates[:, 3 * h + 2:3 * h + 3] * o_win[rs])
        o_ref[:, h * HEAD_DIM:(h + 1) * HEAD_DIM] = o


def nsa_attend_decode(page_table, q, gates, kvs_new, kvw_new, win_state, kcmp_pool, slc_pool, layer, past):
    n_seq, n_pages = page_table.shape
    page = slc_pool.shape[3]
    bpp = page // NSA_BLOCK
    w_buf = win_state.shape[3]

    seqs = DECODE_SEQS if n_seq % DECODE_SEQS == 0 else 1

    def seq_spec(width):
        return pl.BlockSpec((seqs, TQ_PAD, width), lambda b, pt: (b, 0, 0))

    def kcmp_spec(s, p):
        return pl.BlockSpec((None, 1, bpp * KV_ROW), lambda b, pt: (pt[b * seqs + s, p], 0, 0))

    def page_spec(s, p):
        return pl.BlockSpec((None, None, KV_ROW, page), lambda b, pt: (layer, pt[b * seqs + s, p], 0, 0))

    assert page % NSA_BLOCK == 0 and TQ_PAD <= NSA_BLOCK and n_pages * bpp + 1 <= LANE
    kern = functools.partial(_nsa_decode_kernel, n_pages=n_pages, page=page, past=past, seqs=seqs)
    grid_spec = pltpu.PrefetchScalarGridSpec(
        num_scalar_prefetch=1,
        grid=(n_seq // seqs,),
        in_specs=([seq_spec(NSA_Q), seq_spec(LANE), seq_spec(KV_ROW), seq_spec(KV_ROW),
                   pl.BlockSpec((None, seqs, KV_ROW, w_buf), lambda b, pt: (layer, b, 0, 0))]
                  + [kcmp_spec(s, p) for s in range(seqs) for p in range(n_pages)]
                  + [page_spec(s, p) for s in range(seqs) for p in range(n_pages)]),
        out_specs=seq_spec(NSA_Q),
        scratch_shapes=[pltpu.VMEM((seqs, n_pages * bpp, KV_ROW), F32)],
    )
    n_refs = seqs * n_pages
    return pl.pallas_call(
        kern,
        grid_spec=grid_spec,
        out_shape=jax.ShapeDtypeStruct((n_seq, TQ_PAD, NSA_Q), F32),
        compiler_params=_params("arbitrary"),
        name="nsa_decode",
    )(page_table, q, gates, kvs_new, kvw_new, win_state, *([kcmp_pool] * n_refs), *([slc_pool] * n_refs))


def _lane_cumsum(x):
    n = x.shape[-1]
    lane = lax.broadcasted_iota(jnp.int32, x.shape, x.ndim - 1)
    s = 1
    while s < n:
        x = x + jnp.where(lane >= s, pltpu.roll(x, s, x.ndim - 1), 0.0)
        s *= 2
    return x


def _cumsum_kernel(x_ref, o_ref):
    o_ref[...] = _lane_cumsum(x_ref[...])


def cumsum_lanes(x):
    b, r, t = x.shape
    return pl.pallas_call(
        _cumsum_kernel,
        grid=(b,),
        in_specs=[pl.BlockSpec((None, r, t), lambda i: (i, 0, 0))],
        out_specs=pl.BlockSpec((None, r, t), lambda i: (i, 0, 0)),
        out_shape=jax.ShapeDtypeStruct(x.shape, F32),
        compiler_params=_params("parallel"),
        name="logf_cumsum",
    )(x)


def _fox_prompt_kernel(q_ref, kv_ref, kv_t_ref, cum_ref, cumq_ref, o_ref,
                       qbd_scr, m_scr, l_scr, acc_scr, out_scr, *, tq):
    i = pl.program_id(1)
    lane = lax.broadcasted_iota(jnp.int32, (tq, FOX_W), 1)
    q = q_ref[...].astype(F32)
    qbd_scr[...] = jnp.concatenate(
        [jnp.where(lane // HEAD_DIM == h, q, 0.0) for h in range(FOX_HEADS)], axis=0).astype(BF16)
    cum_q = cum_ref[i]
    cum_q = jnp.concatenate([cum_q[h:h + 1, :] for h in range(FOX_HEADS)], axis=1)
    r_key = lax.broadcasted_iota(jnp.int32, (tq, tq), 0)
    c_qry = lax.broadcasted_iota(jnp.int32, (tq, tq), 1)
    causal_bias = jnp.concatenate([jnp.where(r_key <= c_qry, 0.0, NEG_INF)] * FOX_HEADS, axis=1)
    _flash_init(m_scr, l_scr, acc_scr)

    def chunks(j, n, causal_last):
        k0 = pl.multiple_of(j * tq, tq)
        cum_k = cumq_ref[pl.ds(k0, n * tq), :]
        cum_k = jnp.concatenate(
            [jnp.broadcast_to(cum_k[:, h:h + 1], (n * tq, tq)) for h in range(FOX_HEADS)], axis=1)
        s_t = _dot_nt(kv_ref[pl.ds(k0, n * tq), :FOX_W], qbd_scr[...]) + (cum_q - cum_k)
        if causal_last:
            visible = [jnp.zeros(((n - 1) * tq, FOX_HEADS * tq), F32)] if n > 1 else []
            s_t = s_t + jnp.concatenate(visible + [causal_bias], axis=0)
        p, alpha = _flash_stats(s_t, m_scr, l_scr)
        for h in range(FOX_HEADS):
            cols = slice(h * tq, (h + 1) * tq)
            rows = slice(FOX_W + h * HEAD_DIM, FOX_W + (h + 1) * HEAD_DIM)
            v_t = jnp.concatenate([kv_t_ref[j + d, rows, :] for d in range(n)], axis=1)
            acc_scr[h] = alpha[:, cols] * acc_scr[h] + _dot(v_t, p[:, cols])

    def body(jp, _):
        chunks(2 * jp, 2, False)
        return 0

    lax.fori_loop(0, i // 2, body, 0)

    @pl.when(i % 2 == 1)
    def _():
        chunks(i - 1, 2, True)

    @pl.when(i % 2 == 0)
    def _():
        chunks(i, 1, True)
    inv_l = 1.0 / jnp.maximum(l_scr[0:1, :], TINY)
    for h in range(FOX_HEADS):
        out_scr[h * HEAD_DIM:(h + 1) * HEAD_DIM, :] = acc_scr[h] * inv_l[:, h * tq:(h + 1) * tq]
    o_ref[...] = out_scr[...].T


def fox_attend_prompt(qf, kvf, kvf_t, cum, cum_q, batch, seq, tq=128):
    nq = seq // tq
    kern = functools.partial(_fox_prompt_kernel, tq=tq)
    return pl.pallas_call(
        kern,
        grid=(batch, nq),
        in_specs=[pl.BlockSpec((tq, FOX_W), lambda b, i: (b * nq + i, 0)),
                  pl.BlockSpec((seq, FOX_ROW), lambda b, i: (b, 0)),
                  pl.BlockSpec((None, nq, FOX_ROW, tq), lambda b, i: (b, 0, 0, 0)),
                  pl.BlockSpec((None, nq, 8, tq), lambda b, i: (b, 0, 0, 0)),
                  pl.BlockSpec((seq, 8), lambda b, i: (b, 0))],
        out_specs=pl.BlockSpec((tq, FOX_W), lambda b, i: (b * nq + i, 0)),
        out_shape=jax.ShapeDtypeStruct((batch * seq, FOX_W), F32),
        scratch_shapes=[pltpu.VMEM((FOX_HEADS * tq, FOX_W), BF16),
                        pltpu.VMEM((STAT_ROWS, FOX_HEADS * tq), F32),
                        pltpu.VMEM((STAT_ROWS, FOX_HEADS * tq), F32),
                        pltpu.VMEM((FOX_HEADS, HEAD_DIM, tq), F32),
                        pltpu.VMEM((FOX_W, tq), F32)],
        compiler_params=_params("parallel", "arbitrary"),
        name="fox_prompt",
    )(qf, kvf, kvf_t, jnp.swapaxes(cum.reshape(batch, 8, nq, tq), 1, 2), cum_q)


def _fox_decode_kernel(pt_ref, q_ref, kv_new_ref, lf_new_ref, *rest, n_pages, page, past, seqs):
    del pt_ref
    kv_refs = rest[:seqs * n_pages]
    lf_refs = rest[seqs * n_pages:2 * seqs * n_pages]
    o_ref = rest[2 * seqs * n_pages]
    for s in range(seqs):
        _fox_decode_one(q_ref.at[s], kv_new_ref.at[s], lf_new_ref.at[s], kv_refs[s * n_pages:(s + 1) * n_pages],
                        lf_refs[s * n_pages:(s + 1) * n_pages], o_ref.at[s], page=page, past=past)


def _fox_decode_one(q_ref, kv_new_ref, lf_new_ref, kv_refs, lf_refs, o_ref, *, page, past):
    n_pages = len(kv_refs)
    tq = TQ_PAD
    rows = FOX_HEADS * tq
    row = lax.broadcasted_iota(jnp.int32, (rows, 1), 0)
    lane = lax.broadcasted_iota(jnp.int32, (rows, FOX_W), 1)
    own_head = (lane // HEAD_DIM) == (row // tq)
    t_rows = past + row % tq

    q = q_ref[...] * SCALE
    qbd = jnp.where(own_head, jnp.concatenate([q] * FOX_HEADS, axis=0), 0.0).astype(BF16)

    def head_rows(a):
        return jnp.concatenate(
            [jnp.broadcast_to(a[h:h + 1, :], (tq, a.shape[1])) for h in range(FOX_HEADS)], axis=0)

    lf_past = jnp.concatenate([r[...] for r in lf_refs], axis=1)
    cum_past = _lane_cumsum(lf_past)
    cum_new = cum_past[:, past - 1:past] + _lane_cumsum(lf_new_ref[...])
    ck_past = head_rows(cum_past)
    ck_new = head_rows(cum_new)
    new_lane = lax.broadcasted_iota(jnp.int32, (rows, page), 1)
    cq = jnp.sum(jnp.where(new_lane == row % tq, ck_new, 0.0), axis=-1, keepdims=True)

    kv_new = jnp.concatenate([kv_new_ref[...], jnp.zeros((page - tq, FOX_ROW), F32)], axis=0).astype(BF16)
    key_pos = lax.broadcasted_iota(jnp.int32, (1, past + page), 1)
    s = jnp.concatenate([_dot(qbd, kv_refs[p][:FOX_W, :].astype(BF16)) for p in range(n_pages)]
                        + [_dot_nt(qbd, kv_new[:, :FOX_W])], axis=1)
    s = s + (cq - jnp.concatenate([ck_past, ck_new], axis=1)) + jnp.where(key_pos <= t_rows, 0.0, NEG_INF)
    m = jnp.max(s, axis=-1, keepdims=True)
    pr = jnp.where(s > MASKED_BELOW, jnp.exp(s - m), 0.0)
    den = jnp.maximum(jnp.sum(pr, axis=-1, keepdims=True), TINY)
    pr = pr.astype(BF16)
    acc = _dot(pr[:, past:], kv_new[:, FOX_W:])
    for p in range(n_pages):
        acc = acc + _dot_nt(pr[:, p * page:(p + 1) * page], kv_refs[p][FOX_W:, :].astype(BF16))
    o = jnp.where(own_head, acc / den, 0.0)
    o_ref[...] = o[0:tq] + o[tq:2 * tq] + o[2 * tq:3 * tq] + o[3 * tq:4 * tq]


def fox_attend_decode(page_table, qf, kvf_new, lf_new_t, fox_pool, lf_pool_t, layer, past):
    n_seq, n_pages = page_table.shape
    page = fox_pool.shape[3]

    seqs = DECODE_SEQS if n_seq % DECODE_SEQS == 0 else 1

    def seq_spec(r, width):
        return pl.BlockSpec((seqs, r, width), lambda b, pt: (b, 0, 0))

    def page_spec(shape, s, p):
        return pl.BlockSpec((None, None) + shape, lambda b, pt: (layer, pt[b * seqs + s, p], 0, 0))

    assert TQ_PAD <= page
    kern = functools.partial(_fox_decode_kernel, n_pages=n_pages, page=page, past=past, seqs=seqs)
    grid_spec = pltpu.PrefetchScalarGridSpec(
        num_scalar_prefetch=1,
        grid=(n_seq // seqs,),
        in_specs=([seq_spec(TQ_PAD, FOX_W), seq_spec(TQ_PAD, FOX_ROW), seq_spec(8, page)]
                  + [page_spec((FOX_ROW, page), s, p) for s in range(seqs) for p in range(n_pages)]
                  + [page_spec((8, page), s, p) for s in range(seqs) for p in range(n_pages)]),
        out_specs=seq_spec(TQ_PAD, FOX_W),
    )
    n_refs = seqs * n_pages
    return pl.pallas_call(
        kern,
        grid_spec=grid_spec,
        out_shape=jax.ShapeDtypeStruct((n_seq, TQ_PAD, FOX_W), F32),
        compiler_params=_params("arbitrary"),
        name="fox_decode",
    )(page_table, qf, kvf_new, lf_new_t, *([fox_pool] * n_refs), *([lf_pool_t] * n_refs))


CONV_PAD = 32
CONV_SUB = 64


def _ln_silu(y, g, b):
    yc = y - jnp.mean(y, axis=-1, keepdims=True)
    var = jnp.mean(yc * yc, axis=-1, keepdims=True)
    return _silu(yc * lax.rsqrt(var + LN_EPS) * g + b)


def _conv_prompt_kernel(prev_ref, cur_ref, w_ref, b_ref, g_ref, bb_ref, o_ref, ctx_scr, *, tc):
    k = pl.program_id(1)
    ctx_scr[0:CONV_PAD, :] = jnp.where(k > 0, prev_ref[...], 0.0)
    ctx_scr[CONV_PAD:, :] = cur_ref[...]
    lead = CONV_PAD - (CONV_WIDTH - 1)
    for sub in range(tc // CONV_SUB):
        acc = jnp.zeros((CONV_SUB, CONV_CH), F32) + b_ref[...]
        for w in range(CONV_WIDTH):
            acc = acc + ctx_scr[pl.ds(sub * CONV_SUB + lead + w, CONV_SUB), :] * w_ref[w:w + 1, :]
        o_ref[sub * CONV_SUB:(sub + 1) * CONV_SUB, :] = _ln_silu(acc, g_ref[...], bb_ref[...])


def conv_module_prompt(glu, dw_w, dw_b, ln_g, ln_b, layer, batch, seq, tc=256):
    tc = min(tc, seq)
    nt = seq // tc
    per = tc // CONV_PAD

    def vec_spec():
        return pl.BlockSpec((None, 1, CONV_CH), lambda b, k: (layer, 0, 0))

    return pl.pallas_call(
        functools.partial(_conv_prompt_kernel, tc=tc),
        grid=(batch, nt),
        in_specs=[pl.BlockSpec((CONV_PAD, CONV_CH), lambda b, k: (jnp.maximum((b * nt + k) * per - 1, 0), 0)),
                  pl.BlockSpec((tc, CONV_CH), lambda b, k: (b * nt + k, 0)),
                  pl.BlockSpec((None, CONV_WIDTH, CONV_CH), lambda b, k: (layer, 0, 0)),
                  vec_spec(), vec_spec(), vec_spec()],
        out_specs=pl.BlockSpec((tc, CONV_CH), lambda b, k: (b * nt + k, 0)),
        out_shape=jax.ShapeDtypeStruct((batch * seq, CONV_CH), F32),
        scratch_shapes=[pltpu.VMEM((CONV_PAD + tc, CONV_CH), F32)],
        compiler_params=_params("parallel", "arbitrary"),
        name="conv_prompt",
    )(glu, glu, dw_w, dw_b, ln_g, ln_b)


def _conv_decode_kernel(state_ref, glu_ref, w_ref, b_ref, g_ref, bb_ref, o_ref):
    n_state = state_ref.shape[0]
    t_new = glu_ref.shape[0]
    ctx = [state_ref[i] for i in range(n_state)] + [glu_ref[i] for i in range(t_new)]
    for i in range(t_new):
        acc = jnp.zeros(ctx[0].shape, F32) + b_ref[...]
        for w in range(CONV_WIDTH):
            acc = acc + ctx[i + w] * w_ref[w:w + 1, :]
        o_ref[i] = _ln_silu(acc, g_ref[...], bb_ref[...])


def conv_module_decode(state_t, glu_t, dw_w, dw_b, ln_g, ln_b, layer):
    t_new, n_seq, _ = glu_t.shape

    def vec_spec():
        return pl.BlockSpec((None, 1, CONV_CH), lambda i: (layer, 0, 0))

    return pl.pallas_call(
        _conv_decode_kernel,
        grid=(1,),
        in_specs=[pl.BlockSpec(state_t.shape, lambda i: (0, 0, 0)),
                  pl.BlockSpec(glu_t.shape, lambda i: (0, 0, 0)),
                  pl.BlockSpec((None, CONV_WIDTH, CONV_CH), lambda i: (layer, 0, 0)),
                  vec_spec(), vec_spec(), vec_spec()],
        out_specs=pl.BlockSpec((t_new, n_seq, CONV_CH), lambda i: (0, 0, 0)),
        out_shape=jax.ShapeDtypeStruct((t_new, n_seq, CONV_CH), F32),
        compiler_params=_params("arbitrary"),
        name="conv_decode",
    )(state_t, glu_t, dw_w, dw_b, ln_g, ln_b)


def _merge_kernel(x_ref, ga_ref, on_ref, yc_ref, of_ref, mg_ref, wn_ref, wc_ref, wf_ref, wo_ref, o_ref):
    merged = (mg_ref[:, 0:D_MODEL] * _dot(on_ref[...].astype(BF16), wn_ref[...])
              + mg_ref[:, D_MODEL:2 * D_MODEL] * _dot(yc_ref[...].astype(BF16), wc_ref[...])
              + mg_ref[:, 2 * D_MODEL:] * _dot(of_ref[...].astype(BF16), wf_ref[...]))
    o_ref[...] = x_ref[...] + ga_ref[...] * _dot(merged.astype(BF16), wo_ref[...])


def merge_and_project(x, mod, o_nsa, y_conv, o_fox, mg, wb_nsa, wb_conv, wb_fox, w_out, layer, tm):
    m = x.shape[0]

    def row_spec(width):
        return pl.BlockSpec((tm, width), lambda i: (i, 0))

    def w_spec(rows):
        return pl.BlockSpec((None, rows, D_MODEL), lambda i: (layer, 0, 0))

    return pl.pallas_call(
        _merge_kernel,
        grid=(m // tm,),
        in_specs=[row_spec(D_MODEL), mod.spec(5, tm), row_spec(NSA_Q), row_spec(CONV_CH), row_spec(FOX_W),
                  row_spec(3 * D_MODEL), w_spec(NSA_Q), w_spec(CONV_CH), w_spec(FOX_W), w_spec(D_MODEL)],
        out_specs=row_spec(D_MODEL),
        out_shape=jax.ShapeDtypeStruct((m, D_MODEL), F32),
        compiler_params=_params("parallel"),
        name="merge",
    )(x, mod.arr, o_nsa, y_conv, o_fox, mg, wb_nsa, wb_conv, wb_fox, w_out)


def _final_norm_kernel(x_ref, g_ref, o_ref):
    x = x_ref[...]
    o_ref[...] = x * lax.rsqrt(jnp.mean(x * x, axis=-1, keepdims=True) + RMS_EPS) * g_ref[...]


def final_norm(x, g, tm):
    m = x.shape[0]
    return pl.pallas_call(
        _final_norm_kernel,
        grid=(m // tm,),
        in_specs=[pl.BlockSpec((tm, D_MODEL), lambda i: (i, 0)), pl.BlockSpec((1, D_MODEL), lambda i: (0, 0))],
        out_specs=pl.BlockSpec((tm, D_MODEL), lambda i: (i, 0)),
        out_shape=jax.ShapeDtypeStruct((m, D_MODEL), F32),
        compiler_params=_params("parallel"),
        name="final_norm",
    )(x, g.reshape(1, D_MODEL))


def _rope_tables(pos):
    half = HEAD_DIM // 2
    inv_freq = ROPE_THETA ** (-jnp.arange(half, dtype=F32) / half)
    ang = pos.astype(F32)[:, None] * inv_freq[None, :]
    cos, sin = jnp.cos(ang), jnp.sin(ang)
    cos = jnp.concatenate([cos, cos] * (LANE // HEAD_DIM), axis=1)
    sin = jnp.concatenate([-sin, sin] * (LANE // HEAD_DIM), axis=1)
    return cos, sin


def _aligned_in_proj(w_in, b_in):
    o_gn = NSA_Q + 3 * KV_ROW
    o_glu = o_gn + 3 * NSA_HEADS
    o_qkvf = o_glu + 2 * CONV_CH
    o_lf = o_qkvf + 3 * FOX_W
    o_mg = o_lf + FOX_HEADS

    def cols(a):
        pad_gn = jnp.zeros(a.shape[:-1] + (LANE - 3 * NSA_HEADS,), a.dtype)
        pad_lf = jnp.zeros(a.shape[:-1] + (LANE - FOX_HEADS,), a.dtype)
        return jnp.concatenate([a[..., :o_gn], a[..., o_glu:o_lf], a[..., o_mg:],
                                a[..., o_gn:o_glu], pad_gn, a[..., o_lf:o_mg], pad_lf], axis=-1)

    return cols(w_in).astype(BF16), cols(b_in)[:, None, :]


def _compress_weights(pe, wk, wv):
    pe2 = jnp.concatenate([pe, pe], axis=-1)

    def block_diag(w):
        z = jnp.zeros_like(w)
        return jnp.concatenate([jnp.concatenate([w, z], axis=-1), jnp.concatenate([z, w], axis=-1)],
                               axis=-2).astype(BF16)

    return pe2, block_diag(wk), block_diag(wv)


def _pad_tokens(a, n_seq, t_new):
    a = a.reshape(n_seq, t_new, a.shape[-1])
    return jnp.pad(a, ((0, 0), (0, TQ_PAD - t_new), (0, 0)))


def kernel(x_prompt, x_sample, cache_nsa_cmp_kv, cache_nsa_slc_kv, cache_fox_kv, cache_fox_logf,
           state_nsa_win_kv, state_conv, page_table, c_prompt, c_sample, w_ada, b_ada, norm_g,
           w_ffn_gu, w_ffn_down, w_in, b_in, nsa_cmp_pe, nsa_cmp_wk, nsa_cmp_wv, conv_dw_w, conv_dw_b,
           conv_ln_g, conv_ln_b, w_branch_nsa, w_branch_conv, w_branch_fox, w_out, final_norm_g):
    batch, seq, _ = x_prompt.shape
    n_seq, t_new, _ = x_sample.shape
    depth = w_ada.shape[0]
    n_pool, page = cache_nsa_cmp_kv.shape[1], cache_nsa_cmp_kv.shape[2]
    n_pages = page_table.shape[1]
    past = n_pages * page
    w_buf = state_nsa_win_kv.shape[2]
    mp, ms = batch * seq, n_seq * t_new
    tm_p = min(512, seq)
    tm_s = min(256, ms)

    w_ada_b = w_ada.astype(BF16)
    w_gu_b = w_ffn_gu.astype(BF16)
    w_down_b = w_ffn_down.astype(BF16)
    w_in_b, b_in_p = _aligned_in_proj(w_in, b_in)
    pe2, wk_bd, wv_bd = _compress_weights(nsa_cmp_pe, nsa_cmp_wk, nsa_cmp_wv)
    pe_t, wk_t, wv_t = _compress_weights(jnp.swapaxes(nsa_cmp_pe, 1, 2), jnp.swapaxes(nsa_cmp_wk, 1, 2),
                                         jnp.swapaxes(nsa_cmp_wv, 1, 2))
    wb_nsa_b, wb_conv_b = w_branch_nsa.astype(BF16), w_branch_conv.astype(BF16)
    wb_fox_b, w_out_b = w_branch_fox.astype(BF16), w_out.astype(BF16)
    norm_g4 = norm_g[:, :, None, :]
    dw_b3, ln_g3, ln_b3 = conv_dw_b[:, None, :], conv_ln_g[:, None, :], conv_ln_b[:, None, :]

    to_feature_major = (0, 1, 3, 4, 5, 2)
    cmp_pool = jnp.transpose(cache_nsa_cmp_kv, to_feature_major).reshape(depth, n_pool * KV_ROW, page)
    slc_pool = jnp.transpose(cache_nsa_slc_kv, to_feature_major).reshape(depth, n_pool, KV_ROW, page)
    fox_pool = jnp.transpose(cache_fox_kv, to_feature_major).reshape(depth, n_pool, FOX_ROW, page)
    lf_pool_t = jnp.pad(jnp.swapaxes(cache_fox_logf, 2, 3).astype(F32),
                        ((0, 0), (0, 0), (0, 8 - FOX_HEADS), (0, 0)))
    win_state = jnp.transpose(state_nsa_win_kv, to_feature_major).reshape(depth, n_seq, KV_ROW, w_buf)

    cos_p, sin_p = _rope_tables(jnp.arange(seq, dtype=jnp.int32))
    cos_s, sin_s = _rope_tables(past + jnp.arange(t_new, dtype=jnp.int32))
    cos_s, sin_s = jnp.tile(cos_s, (n_seq, 1)), jnp.tile(sin_s, (n_seq, 1))

    mod_all = ada_all_layers(jnp.concatenate([c_prompt, c_sample], axis=0), w_ada_b, b_ada)

    xp = x_prompt.reshape(mp, D_MODEL)
    xs = x_sample.reshape(ms, D_MODEL)
    out_p = [[] for _ in range(6)]
    out_s = [[] for _ in range(6)]

    for l in range(depth):
        mod_p = Mod(mod_all[l, :batch, None, :], seq)
        mod_s = Mod(jnp.repeat(mod_all[l, batch:], t_new, axis=0), 1)

        xp = ffn_half_step(xp, mod_p, 0, norm_g4, w_gu_b, w_down_b, l, 0, tm_p)
        xs = ffn_half_step(xs, mod_s, 0, norm_g4, w_gu_b, w_down_b, l, 0, tm_s)

        tm_in = min(256, seq)
        (q_p, kvc_p, kvs_p, kvw_p, glu_p, qf_p, kvf_p, mg_p, gn_p, lf_p,
         kvc_pt, kvs_pt, kvw_pt, kvf_pt, lf_pt, kvs_pc, kvw_pc, kvf_pc) = input_projection(
            xp, mod_p, norm_g4, w_in_b, b_in_p, cos_p, sin_p, l, tm_in, seq // tm_in, feature_major_batch=batch)
        (q_s, kvc_s, kvs_s, kvw_s, glu_s, qf_s, kvf_s, mg_s, gn_s, lf_s) = input_projection(
            xs, mod_s, norm_g4, w_in_b, b_in_p, cos_s, sin_s, l, tm_s, ms // tm_s)

        kcmp_p = nsa_compress_rows(kvc_p.reshape(mp * 2, NSA_KV), pe2, wk_bd, wv_bd, l, 128)
        o_nsa_p = nsa_attend_prompt(q_p, gn_p, kcmp_p, kvs_p, kvs_pc, kvw_p, kvw_pc, batch, seq)
        cum_p = cumsum_lanes(lf_pt)
        cum_q = jnp.swapaxes(cum_p, 1, 2).reshape(mp, 8)
        o_fox_p = fox_attend_prompt(qf_p, kvf_p, kvf_pc, cum_p, cum_q, batch, seq)
        y_conv_p = conv_module_prompt(glu_p, conv_dw_w, dw_b3, ln_g3, ln_b3, l, batch, seq)

        kcmp_pool = nsa_compress_pool(cmp_pool, pe_t, wk_t, wv_t, l, 128)
        o_nsa_s = nsa_attend_decode(page_table, _pad_tokens(q_s, n_seq, t_new), _pad_tokens(gn_s, n_seq, t_new),
                                    _pad_tokens(kvs_s, n_seq, t_new), _pad_tokens(kvw_s, n_seq, t_new),
                                    win_state, kcmp_pool, slc_pool, l, past)
        o_nsa_s = o_nsa_s[:, :t_new].reshape(ms, NSA_Q)
        lf_new_t = jnp.swapaxes(lf_s[:, :FOX_HEADS].reshape(n_seq, t_new, FOX_HEADS), 1, 2)
        lf_new_t = jnp.pad(lf_new_t, ((0, 0), (0, 8 - FOX_HEADS), (0, page - t_new)))
        o_fox_s = fox_attend_decode(page_table, _pad_tokens(qf_s, n_seq, t_new), _pad_tokens(kvf_s, n_seq, t_new),
                                    lf_new_t, fox_pool, lf_pool_t, l, past)
        o_fox_s = o_fox_s[:, :t_new].reshape(ms, FOX_W)
        state_t = jnp.swapaxes(state_conv[l], 0, 1)
        glu_t = jnp.swapaxes(glu_s.reshape(n_seq, t_new, CONV_CH), 0, 1)
        y_conv_s = conv_module_decode(state_t, glu_t, conv_dw_w, dw_b3, ln_g3, ln_b3, l)
        y_conv_s = jnp.swapaxes(y_conv_s, 0, 1).reshape(ms, CONV_CH)

        xp = merge_and_project(xp, mod_p, o_nsa_p, y_conv_p, o_fox_p, mg_p,
                               wb_nsa_b, wb_conv_b, wb_fox_b, w_out_b, l, tm_p)
        xs = merge_and_project(xs, mod_s, o_nsa_s, y_conv_s, o_fox_s, mg_s,
                               wb_nsa_b, wb_conv_b, wb_fox_b, w_out_b, l, tm_s)
        xp = ffn_half_step(xp, mod_p, 6, norm_g4, w_gu_b, w_down_b, l, 1, tm_p)
        xs = ffn_half_step(xs, mod_s, 6, norm_g4, w_gu_b, w_down_b, l, 1, tm_s)

        kv5 = (2, NSA_KV_HEADS, HEAD_DIM)
        glu_p3 = glu_p.reshape(batch, seq, CONV_CH)
        w_keep = min(NSA_WINDOW, seq)

        def token_major(a, heads):
            a = a.reshape(batch, 2, heads, HEAD_DIM, a.shape[-1])
            return jnp.transpose(a, (0, 4, 1, 2, 3))

        out_p[0].append(token_major(kvc_pt, NSA_KV_HEADS))
        out_p[1].append(token_major(kvs_pt, NSA_KV_HEADS))
        out_p[2].append(token_major(kvf_pt, FOX_HEADS))
        out_p[3].append(jnp.swapaxes(lf_pt[:, :FOX_HEADS], 1, 2))
        out_p[4].append(token_major(kvw_pt[:, :, seq - w_keep:], NSA_KV_HEADS))
        out_p[5].append(glu_p3[:, seq - (CONV_WIDTH - 1):])
        kvw_s5 = kvw_s.reshape((n_seq, t_new) + kv5)
        out_s[0].append(kvc_s.reshape((n_seq, t_new) + kv5))
        out_s[1].append(kvs_s.reshape((n_seq, t_new) + kv5))
        out_s[2].append(kvf_s.reshape(n_seq, t_new, 2, FOX_HEADS, HEAD_DIM))
        out_s[3].append(lf_s[:, :FOX_HEADS].reshape(n_seq, t_new, FOX_HEADS))
        out_s[4].append(kvw_s5)
        out_s[5].append(glu_s.reshape(n_seq, t_new, CONV_CH))

    y_p = final_norm(xp, final_norm_g, tm_p).reshape(batch, seq, D_MODEL)
    y_s = final_norm(xs, final_norm_g, tm_s).reshape(n_seq, t_new, D_MODEL)
    outs = [y_p, y_s]
    for f in range(6):
        outs.append(jnp.stack(out_p[f]))
        new_s = jnp.stack(out_s[f])
        if f == 4:
            new_s = jnp.concatenate([state_nsa_win_kv, new_s], axis=2)[:, :, t_new:]
        if f == 5:
            new_s = jnp.concatenate([state_conv, new_s], axis=2)
            new_s = new_s[:, :, new_s.shape[2] - (CONV_WIDTH - 1):]
        outs.append(new_s)
    return tuple(outs)
```

```python
import functools

import jax
import jax.numpy as jnp
from jax import lax
from jax.experimental import pallas as pl
from jax.experimental.pallas import tpu as pltpu

D_MODEL = 1024
HEAD_DIM = 64
NSA_HEADS = 8
NSA_KV_HEADS = 2
NSA_GROUP = NSA_HEADS // NSA_KV_HEADS
NSA_BLOCK = 64
NSA_TOP_K = 16
NSA_WINDOW = 512
FOX_HEADS = 4
CONV_CH = 256
CONV_WIDTH = 31
D_FF = 2816
ROPE_THETA = 10000.0
RMS_EPS = 1e-6
LN_EPS = 1e-5
NEG_INF = -1e30
POS_INF = 1e30
TINY = 1e-30
MASKED_BELOW = -5e29
SCALE = HEAD_DIM ** -0.5

NSA_Q = NSA_HEADS * HEAD_DIM
NSA_KV = NSA_KV_HEADS * HEAD_DIM
FOX_W = FOX_HEADS * HEAD_DIM
KV_ROW = 2 * NSA_KV
FOX_ROW = 2 * FOX_W

LANE = 128
SUBLANES = 8
C_Q = 0
C_KVC = C_Q + NSA_Q
C_KVS = C_KVC + KV_ROW
C_KVW = C_KVS + KV_ROW
C_GLU = C_KVW + KV_ROW
C_QF = C_GLU + 2 * CONV_CH
C_KVF = C_QF + FOX_W
C_GN = C_KVF + FOX_ROW
C_LF = C_GN + LANE
D_IN_PAD = C_LF + LANE

TQ_PAD = 8
DECODE_SEQS = 2
VMEM_LIMIT = 56 * 1024 * 1024

F32 = jnp.float32
BF16 = jnp.bfloat16
NT = (((1,), (1,)), ((), ()))


def _params(*sem):
    return pltpu.CompilerParams(dimension_semantics=sem, vmem_limit_bytes=VMEM_LIMIT)


def _dot(a, b):
    return jnp.dot(a, b, preferred_element_type=F32)


def _dot_nt(a, b):
    return lax.dot_general(a, b, NT, preferred_element_type=F32)


def _rms_mod(x, g, shift, scale):
    y = x * lax.rsqrt(jnp.mean(x * x, axis=-1, keepdims=True) + RMS_EPS) * g
    return y * (1.0 + scale) + shift


def _sigmoid(x):
    return 1.0 / (1.0 + jnp.exp(-x))


def _silu(x):
    return x * _sigmoid(x)


def _ada_kernel(c_ref, w_ref, b_ref, o_ref):
    c = _silu(c_ref[...]).astype(BF16)
    o_ref[...] = _dot(c, w_ref[...]) + b_ref[...]


def ada_all_layers(cond, w_ada, b_ada):
    depth, _, n = w_ada.shape
    rows = cond.shape[0]
    tn = n // 8
    return pl.pallas_call(
        _ada_kernel,
        grid=(depth, n // tn),
        in_specs=[pl.BlockSpec((rows, D_MODEL), lambda l, j: (0, 0)),
                  pl.BlockSpec((None, D_MODEL, tn), lambda l, j: (l, 0, j)),
                  pl.BlockSpec((None, 1, tn), lambda l, j: (l, 0, j))],
        out_specs=pl.BlockSpec((None, rows, tn), lambda l, j: (l, 0, j)),
        out_shape=jax.ShapeDtypeStruct((depth, rows, n), F32),
        compiler_params=_params("parallel", "parallel"),
        name="ada",
    )(cond, w_ada, b_ada.reshape(depth, 1, n))


class Mod:
    def __init__(self, arr, tokens_per_cond):
        self.arr = arr
        self.tokens_per_cond = tokens_per_cond

    def spec(self, k, tm):
        if self.arr.ndim == 3:
            per = self.tokens_per_cond // tm
            return pl.BlockSpec((None, 1, D_MODEL), lambda m, *_: (m // per, 0, k))
        return pl.BlockSpec((tm, D_MODEL), lambda m, *_: (m, k))


def _ffn_kernel(x_ref, g_ref, sh_ref, sc_ref, ga_ref, wg_ref, wu_ref, wd_ref, o_ref, h_scr, acc_scr):
    j = pl.program_id(1)

    @pl.when(j == 0)
    def _():
        h_scr[...] = _rms_mod(x_ref[...], g_ref[...], sh_ref[...], sc_ref[...]).astype(BF16)
        acc_scr[...] = jnp.zeros_like(acc_scr)

    h = h_scr[...]
    gate = _dot(h, wg_ref[...])
    up = _dot(h, wu_ref[...])
    a = (_silu(gate) * up).astype(BF16)
    acc_scr[...] += _dot(a, wd_ref[...])

    @pl.when(j == pl.num_programs(1) - 1)
    def _():
        o_ref[...] = x_ref[...] + 0.5 * ga_ref[...] * acc_scr[...]


def ffn_half_step(x, mod, k0, norm_g, w_gu, w_down, layer, idx, tm):
    m = x.shape[0]
    nf = 2
    tf = D_FF // nf
    return pl.pallas_call(
        _ffn_kernel,
        grid=(m // tm, nf),
        in_specs=[pl.BlockSpec((tm, D_MODEL), lambda i, j: (i, 0)),
                  pl.BlockSpec((None, None, 1, D_MODEL), lambda i, j: (layer, 2 * idx, 0, 0)),
                  mod.spec(k0, tm), mod.spec(k0 + 1, tm), mod.spec(k0 + 2, tm),
                  pl.BlockSpec((None, None, D_MODEL, tf), lambda i, j: (layer, idx, 0, j)),
                  pl.BlockSpec((None, None, D_MODEL, tf), lambda i, j: (layer, idx, 0, nf + j)),
                  pl.BlockSpec((None, None, tf, D_MODEL), lambda i, j: (layer, idx, j, 0))],
        out_specs=pl.BlockSpec((tm, D_MODEL), lambda i, j: (i, 0)),
        out_shape=jax.ShapeDtypeStruct((m, D_MODEL), F32),
        scratch_shapes=[pltpu.VMEM((tm, D_MODEL), BF16), pltpu.VMEM((tm, D_MODEL), F32)],
        compiler_params=_params("parallel", "arbitrary"),
        name="ffn",
    )(x, norm_g, mod.arr, mod.arr, mod.arr, w_gu, w_gu, w_down)


def _rope(z, cos, sin):
    n = z.shape[1]
    lane = lax.broadcasted_iota(jnp.int32, z.shape, 1)
    first_half = (lane % HEAD_DIM) < (HEAD_DIM // 2)
    partner = jnp.where(first_half, pltpu.roll(z, n - HEAD_DIM // 2, 1), pltpu.roll(z, HEAD_DIM // 2, 1))
    reps = n // LANE
    if reps > 1:
        cos = jnp.concatenate([cos] * reps, axis=1)
        sin = jnp.concatenate([sin] * reps, axis=1)
    return z * cos + partner * sin


def _inproj_kernel(x_ref, g_ref, sh_ref, sc_ref, cos_ref, sin_ref, w_ref, b_ref,
                   q_ref, kvc_ref, kvs_ref, kvw_ref, glu_ref, qf_ref, kvf_ref, gn_ref, lf_ref,
                   *feature_major_refs):
    u = _rms_mod(x_ref[...], g_ref[...], sh_ref[...], sc_ref[...]).astype(BF16)
    cos = cos_ref[...]
    sin = sin_ref[...]

    def proj(lo, hi):
        return _dot(u, w_ref[:, lo:hi]) + b_ref[:, lo:hi]

    def as_operand(ref, val, scale=None):
        if ref.dtype == BF16 and scale is not None:
            val = val * scale
        ref[...] = val.astype(ref.dtype)

    as_operand(q_ref, _rope(proj(C_Q, C_KVC), cos, sin), SCALE)
    kv_rows = []
    for ref, lo in ((kvc_ref, C_KVC), (kvs_ref, C_KVS), (kvw_ref, C_KVW)):
        z = proj(lo, lo + KV_ROW)
        z = jnp.concatenate([_rope(z[:, :NSA_KV], cos, sin), z[:, NSA_KV:]], axis=1)
        as_operand(ref, z)
        kv_rows.append(z)
    z = proj(C_GLU, C_QF)
    glu_ref[...] = z[:, :CONV_CH] * _sigmoid(z[:, CONV_CH:])
    as_operand(qf_ref, proj(C_QF, C_KVF), SCALE)
    kvf = proj(C_KVF, C_GN)
    as_operand(kvf_ref, kvf)
    gn_ref[...] = _sigmoid(proj(C_GN, C_LF))
    z = proj(C_LF, D_IN_PAD)
    lf = jnp.minimum(z, 0.0) - jnp.log(1.0 + jnp.exp(-jnp.abs(z)))
    lf_ref[...] = lf
    if feature_major_refs:
        kvc_t, kvs_t, kvw_t, kvf_t, lf_t, kvs_c, kvw_c, kvf_c = feature_major_refs
        for ref, rows, chunked in ((kvc_t, kv_rows[0], None), (kvs_t, kv_rows[1], kvs_c),
                                   (kvw_t, kv_rows[2], kvw_c), (kvf_t, kvf, kvf_c)):
            rows_t = rows.T
            ref[...] = rows_t
            if chunked is not None:
                for c in range(chunked.shape[0]):
                    chunked[c] = rows_t[:, c * LANE:(c + 1) * LANE].astype(BF16)
        lf_t[...] = lf.T[:lf_t.shape[0], :]


def input_projection(x, mod, norm_g, w_in, b_in, cos, sin, layer, tm, pos_tiles, feature_major_batch=None):
    m = x.shape[0]
    widths = (NSA_Q, KV_ROW, KV_ROW, KV_ROW, CONV_CH, FOX_W, FOX_ROW, LANE, LANE)
    out_specs = [pl.BlockSpec((tm, w), lambda i: (i, 0)) for w in widths]
    operand = (0, 2, 3, 5, 6) if feature_major_batch else ()
    out_shape = [jax.ShapeDtypeStruct((m, w), BF16 if k in operand else F32) for k, w in enumerate(widths)]
    if feature_major_batch:
        t = m // feature_major_batch
        for rows in (KV_ROW, KV_ROW, KV_ROW, FOX_ROW, 8):
            out_specs.append(pl.BlockSpec((None, rows, tm), lambda i: (i // pos_tiles, 0, i % pos_tiles)))
            out_shape.append(jax.ShapeDtypeStruct((feature_major_batch, rows, t), F32))
        for rows in (KV_ROW, KV_ROW, FOX_ROW):
            out_specs.append(pl.BlockSpec((None, tm // LANE, rows, LANE),
                                          lambda i: (i // pos_tiles, i % pos_tiles, 0, 0)))
            out_shape.append(jax.ShapeDtypeStruct((feature_major_batch, t // LANE, rows, LANE), BF16))
    return pl.pallas_call(
        _inproj_kernel,
        grid=(m // tm,),
        in_specs=[pl.BlockSpec((tm, D_MODEL), lambda i: (i, 0)),
                  pl.BlockSpec((None, None, 1, D_MODEL), lambda i: (layer, 1, 0, 0)),
                  mod.spec(3, tm), mod.spec(4, tm),
                  pl.BlockSpec((tm, LANE), lambda i: (i % pos_tiles, 0)),
                  pl.BlockSpec((tm, LANE), lambda i: (i % pos_tiles, 0)),
                  pl.BlockSpec((None, D_MODEL, D_IN_PAD), lambda i: (layer, 0, 0)),
                  pl.BlockSpec((None, 1, D_IN_PAD), lambda i: (layer, 0, 0))],
        out_specs=out_specs,
        out_shape=out_shape,
        compiler_params=_params("parallel"),
        name="inproj",
    )(x, norm_g, mod.arr, mod.arr, cos, sin, w_in, b_in)


def _compress_kernel(x_ref, pe_ref, wk_ref, wv_ref, o_ref):
    nblk = o_ref.shape[0]
    acc_k = jnp.zeros((nblk, NSA_KV), F32)
    acc_v = jnp.zeros((nblk, NSA_KV), F32)
    for l in range(NSA_BLOCK):
        pe = pe_ref[l:l + 1, :]
        k_rows = x_ref[pl.ds(2 * l, nblk, stride=2 * NSA_BLOCK), :] + pe
        v_rows = x_ref[pl.ds(2 * l + 1, nblk, stride=2 * NSA_BLOCK), :] + pe
        acc_k = acc_k + _dot(k_rows.astype(BF16), wk_ref[l])
        acc_v = acc_v + _dot(v_rows.astype(BF16), wv_ref[l])
    o_ref[:, :NSA_KV] = acc_k
    o_ref[:, NSA_KV:] = acc_v


def nsa_compress_rows(rows, pe2, wk_bd, wv_bd, layer, blocks_per_step):
    lead = rows.shape[:-2]
    nblk = rows.shape[-2] // (2 * NSA_BLOCK)
    nb = max(d for d in range(8, blocks_per_step + 1, 8) if nblk % d == 0)
    if lead:
        x_spec = pl.BlockSpec((None, nb * 2 * NSA_BLOCK, NSA_KV), lambda i: (layer, i, 0))
    else:
        x_spec = pl.BlockSpec((nb * 2 * NSA_BLOCK, NSA_KV), lambda i: (i, 0))
    w_spec = pl.BlockSpec((None, NSA_BLOCK, NSA_KV, NSA_KV), lambda i: (layer, 0, 0, 0))
    return pl.pallas_call(
        _compress_kernel,
        grid=(nblk // nb,),
        in_specs=[x_spec, pl.BlockSpec((None, NSA_BLOCK, NSA_KV), lambda i: (layer, 0, 0)), w_spec, w_spec],
        out_specs=pl.BlockSpec((nb, KV_ROW), lambda i: (i, 0)),
        out_shape=jax.ShapeDtypeStruct((nblk, KV_ROW), F32),
        compiler_params=_params("parallel"),
        name="nsa_compress",
    )(rows, pe2, wk_bd, wv_bd)


def _compress_pool_kernel(x_ref, pe_ref, wk_ref, wv_ref, o_ref, *, blocks_per_page):
    pages = o_ref.shape[0]
    page = x_ref.shape[1]
    for c in range(2 * NSA_KV_HEADS):
        w_ref = wk_ref if c < NSA_KV_HEADS else wv_ref
        acc = jnp.zeros((pages, page), F32)
        for d in range(HEAD_DIM):
            rows = x_ref[pl.ds(c * HEAD_DIM + d, pages, stride=KV_ROW), :] + pe_ref[d:d + 1, :]
            acc = acc + _dot(rows.astype(BF16), w_ref[d])
        for j in range(blocks_per_page):
            o_ref[:, j * KV_ROW + c * HEAD_DIM:j * KV_ROW + (c + 1) * HEAD_DIM] = acc[:, j * HEAD_DIM:(j + 1) * HEAD_DIM]


def nsa_compress_pool(pool_t, pe_t, wk_t, wv_t, layer, pages_per_step):
    page = pool_t.shape[-1]
    n_pool = pool_t.shape[1] // KV_ROW
    bpp = page // NSA_BLOCK
    assert page == LANE and bpp * HEAD_DIM == page
    pp = max(d for d in range(8, pages_per_step + 1, 8) if n_pool % d == 0)
    w_spec = pl.BlockSpec((None, HEAD_DIM, page, page), lambda i: (layer, 0, 0, 0))
    out = pl.pallas_call(
        functools.partial(_compress_pool_kernel, blocks_per_page=bpp),
        grid=(n_pool // pp,),
        in_specs=[pl.BlockSpec((None, pp * KV_ROW, page), lambda i: (layer, i, 0)),
                  pl.BlockSpec((None, HEAD_DIM, page), lambda i: (layer, 0, 0)), w_spec, w_spec],
        out_specs=pl.BlockSpec((pp, bpp * KV_ROW), lambda i: (i, 0)),
        out_shape=jax.ShapeDtypeStruct((n_pool, bpp * KV_ROW), F32),
        compiler_params=_params("parallel"),
        name="nsa_compress_pool",
    )(pool_t, pe_t, wk_t, wv_t)
    return out.reshape(n_pool, 1, bpp * KV_ROW)


def _softmax_rows(logits):
    m = jnp.max(logits, axis=-1, keepdims=True)
    p = jnp.where(logits > MASKED_BELOW, jnp.exp(logits - m), 0.0)
    return p / jnp.maximum(jnp.sum(p, axis=-1, keepdims=True), TINY)


STAT_ROWS = 8


def _flash_init(m_scr, l_scr, acc_scr):
    m_scr[...] = jnp.full(m_scr.shape, NEG_INF, F32)
    l_scr[...] = jnp.zeros(l_scr.shape, F32)
    acc_scr[...] = jnp.zeros(acc_scr.shape, F32)


def _flash_stats(s_t, m_scr, l_scr):
    m = m_scr[0:1, :]
    m_new = jnp.maximum(m, jnp.max(s_t, axis=0, keepdims=True))
    alpha = jnp.exp(m - m_new)
    m_safe = jnp.where(m_new > MASKED_BELOW, m_new, 0.0)
    ps, sums = [], []
    for c in range(0, s_t.shape[1], LANE):
        p = jnp.exp(s_t[:, c:c + LANE] - m_safe[:, c:c + LANE])
        sums.append(jnp.sum(p, axis=0, keepdims=True))
        ps.append(p.astype(BF16))
    l_scr[0:1, :] = alpha * l_scr[0:1, :] + jnp.concatenate(sums, axis=1)
    m_scr[0:1, :] = m_new
    return jnp.concatenate(ps, axis=1), alpha


def _select_blocks(score, t, nb):
    blk = lax.broadcasted_iota(jnp.int32, score.shape, 1)
    cur = t // NSA_BLOCK
    forced = jnp.where(blk == 0, 1, jnp.where(blk == cur, 1, jnp.where(blk == cur - 1, 1, 0)))
    s = jnp.where(forced == 1, POS_INF, jnp.where(blk > cur, NEG_INF, score))
    rank = jnp.zeros(score.shape, jnp.int32)
    for i in range(nb):
        col = s[:, i:i + 1]
        ahead = jnp.where(col > s, 1, jnp.where(col == s, jnp.where(blk > i, 1, 0), 0))
        rank = rank + ahead
    k_sel = min(NSA_TOP_K, nb)
    return jnp.where(rank < k_sel, jnp.where(blk < nb, 1.0, 0.0), 0.0)


def _expand_blocks(sel, k0, kc):
    width = sel.shape[1]
    kblk = (k0 + lax.broadcasted_iota(jnp.int32, (width, kc), 1)) // NSA_BLOCK
    e = jnp.where(kblk == lax.broadcasted_iota(jnp.int32, (width, kc), 0), 1.0, 0.0).astype(BF16)
    return _dot(sel.astype(BF16), e)


def _select_blocks_t(score_t, t_row, nb):
    blk = lax.broadcasted_iota(jnp.int32, score_t.shape, 0)
    cur = t_row // NSA_BLOCK
    forced = jnp.where(blk == 0, 1, jnp.where(blk == cur, 1, jnp.where(blk == cur - 1, 1, 0)))
    s = jnp.where(forced == 1, POS_INF, jnp.where(blk > cur, NEG_INF, score_t))
    rank = jnp.zeros(score_t.shape, jnp.int32)
    for i in range(nb):
        other = s[i:i + 1, :]
        rank = rank + jnp.where(other > s, 1, jnp.where(other == s, jnp.where(blk > i, 1, 0), 0))
    return jnp.where(rank < min(NSA_TOP_K, nb), 1.0, 0.0)


def _nsa_prompt_kernel(q_ref, gate_ref, kcmp_ref, kvs_ref, kvs_t_ref, kvw_ref, kvw_t_ref, o_ref,
                       qbd_scr, expand_scr, bias_scr, m_scr, l_scr, acc_scr, out_scr, *, tq):
    i = pl.program_id(1)
    t0 = i * tq
    t_row = t0 + lax.broadcasted_iota(jnp.int32, (1, tq), 1)
    nc = kcmp_ref.shape[0]
    lane = lax.broadcasted_iota(jnp.int32, (tq, LANE), 1)
    low_half = lane < HEAD_DIM

    pieces = []
    for h in range(NSA_HEADS):
        g = h // NSA_GROUP
        pair = q_ref[:, (h // 2) * LANE:(h // 2 + 1) * LANE].astype(F32)
        if h % 2 != g:
            pair = pltpu.roll(pair, HEAD_DIM, 1)
        pieces.append(jnp.where(low_half, pair, 0.0) if g == 0 else jnp.where(low_half, 0.0, pair))
    qbd_scr[...] = jnp.concatenate(pieces, axis=0).astype(BF16)
    qbd = qbd_scr[...]
    gates_t = gate_ref[...].T

    kcm = kcmp_ref[:, :NSA_KV].astype(BF16)
    vcm_t = jnp.concatenate([kcmp_ref[:, NSA_KV:], jnp.zeros((LANE - nc, NSA_KV), F32)], axis=0).T
    blk_t = lax.broadcasted_iota(jnp.int32, (nc, 1), 0)
    cmp_bias_t = jnp.where((blk_t + 1) * NSA_BLOCK - 1 <= t_row, 0.0, NEG_INF)
    lc_t = _dot_nt(kcm, qbd) + jnp.concatenate([cmp_bias_t] * NSA_HEADS, axis=1)
    m_t = jnp.max(lc_t, axis=0, keepdims=True)
    p_t = jnp.where(lc_t > MASKED_BELOW, jnp.exp(lc_t - m_t), 0.0)
    pc_t = p_t / jnp.maximum(jnp.sum(p_t, axis=0, keepdims=True), TINY)
    o_cmp_t = _dot(vcm_t[:, :nc].astype(BF16), pc_t.astype(BF16))
    for h in range(NSA_HEADS):
        g = h // NSA_GROUP
        out_scr[h * HEAD_DIM:(h + 1) * HEAD_DIM, :] = (
            gates_t[3 * h:3 * h + 1, :] * o_cmp_t[g * HEAD_DIM:(g + 1) * HEAD_DIM, h * tq:(h + 1) * tq])

    r_key = lax.broadcasted_iota(jnp.int32, (tq, tq), 0)
    c_qry = lax.broadcasted_iota(jnp.int32, (tq, tq), 1)
    sels = []
    for g in range(NSA_KV_HEADS):
        base = g * NSA_GROUP * tq
        score_t = (pc_t[:, base:base + tq] + pc_t[:, base + tq:base + 2 * tq]
                   + pc_t[:, base + 2 * tq:base + 3 * tq] + pc_t[:, base + 3 * tq:base + 4 * tq])
        sel_t = _select_blocks_t(score_t, t_row, nc)
        sels.append(jnp.concatenate([sel_t, jnp.zeros((LANE - nc, tq), F32)], axis=0))
    sel_t = jnp.concatenate(sels, axis=1).astype(BF16)

    n_chunks = bias_scr.shape[1]
    span = min(4, n_chunks)

    @pl.when(i == 0)
    def _():
        kblk = lax.broadcasted_iota(jnp.int32, expand_scr.shape, 0) // NSA_BLOCK
        expand_scr[...] = jnp.where(kblk == lax.broadcasted_iota(jnp.int32, expand_scr.shape, 1),
                                    1.0, 0.0).astype(BF16)

    t_rows = jnp.concatenate([t_row] * NSA_KV_HEADS, axis=1)
    for k in range(n_chunks // span):
        @pl.when(i + 1 >= k * span)
        def _(k=k):
            first = k * span * tq
            ok = _dot(expand_scr[first:first + span * tq, :], sel_t)
            kpos = first + lax.broadcasted_iota(jnp.int32, (span * tq, 1), 0)
            bias = jnp.where(kpos <= t_rows, jnp.where(ok > 0.5, 0.0, NEG_INF), NEG_INF)
            for jj in range(span):
                for g in range(NSA_KV_HEADS):
                    bias_scr[g, k * span + jj] = bias[jj * tq:(jj + 1) * tq, g * tq:(g + 1) * tq]

    group_w = NSA_GROUP * tq

    def attend_chunk(keys, vals_t, bias_per_group):
        s_t = _dot_nt(keys, qbd_scr[...])
        if bias_per_group is not None:
            s_t = s_t + jnp.concatenate([b for b in bias_per_group for _ in range(NSA_GROUP)], axis=1)
        p, alpha = _flash_stats(s_t, m_scr, l_scr)
        for g in range(NSA_KV_HEADS):
            cols = slice(g * group_w, (g + 1) * group_w)
            rows = slice(NSA_KV + g * HEAD_DIM, NSA_KV + (g + 1) * HEAD_DIM)
            v_t = jnp.concatenate([v[rows, :] for v in vals_t], axis=1)
            acc_scr[g] = alpha[:, cols] * acc_scr[g] + _dot(v_t, p[:, cols])

    def finish_branch(branch):
        inv_l = 1.0 / jnp.maximum(l_scr[0:1, :], TINY)
        for h in range(NSA_HEADS):
            g, r = divmod(h, NSA_GROUP)
            rs = slice(h * HEAD_DIM, (h + 1) * HEAD_DIM)
            o_t = acc_scr[g, :, r * tq:(r + 1) * tq] * inv_l[:, h * tq:(h + 1) * tq]
            out_scr[rs, :] = out_scr[rs, :] + gates_t[3 * h + branch:3 * h + branch + 1, :] * o_t

    _flash_init(m_scr, l_scr, acc_scr)
    pair = min(2, n_chunks)

    def slc_body(jp, _):
        j = jp * pair
        k0 = pl.multiple_of(j * tq, pair * tq)
        attend_chunk(kvs_ref[pl.ds(k0, pair * tq), :NSA_KV], [kvs_t_ref.at[j + d] for d in range(pair)],
                     [jnp.concatenate([bias_scr[g, j + d] for d in range(pair)], axis=0)
                      for g in range(NSA_KV_HEADS)])
        return 0

    lax.fori_loop(0, i // pair + 1, slc_body, 0)
    finish_branch(1)

    _flash_init(m_scr, l_scr, acc_scr)
    n_win = NSA_WINDOW // tq + 1

    def win_chunks(c_first, n, j_first):
        k0 = pl.multiple_of(j_first * tq, tq)
        biases = []
        for c in range(c_first, c_first + n):
            bias = jnp.zeros((tq, tq), F32)
            if c == 0:
                bias = bias + jnp.where(r_key > c_qry, 0.0, NEG_INF)
            if c == n_win - 1:
                bias = bias + jnp.where(r_key <= c_qry, 0.0, NEG_INF)
            biases.append(bias)
        masked = c_first == 0 or c_first + n == n_win
        attend_chunk(kvw_ref[pl.ds(k0, n * tq), :NSA_KV], [kvw_t_ref.at[j_first + d] for d in range(n)],
                     [jnp.concatenate(biases, axis=0)] * NSA_KV_HEADS if masked else None)

    c = 0
    while c < n_win:
        n = 2 if c + 1 < n_win - 1 else 1
        jc = i - (n_win - 1) + c
        pl.when(jc >= 0)(functools.partial(win_chunks, c, n, jc))
        if n == 2:
            pl.when(jc == -1)(functools.partial(win_chunks, c + 1, 1, 0))
        c += n

    finish_branch(2)
    o_ref[...] = out_scr[...].T


def nsa_attend_prompt(q, gates, kcmp, kvs, kvs_t, kvw, kvw_t, batch, seq, tq=128):
    nq = seq // tq
    nc = seq // NSA_BLOCK
    assert nc <= LANE and NSA_WINDOW % tq == 0 and tq == LANE and nq % min(4, nq) == 0
    kern = functools.partial(_nsa_prompt_kernel, tq=tq)
    return pl.pallas_call(
        kern,
        grid=(batch, nq),
        in_specs=[pl.BlockSpec((tq, NSA_Q), lambda b, i: (b * nq + i, 0)),
                  pl.BlockSpec((tq, LANE), lambda b, i: (b * nq + i, 0)),
                  pl.BlockSpec((nc, KV_ROW), lambda b, i: (b, 0)),
                  pl.BlockSpec((seq, KV_ROW), lambda b, i: (b, 0)),
                  pl.BlockSpec((None, nq, KV_ROW, tq), lambda b, i: (b, 0, 0, 0)),
                  pl.BlockSpec((seq, KV_ROW), lambda b, i: (b, 0)),
                  pl.BlockSpec((None, nq, KV_ROW, tq), lambda b, i: (b, 0, 0, 0))],
        out_specs=pl.BlockSpec((tq, NSA_Q), lambda b, i: (b * nq + i, 0)),
        out_shape=jax.ShapeDtypeStruct((batch * seq, NSA_Q), F32),
        scratch_shapes=[pltpu.VMEM((NSA_HEADS * tq, LANE), BF16),
                        pltpu.VMEM((seq, LANE), BF16),
                        pltpu.VMEM((NSA_KV_HEADS, nq, tq, tq), F32),
                        pltpu.VMEM((STAT_ROWS, NSA_HEADS * tq), F32),
                        pltpu.VMEM((STAT_ROWS, NSA_HEADS * tq), F32),
                        pltpu.VMEM((NSA_KV_HEADS, HEAD_DIM, NSA_GROUP * tq), F32),
                        pltpu.VMEM((NSA_Q, tq), F32)],
        compiler_params=_params("parallel", "arbitrary"),
        name="nsa_prompt",
    )(q, gates, kcmp, kvs, kvs_t, kvw, kvw_t)


def _nsa_decode_kernel(pt_ref, q_ref, gate_ref, kvs_new_ref, kvw_new_ref, win_ref, *rest,
                       n_pages, page, past, seqs):
    del pt_ref
    kcmp_refs = rest[:seqs * n_pages]
    kvs_refs = rest[seqs * n_pages:2 * seqs * n_pages]
    o_ref, kcmp_scr = rest[2 * seqs * n_pages:]
    for s in range(seqs):
        _nsa_decode_one(q_ref.at[s], gate_ref.at[s], kvs_new_ref.at[s], kvw_new_ref.at[s], win_ref.at[s],
                        kcmp_refs[s * n_pages:(s + 1) * n_pages], kvs_refs[s * n_pages:(s + 1) * n_pages],
                        o_ref.at[s], kcmp_scr.at[s], n_pages=n_pages, page=page, past=past)


def _nsa_decode_one(q_ref, gate_ref, kvs_new_ref, kvw_new_ref, win_ref, kcmp_refs, kvs_refs, o_ref, kcmp_scr,
                    *, n_pages, page, past):
    tq = TQ_PAD
    rows = NSA_HEADS * tq
    blocks_per_page = page // NSA_BLOCK
    nc = n_pages * blocks_per_page
    nb = nc + 1
    for p in range(n_pages):
        for j in range(blocks_per_page):
            kcmp_scr[p * blocks_per_page + j:p * blocks_per_page + j + 1, :] = (
                kcmp_refs[p][:, j * KV_ROW:(j + 1) * KV_ROW])

    row = lax.broadcasted_iota(jnp.int32, (rows, 1), 0)
    t_rows = past + row % tq
    t1 = past + lax.broadcasted_iota(jnp.int32, (tq, 1), 0)
    lane = lax.broadcasted_iota(jnp.int32, (rows, LANE), 1)
    own_group = (lane // HEAD_DIM) == (row // (NSA_GROUP * tq))

    q = q_ref[...] * SCALE
    pieces = []
    for h in range(NSA_HEADS):
        qh = q[:, h * HEAD_DIM:(h + 1) * HEAD_DIM]
        zero = jnp.zeros_like(qh)
        pieces.append(jnp.concatenate([qh, zero] if h < NSA_GROUP else [zero, qh], axis=1))
    qbd = jnp.concatenate(pieces, axis=0).astype(BF16)

    def own_half(o):
        o = jnp.where(own_group, o, 0.0)
        return o[:, :HEAD_DIM] + o[:, HEAD_DIM:]

    kcm = kcmp_scr[:, :NSA_KV].astype(BF16)
    vcm = kcmp_scr[:, NSA_KV:].astype(BF16)
    blk = lax.broadcasted_iota(jnp.int32, (1, nc), 1)
    lc = _dot_nt(qbd, kcm) + jnp.where((blk + 1) * NSA_BLOCK - 1 <= t_rows, 0.0, NEG_INF)
    pc = _softmax_rows(lc)
    o_cmp = own_half(_dot(pc.astype(BF16), vcm))

    gates = gate_ref[...]
    new_k_pos = past + lax.broadcasted_iota(jnp.int32, (1, page), 1)
    pad = jnp.zeros((page - tq, KV_ROW), F32)
    kvs_new = jnp.concatenate([kvs_new_ref[...], pad], axis=0).astype(BF16)
    kvw_new = jnp.concatenate([kvw_new_ref[...], pad], axis=0).astype(BF16)

    n_keys = (n_pages + 1) * page
    key_pos = lax.broadcasted_iota(jnp.int32, (1, n_keys), 1)
    sels = []
    for g in range(NSA_KV_HEADS):
        pcg = pc[g * NSA_GROUP * tq:(g + 1) * NSA_GROUP * tq]
        score = pcg[0:tq] + pcg[tq:2 * tq] + pcg[2 * tq:3 * tq] + pcg[3 * tq:4 * tq]
        score = jnp.concatenate([score, jnp.zeros((tq, LANE - nc), F32)], axis=1)
        sels.append(_select_blocks(score, t1, nb))
    ok = _expand_blocks(jnp.concatenate(sels, axis=0), 0, n_keys)
    t2 = jnp.concatenate([t1] * NSA_KV_HEADS, axis=0)
    bias2 = jnp.where(key_pos <= t2, jnp.where(ok > 0.5, 0.0, NEG_INF), NEG_INF)
    bias = jnp.concatenate(
        [bias2[g * tq:(g + 1) * tq] for g in range(NSA_KV_HEADS) for _ in range(NSA_GROUP)], axis=0)
    s = jnp.concatenate([_dot(qbd, kvs_refs[p][:NSA_KV, :].astype(BF16)) for p in range(n_pages)]
                        + [_dot_nt(qbd, kvs_new[:, :NSA_KV])], axis=1) + bias
    m = jnp.max(s, axis=-1, keepdims=True)
    pr = jnp.where(s > MASKED_BELOW, jnp.exp(s - m), 0.0)
    den = jnp.maximum(jnp.sum(pr, axis=-1, keepdims=True), TINY)
    pr = pr.astype(BF16)
    acc = _dot(pr[:, n_pages * page:], kvs_new[:, NSA_KV:])
    for p in range(n_pages):
        acc = acc + _dot_nt(pr[:, p * page:(p + 1) * page], kvs_refs[p][NSA_KV:, :].astype(BF16))
    o_slc = own_half(acc / den)

    w_buf = win_ref.shape[1]
    wpos = jnp.concatenate([past - w_buf + lax.broadcasted_iota(jnp.int32, (1, w_buf), 1), new_k_pos], axis=1)
    ok = jnp.where(wpos <= t_rows, jnp.where(wpos > t_rows - NSA_WINDOW, jnp.where(wpos >= 0, 1.0, 0.0), 0.0), 0.0)
    s = jnp.concatenate([_dot(qbd, win_ref[:NSA_KV, :].astype(BF16)), _dot_nt(qbd, kvw_new[:, :NSA_KV])], axis=1)
    s = s + jnp.where(ok > 0.5, 0.0, NEG_INF)
    m = jnp.max(s, axis=-1, keepdims=True)
    pr = jnp.where(s > MASKED_BELOW, jnp.exp(s - m), 0.0)
    den = jnp.maximum(jnp.sum(pr, axis=-1, keepdims=True), TINY)
    pr = pr.astype(BF16)
    acc = _dot_nt(pr[:, :w_buf], win_ref[NSA_KV:, :].astype(BF16)) + _dot(pr[:, w_buf:], kvw_new[:, NSA_KV:])
    o_win = own_half(acc / den)

    for h in range(NSA_HEADS):
        rs = slice(h * tq, (h + 1) * tq)
        o = (gates[:, 3 * h:3 * h + 1] * o_cmp[rs] + gates[:, 3 * h + 1:3 * h + 2] * o_slc[rs]
             + gates[:, 3 * h + 2:3 * h + 3] * o_win[rs])
        o_ref[:, h * HEAD_DIM:(h + 1) * HEAD_DIM] = o


def nsa_attend_decode(page_table, q, gates, kvs_new, kvw_new, win_state, kcmp_pool, slc_pool, layer, past):
    n_seq, n_pages = page_table.shape
    page = slc_pool.shape[3]
    bpp = page // NSA_BLOCK
    w_buf = win_state.shape[3]

    seqs = DECODE_SEQS if n_seq % DECODE_SEQS == 0 else 1

    def seq_spec(width):
        return pl.BlockSpec((seqs, TQ_PAD, width), lambda b, pt: (b, 0, 0))

    def kcmp_spec(s, p):
        return pl.BlockSpec((None, 1, bpp * KV_ROW), lambda b, pt: (pt[b * seqs + s, p], 0, 0))

    def page_spec(s, p):
        return pl.BlockSpec((None, None, KV_ROW, page), lambda b, pt: (layer, pt[b * seqs + s, p], 0, 0))

    assert page % NSA_BLOCK == 0 and TQ_PAD <= NSA_BLOCK and n_pages * bpp + 1 <= LANE
    kern = functools.partial(_nsa_decode_kernel, n_pages=n_pages, page=page, past=past, seqs=seqs)
    grid_spec = pltpu.PrefetchScalarGridSpec(
        num_scalar_prefetch=1,
        grid=(n_seq // seqs,),
        in_specs=([seq_spec(NSA_Q), seq_spec(LANE), seq_spec(KV_ROW), seq_spec(KV_ROW),
                   pl.BlockSpec((None, seqs, KV_ROW, w_buf), lambda b, pt: (layer, b, 0, 0))]
                  + [kcmp_spec(s, p) for s in range(seqs) for p in range(n_pages)]
                  + [page_spec(s, p) for s in range(seqs) for p in range(n_pages)]),
        out_specs=seq_spec(NSA_Q),
        scratch_shapes=[pltpu.VMEM((seqs, n_pages * bpp, KV_ROW), F32)],
    )
    n_refs = seqs * n_pages
    return pl.pallas_call(
        kern,
        grid_spec=grid_spec,
        out_shape=jax.ShapeDtypeStruct((n_seq, TQ_PAD, NSA_Q), F32),
        compiler_params=_params("arbitrary"),
        name="nsa_decode",
    )(page_table, q, gates, kvs_new, kvw_new, win_state, *([kcmp_pool] * n_refs), *([slc_pool] * n_refs))


def _lane_cumsum(x):
    n = x.shape[-1]
    lane = lax.broadcasted_iota(jnp.int32, x.shape, x.ndim - 1)
    s = 1
    while s < n:
        x = x + jnp.where(lane >= s, pltpu.roll(x, s, x.ndim - 1), 0.0)
        s *= 2
    return x


def _cumsum_kernel(x_ref, o_ref):
    o_ref[...] = _lane_cumsum(x_ref[...])


def cumsum_lanes(x):
    b, r, t = x.shape
    return pl.pallas_call(
        _cumsum_kernel,
        grid=(b,),
        in_specs=[pl.BlockSpec((None, r, t), lambda i: (i, 0, 0))],
        out_specs=pl.BlockSpec((None, r, t), lambda i: (i, 0, 0)),
        out_shape=jax.ShapeDtypeStruct(x.shape, F32),
        compiler_params=_params("parallel"),
        name="logf_cumsum",
    )(x)


def _fox_prompt_kernel(q_ref, kv_ref, kv_t_ref, cum_ref, cumq_ref, o_ref,
                       qbd_scr, m_scr, l_scr, acc_scr, out_scr, *, tq):
    i = pl.program_id(1)
    lane = lax.broadcasted_iota(jnp.int32, (tq, FOX_W), 1)
    q = q_ref[...].astype(F32)
    qbd_scr[...] = jnp.concatenate(
        [jnp.where(lane // HEAD_DIM == h, q, 0.0) for h in range(FOX_HEADS)], axis=0).astype(BF16)
    cum_q = cum_ref[i]
    cum_q = jnp.concatenate([cum_q[h:h + 1, :] for h in range(FOX_HEADS)], axis=1)
    r_key = lax.broadcasted_iota(jnp.int32, (tq, tq), 0)
    c_qry = lax.broadcasted_iota(jnp.int32, (tq, tq), 1)
    causal_bias = jnp.concatenate([jnp.where(r_key <= c_qry, 0.0, NEG_INF)] * FOX_HEADS, axis=1)
    _flash_init(m_scr, l_scr, acc_scr)

    def chunks(j, n, causal_last):
        k0 = pl.multiple_of(j * tq, tq)
        cum_k = cumq_ref[pl.ds(k0, n * tq), :]
        cum_k = jnp.concatenate(
            [jnp.broadcast_to(cum_k[:, h:h + 1], (n * tq, tq)) for h in range(FOX_HEADS)], axis=1)
        s_t = _dot_nt(kv_ref[pl.ds(k0, n * tq), :FOX_W], qbd_scr[...]) + (cum_q - cum_k)
        if causal_last:
            visible = [jnp.zeros(((n - 1) * tq, FOX_HEADS * tq), F32)] if n > 1 else []
            s_t = s_t + jnp.concatenate(visible + [causal_bias], axis=0)
        p, alpha = _flash_stats(s_t, m_scr, l_scr)
        for h in range(FOX_HEADS):
            cols = slice(h * tq, (h + 1) * tq)
            rows = slice(FOX_W + h * HEAD_DIM, FOX_W + (h + 1) * HEAD_DIM)
            v_t = jnp.concatenate([kv_t_ref[j + d, rows, :] for d in range(n)], axis=1)
            acc_scr[h] = alpha[:, cols] * acc_scr[h] + _dot(v_t, p[:, cols])

    def body(jp, _):
        chunks(2 * jp, 2, False)
        return 0

    lax.fori_loop(0, i // 2, body, 0)

    @pl.when(i % 2 == 1)
    def _():
        chunks(i - 1, 2, True)

    @pl.when(i % 2 == 0)
    def _():
        chunks(i, 1, True)
    inv_l = 1.0 / jnp.maximum(l_scr[0:1, :], TINY)
    for h in range(FOX_HEADS):
        out_scr[h * HEAD_DIM:(h + 1) * HEAD_DIM, :] = acc_scr[h] * inv_l[:, h * tq:(h + 1) * tq]
    o_ref[...] = out_scr[...].T


def fox_attend_prompt(qf, kvf, kvf_t, cum, cum_q, batch, seq, tq=128):
    nq = seq // tq
    kern = functools.partial(_fox_prompt_kernel, tq=tq)
    return pl.pallas_call(
        kern,
        grid=(batch, nq),
        in_specs=[pl.BlockSpec((tq, FOX_W), lambda b, i: (b * nq + i, 0)),
                  pl.BlockSpec((seq, FOX_ROW), lambda b, i: (b, 0)),
                  pl.BlockSpec((None, nq, FOX_ROW, tq), lambda b, i: (b, 0, 0, 0)),
                  pl.BlockSpec((None, nq, 8, tq), lambda b, i: (b, 0, 0, 0)),
                  pl.BlockSpec((seq, 8), lambda b, i: (b, 0))],
        out_specs=pl.BlockSpec((tq, FOX_W), lambda b, i: (b * nq + i, 0)),
        out_shape=jax.ShapeDtypeStruct((batch * seq, FOX_W), F32),
        scratch_shapes=[pltpu.VMEM((FOX_HEADS * tq, FOX_W), BF16),
                        pltpu.VMEM((STAT_ROWS, FOX_HEADS * tq), F32),
                        pltpu.VMEM((STAT_ROWS, FOX_HEADS * tq), F32),
                        pltpu.VMEM((FOX_HEADS, HEAD_DIM, tq), F32),
                        pltpu.VMEM((FOX_W, tq), F32)],
        compiler_params=_params("parallel", "arbitrary"),
        name="fox_prompt",
    )(qf, kvf, kvf_t, jnp.swapaxes(cum.reshape(batch, 8, nq, tq), 1, 2), cum_q)


def _fox_decode_kernel(pt_ref, q_ref, kv_new_ref, lf_new_ref, *rest, n_pages, page, past, seqs):
    del pt_ref
    kv_refs = rest[:seqs * n_pages]
    lf_refs = rest[seqs * n_pages:2 * seqs * n_pages]
    o_ref = rest[2 * seqs * n_pages]
    for s in range(seqs):
        _fox_decode_one(q_ref.at[s], kv_new_ref.at[s], lf_new_ref.at[s], kv_refs[s * n_pages:(s + 1) * n_pages],
                        lf_refs[s * n_pages:(s + 1) * n_pages], o_ref.at[s], page=page, past=past)


def _fox_decode_one(q_ref, kv_new_ref, lf_new_ref, kv_refs, lf_refs, o_ref, *, page, past):
    n_pages = len(kv_refs)
    tq = TQ_PAD
    rows = FOX_HEADS * tq
    row = lax.broadcasted_iota(jnp.int32, (rows, 1), 0)
    lane = lax.broadcasted_iota(jnp.int32, (rows, FOX_W), 1)
    own_head = (lane // HEAD_DIM) == (row // tq)
    t_rows = past + row % tq

    q = q_ref[...] * SCALE
    qbd = jnp.where(own_head, jnp.concatenate([q] * FOX_HEADS, axis=0), 0.0).astype(BF16)

    def head_rows(a):
        return jnp.concatenate(
            [jnp.broadcast_to(a[h:h + 1, :], (tq, a.shape[1])) for h in range(FOX_HEADS)], axis=0)

    lf_past = jnp.concatenate([r[...] for r in lf_refs], axis=1)
    cum_past = _lane_cumsum(lf_past)
    cum_new = cum_past[:, past - 1:past] + _lane_cumsum(lf_new_ref[...])
    ck_past = head_rows(cum_past)
    ck_new = head_rows(cum_new)
    new_lane = lax.broadcasted_iota(jnp.int32, (rows, page), 1)
    cq = jnp.sum(jnp.where(new_lane == row % tq, ck_new, 0.0), axis=-1, keepdims=True)

    kv_new = jnp.concatenate([kv_new_ref[...], jnp.zeros((page - tq, FOX_ROW), F32)], axis=0).astype(BF16)
    key_pos = lax.broadcasted_iota(jnp.int32, (1, past + page), 1)
    s = jnp.concatenate([_dot(qbd, kv_refs[p][:FOX_W, :].astype(BF16)) for p in range(n_pages)]
                        + [_dot_nt(qbd, kv_new[:, :FOX_W])], axis=1)
    s = s + (cq - jnp.concatenate([ck_past, ck_new], axis=1)) + jnp.where(key_pos <= t_rows, 0.0, NEG_INF)
    m = jnp.max(s, axis=-1, keepdims=True)
    pr = jnp.where(s > MASKED_BELOW, jnp.exp(s - m), 0.0)
    den = jnp.maximum(jnp.sum(pr, axis=-1, keepdims=True), TINY)
    pr = pr.astype(BF16)
    acc = _dot(pr[:, past:], kv_new[:, FOX_W:])
    for p in range(n_pages):
        acc = acc + _dot_nt(pr[:, p * page:(p + 1) * page], kv_refs[p][FOX_W:, :].astype(BF16))
    o = jnp.where(own_head, acc / den, 0.0)
    o_ref[...] = o[0:tq] + o[tq:2 * tq] + o[2 * tq:3 * tq] + o[3 * tq:4 * tq]


def fox_attend_decode(page_table, qf, kvf_new, lf_new_t, fox_pool, lf_pool_t, layer, past):
    n_seq, n_pages = page_table.shape
    page = fox_pool.shape[3]

    seqs = DECODE_SEQS if n_seq % DECODE_SEQS == 0 else 1

    def seq_spec(r, width):
        return pl.BlockSpec((seqs, r, width), lambda b, pt: (b, 0, 0))

    def page_spec(shape, s, p):
        return pl.BlockSpec((None, None) + shape, lambda b, pt: (layer, pt[b * seqs + s, p], 0, 0))

    assert TQ_PAD <= page
    kern = functools.partial(_fox_decode_kernel, n_pages=n_pages, page=page, past=past, seqs=seqs)
    grid_spec = pltpu.PrefetchScalarGridSpec(
        num_scalar_prefetch=1,
        grid=(n_seq // seqs,),
        in_specs=([seq_spec(TQ_PAD, FOX_W), seq_spec(TQ_PAD, FOX_ROW), seq_spec(8, page)]
                  + [page_spec((FOX_ROW, page), s, p) for s in range(seqs) for p in range(n_pages)]
                  + [page_spec((8, page), s, p) for s in range(seqs) for p in range(n_pages)]),
        out_specs=seq_spec(TQ_PAD, FOX_W),
    )
    n_refs = seqs * n_pages
    return pl.pallas_call(
        kern,
        grid_spec=grid_spec,
        out_shape=jax.ShapeDtypeStruct((n_seq, TQ_PAD, FOX_W), F32),
        compiler_params=_params("arbitrary"),
        name="fox_decode",
    )(page_table, qf, kvf_new, lf_new_t, *([fox_pool] * n_refs), *([lf_pool_t] * n_refs))


CONV_PAD = 32
CONV_SUB = 64


def _ln_silu(y, g, b):
    yc = y - jnp.mean(y, axis=-1, keepdims=True)
    var = jnp.mean(yc * yc, axis=-1, keepdims=True)
    return _silu(yc * lax.rsqrt(var + LN_EPS) * g + b)


def _conv_prompt_kernel(prev_ref, cur_ref, w_ref, b_ref, g_ref, bb_ref, o_ref, ctx_scr, shift_scr, *, tc):
    k = pl.program_id(1)
    ctx_scr[0:CONV_PAD, :] = jnp.where(k > 0, prev_ref[...], 0.0)
    ctx_scr[CONV_PAD:, :] = cur_ref[...]
    lead = CONV_PAD - (CONV_WIDTH - 1)
    n_fill = tc + CONV_PAD - SUBLANES
    for r in range(1, SUBLANES):
        shift_scr[r - 1, 0:n_fill, :] = ctx_scr[pl.ds(r, n_fill), :]
    for sub in range(tc // CONV_SUB):
        acc = jnp.zeros((CONV_SUB, CONV_CH), F32) + b_ref[...]
        for w in range(CONV_WIDTH):
            q, r = divmod(lead + w, SUBLANES)
            src = ctx_scr if r == 0 else shift_scr.at[r - 1]
            acc = acc + src[pl.ds(sub * CONV_SUB + q * SUBLANES, CONV_SUB), :] * w_ref[w:w + 1, :]
        o_ref[sub * CONV_SUB:(sub + 1) * CONV_SUB, :] = _ln_silu(acc, g_ref[...], bb_ref[...])


def conv_module_prompt(glu, dw_w, dw_b, ln_g, ln_b, layer, batch, seq, tc=256):
    tc = min(tc, seq)
    nt = seq // tc
    per = tc // CONV_PAD

    def vec_spec():
        return pl.BlockSpec((None, 1, CONV_CH), lambda b, k: (layer, 0, 0))

    return pl.pallas_call(
        functools.partial(_conv_prompt_kernel, tc=tc),
        grid=(batch, nt),
        in_specs=[pl.BlockSpec((CONV_PAD, CONV_CH), lambda b, k: (jnp.maximum((b * nt + k) * per - 1, 0), 0)),
                  pl.BlockSpec((tc, CONV_CH), lambda b, k: (b * nt + k, 0)),
                  pl.BlockSpec((None, CONV_WIDTH, CONV_CH), lambda b, k: (layer, 0, 0)),
                  vec_spec(), vec_spec(), vec_spec()],
        out_specs=pl.BlockSpec((tc, CONV_CH), lambda b, k: (b * nt + k, 0)),
        out_shape=jax.ShapeDtypeStruct((batch * seq, CONV_CH), F32),
        scratch_shapes=[pltpu.VMEM((CONV_PAD + tc, CONV_CH), F32),
                        pltpu.VMEM((SUBLANES - 1, CONV_PAD + tc, CONV_CH), F32)],
        compiler_params=_params("parallel", "arbitrary"),
        name="conv_prompt",
    )(glu, glu, dw_w, dw_b, ln_g, ln_b)


def _conv_decode_kernel(state_ref, glu_ref, w_ref, b_ref, g_ref, bb_ref, o_ref):
    n_state = state_ref.shape[0]
    t_new = glu_ref.shape[0]
    ctx = [state_ref[i] for i in range(n_state)] + [glu_ref[i] for i in range(t_new)]
    for i in range(t_new):
        acc = jnp.zeros(ctx[0].shape, F32) + b_ref[...]
        for w in range(CONV_WIDTH):
            acc = acc + ctx[i + w] * w_ref[w:w + 1, :]
        o_ref[i] = _ln_silu(acc, g_ref[...], bb_ref[...])


def conv_module_decode(state_t, glu_t, dw_w, dw_b, ln_g, ln_b, layer):
    t_new, n_seq, _ = glu_t.shape

    def vec_spec():
        return pl.BlockSpec((None, 1, CONV_CH), lambda i: (layer, 0, 0))

    return pl.pallas_call(
        _conv_decode_kernel,
        grid=(1,),
        in_specs=[pl.BlockSpec(state_t.shape, lambda i: (0, 0, 0)),
                  pl.BlockSpec(glu_t.shape, lambda i: (0, 0, 0)),
                  pl.BlockSpec((None, CONV_WIDTH, CONV_CH), lambda i: (layer, 0, 0)),
                  vec_spec(), vec_spec(), vec_spec()],
        out_specs=pl.BlockSpec((t_new, n_seq, CONV_CH), lambda i: (0, 0, 0)),
        out_shape=jax.ShapeDtypeStruct((t_new, n_seq, CONV_CH), F32),
        compiler_params=_params("arbitrary"),
        name="conv_decode",
    )(state_t, glu_t, dw_w, dw_b, ln_g, ln_b)


def _merge_kernel(x_ref, g_ref, sh_ref, sc_ref, ga_ref, on_ref, yc_ref, of_ref,
                  wg_ref, bg_ref, wn_ref, wc_ref, wf_ref, wo_ref, o_ref):
    x = x_ref[...]
    u = _rms_mod(x, g_ref[...], sh_ref[...], sc_ref[...]).astype(BF16)

    def gate(k):
        return _sigmoid(_dot(u, wg_ref[:, k * D_MODEL:(k + 1) * D_MODEL]) + bg_ref[:, k * D_MODEL:(k + 1) * D_MODEL])

    merged = (gate(0) * _dot(on_ref[...].astype(BF16), wn_ref[...])
              + gate(1) * _dot(yc_ref[...].astype(BF16), wc_ref[...])
              + gate(2) * _dot(of_ref[...].astype(BF16), wf_ref[...]))
    o_ref[...] = x + ga_ref[...] * _dot(merged.astype(BF16), wo_ref[...])


def merge_and_project(x, mod, norm_g, o_nsa, y_conv, o_fox, w_mg, b_mg, wb_nsa, wb_conv, wb_fox, w_out, layer, tm):
    m = x.shape[0]

    def row_spec(width):
        return pl.BlockSpec((tm, width), lambda i: (i, 0))

    def w_spec(rows, cols=D_MODEL):
        return pl.BlockSpec((None, rows, cols), lambda i: (layer, 0, 0))

    return pl.pallas_call(
        _merge_kernel,
        grid=(m // tm,),
        in_specs=[row_spec(D_MODEL), pl.BlockSpec((None, None, 1, D_MODEL), lambda i: (layer, 1, 0, 0)),
                  mod.spec(3, tm), mod.spec(4, tm), mod.spec(5, tm),
                  row_spec(NSA_Q), row_spec(CONV_CH), row_spec(FOX_W),
                  w_spec(D_MODEL, 3 * D_MODEL), w_spec(1, 3 * D_MODEL),
                  w_spec(NSA_Q), w_spec(CONV_CH), w_spec(FOX_W), w_spec(D_MODEL)],
        out_specs=row_spec(D_MODEL),
        out_shape=jax.ShapeDtypeStruct((m, D_MODEL), F32),
        compiler_params=_params("parallel"),
        name="merge",
    )(x, norm_g, mod.arr, mod.arr, mod.arr, o_nsa, y_conv, o_fox, w_mg, b_mg, wb_nsa, wb_conv, wb_fox, w_out)


def _final_norm_kernel(x_ref, g_ref, o_ref):
    x = x_ref[...]
    o_ref[...] = x * lax.rsqrt(jnp.mean(x * x, axis=-1, keepdims=True) + RMS_EPS) * g_ref[...]


def final_norm(x, g, tm):
    m = x.shape[0]
    return pl.pallas_call(
        _final_norm_kernel,
        grid=(m // tm,),
        in_specs=[pl.BlockSpec((tm, D_MODEL), lambda i: (i, 0)), pl.BlockSpec((1, D_MODEL), lambda i: (0, 0))],
        out_specs=pl.BlockSpec((tm, D_MODEL), lambda i: (i, 0)),
        out_shape=jax.ShapeDtypeStruct((m, D_MODEL), F32),
        compiler_params=_params("parallel"),
        name="final_norm",
    )(x, g.reshape(1, D_MODEL))


def _rope_tables(pos):
    half = HEAD_DIM // 2
    inv_freq = ROPE_THETA ** (-jnp.arange(half, dtype=F32) / half)
    ang = pos.astype(F32)[:, None] * inv_freq[None, :]
    cos, sin = jnp.cos(ang), jnp.sin(ang)
    cos = jnp.concatenate([cos, cos] * (LANE // HEAD_DIM), axis=1)
    sin = jnp.concatenate([-sin, sin] * (LANE // HEAD_DIM), axis=1)
    return cos, sin


def _aligned_in_proj(w_in, b_in):
    o_gn = NSA_Q + 3 * KV_ROW
    o_glu = o_gn + 3 * NSA_HEADS
    o_qkvf = o_glu + 2 * CONV_CH
    o_lf = o_qkvf + 3 * FOX_W
    o_mg = o_lf + FOX_HEADS

    def cols(a):
        pad_gn = jnp.zeros(a.shape[:-1] + (LANE - 3 * NSA_HEADS,), a.dtype)
        pad_lf = jnp.zeros(a.shape[:-1] + (LANE - FOX_HEADS,), a.dtype)
        return jnp.concatenate([a[..., :o_gn], a[..., o_glu:o_lf],
                                a[..., o_gn:o_glu], pad_gn, a[..., o_lf:o_mg], pad_lf], axis=-1)

    return (cols(w_in).astype(BF16), cols(b_in)[:, None, :],
            w_in[..., o_mg:].astype(BF16), b_in[:, None, o_mg:])


def _compress_weights(pe, wk, wv):
    pe2 = jnp.concatenate([pe, pe], axis=-1)

    def block_diag(w):
        z = jnp.zeros_like(w)
        return jnp.concatenate([jnp.concatenate([w, z], axis=-1), jnp.concatenate([z, w], axis=-1)],
                               axis=-2).astype(BF16)

    return pe2, block_diag(wk), block_diag(wv)


def _pad_tokens(a, n_seq, t_new):
    a = a.reshape(n_seq, t_new, a.shape[-1])
    return jnp.pad(a, ((0, 0), (0, TQ_PAD - t_new), (0, 0)))


def kernel(x_prompt, x_sample, cache_nsa_cmp_kv, cache_nsa_slc_kv, cache_fox_kv, cache_fox_logf,
           state_nsa_win_kv, state_conv, page_table, c_prompt, c_sample, w_ada, b_ada, norm_g,
           w_ffn_gu, w_ffn_down, w_in, b_in, nsa_cmp_pe, nsa_cmp_wk, nsa_cmp_wv, conv_dw_w, conv_dw_b,
           conv_ln_g, conv_ln_b, w_branch_nsa, w_branch_conv, w_branch_fox, w_out, final_norm_g):
    batch, seq, _ = x_prompt.shape
    n_seq, t_new, _ = x_sample.shape
    depth = w_ada.shape[0]
    n_pool, page = cache_nsa_cmp_kv.shape[1], cache_nsa_cmp_kv.shape[2]
    n_pages = page_table.shape[1]
    past = n_pages * page
    w_buf = state_nsa_win_kv.shape[2]
    mp, ms = batch * seq, n_seq * t_new
    tm_p = min(512, seq)
    tm_s = min(256, ms)

    w_ada_b = w_ada.astype(BF16)
    w_gu_b = w_ffn_gu.astype(BF16)
    w_down_b = w_ffn_down.astype(BF16)
    w_in_b, b_in_p, w_mg_b, b_mg = _aligned_in_proj(w_in, b_in)
    pe2, wk_bd, wv_bd = _compress_weights(nsa_cmp_pe, nsa_cmp_wk, nsa_cmp_wv)
    pe_t, wk_t, wv_t = _compress_weights(jnp.swapaxes(nsa_cmp_pe, 1, 2), jnp.swapaxes(nsa_cmp_wk, 1, 2),
                                         jnp.swapaxes(nsa_cmp_wv, 1, 2))
    wb_nsa_b, wb_conv_b = w_branch_nsa.astype(BF16), w_branch_conv.astype(BF16)
    wb_fox_b, w_out_b = w_branch_fox.astype(BF16), w_out.astype(BF16)
    norm_g4 = norm_g[:, :, None, :]
    dw_b3, ln_g3, ln_b3 = conv_dw_b[:, None, :], conv_ln_g[:, None, :], conv_ln_b[:, None, :]

    to_feature_major = (0, 1, 3, 4, 5, 2)
    cmp_pool = jnp.transpose(cache_nsa_cmp_kv, to_feature_major).reshape(depth, n_pool * KV_ROW, page)
    slc_pool = jnp.transpose(cache_nsa_slc_kv, to_feature_major).reshape(depth, n_pool, KV_ROW, page)
    fox_pool = jnp.transpose(cache_fox_kv, to_feature_major).reshape(depth, n_pool, FOX_ROW, page)
    lf_pool_t = jnp.pad(jnp.swapaxes(cache_fox_logf, 2, 3).astype(F32),
                        ((0, 0), (0, 0), (0, 8 - FOX_HEADS), (0, 0)))
    win_state = jnp.transpose(state_nsa_win_kv, to_feature_major).reshape(depth, n_seq, KV_ROW, w_buf)

    cos_p, sin_p = _rope_tables(jnp.arange(seq, dtype=jnp.int32))
    cos_s, sin_s = _rope_tables(past + jnp.arange(t_new, dtype=jnp.int32))
    cos_s, sin_s = jnp.tile(cos_s, (n_seq, 1)), jnp.tile(sin_s, (n_seq, 1))

    mod_all = ada_all_layers(jnp.concatenate([c_prompt, c_sample], axis=0), w_ada_b, b_ada)

    xp = x_prompt.reshape(mp, D_MODEL)
    xs = x_sample.reshape(ms, D_MODEL)
    out_p = [[] for _ in range(6)]
    out_s = [[] for _ in range(6)]

    for l in range(depth):
        mod_p = Mod(mod_all[l, :batch, None, :], seq)
        mod_s = Mod(jnp.repeat(mod_all[l, batch:], t_new, axis=0), 1)

        xp = ffn_half_step(xp, mod_p, 0, norm_g4, w_gu_b, w_down_b, l, 0, tm_p)
        xs = ffn_half_step(xs, mod_s, 0, norm_g4, w_gu_b, w_down_b, l, 0, tm_s)

        tm_in = min(512, seq)
        (q_p, kvc_p, kvs_p, kvw_p, glu_p, qf_p, kvf_p, gn_p, lf_p,
         kvc_pt, kvs_pt, kvw_pt, kvf_pt, lf_pt, kvs_pc, kvw_pc, kvf_pc) = input_projection(
            xp, mod_p, norm_g4, w_in_b, b_in_p, cos_p, sin_p, l, tm_in, seq // tm_in, feature_major_batch=batch)
        (q_s, kvc_s, kvs_s, kvw_s, glu_s, qf_s, kvf_s, gn_s, lf_s) = input_projection(
            xs, mod_s, norm_g4, w_in_b, b_in_p, cos_s, sin_s, l, tm_s, ms // tm_s)

        kcmp_p = nsa_compress_rows(kvc_p.reshape(mp * 2, NSA_KV), pe2, wk_bd, wv_bd, l, 128)
        o_nsa_p = nsa_attend_prompt(q_p, gn_p, kcmp_p, kvs_p, kvs_pc, kvw_p, kvw_pc, batch, seq)
        cum_p = cumsum_lanes(lf_pt)
        cum_q = jnp.swapaxes(cum_p, 1, 2).reshape(mp, 8)
        o_fox_p = fox_attend_prompt(qf_p, kvf_p, kvf_pc, cum_p, cum_q, batch, seq)
        y_conv_p = conv_module_prompt(glu_p, conv_dw_w, dw_b3, ln_g3, ln_b3, l, batch, seq)

        kcmp_pool = nsa_compress_pool(cmp_pool, pe_t, wk_t, wv_t, l, 128)
        o_nsa_s = nsa_attend_decode(page_table, _pad_tokens(q_s, n_seq, t_new), _pad_tokens(gn_s, n_seq, t_new),
                                    _pad_tokens(kvs_s, n_seq, t_new), _pad_tokens(kvw_s, n_seq, t_new),
                                    win_state, kcmp_pool, slc_pool, l, past)
        o_nsa_s = o_nsa_s[:, :t_new].reshape(ms, NSA_Q)
        lf_new_t = jnp.swapaxes(lf_s[:, :FOX_HEADS].reshape(n_seq, t_new, FOX_HEADS), 1, 2)
        lf_new_t = jnp.pad(lf_new_t, ((0, 0), (0, 8 - FOX_HEADS), (0, page - t_new)))
        o_fox_s = fox_attend_decode(page_table, _pad_tokens(qf_s, n_seq, t_new), _pad_tokens(kvf_s, n_seq, t_new),
                                    lf_new_t, fox_pool, lf_pool_t, l, past)
        o_fox_s = o_fox_s[:, :t_new].reshape(ms, FOX_W)
        state_t = jnp.swapaxes(state_conv[l], 0, 1)
        glu_t = jnp.swapaxes(glu_s.reshape(n_seq, t_new, CONV_CH), 0, 1)
        y_conv_s = conv_module_decode(state_t, glu_t, conv_dw_w, dw_b3, ln_g3, ln_b3, l)
        y_conv_s = jnp.swapaxes(y_conv_s, 0, 1).reshape(ms, CONV_CH)

        xp = merge_and_project(xp, mod_p, norm_g4, o_nsa_p, y_conv_p, o_fox_p, w_mg_b, b_mg,
                               wb_nsa_b, wb_conv_b, wb_fox_b, w_out_b, l, tm_p)
        xs = merge_and_project(xs, mod_s, norm_g4, o_nsa_s, y_conv_s, o_fox_s, w_mg_b, b_mg,
                               wb_nsa_b, wb_conv_b, wb_fox_b, w_out_b, l, tm_s)
        xp = ffn_half_step(xp, mod_p, 6, norm_g4, w_gu_b, w_down_b, l, 1, tm_p)
        xs = ffn_half_step(xs, mod_s, 6, norm_g4, w_gu_b, w_down_b, l, 1, tm_s)

        kv5 = (2, NSA_KV_HEADS, HEAD_DIM)
        glu_p3 = glu_p.reshape(batch, seq, CONV_CH)
        w_keep = min(NSA_WINDOW, seq)

        def token_major(a, heads):
            a = a.reshape(batch, 2, heads, HEAD_DIM, a.shape[-1])
            return jnp.transpose(a, (0, 4, 1, 2, 3))

        out_p[0].append(token_major(kvc_pt, NSA_KV_HEADS))
        out_p[1].append(token_major(kvs_pt, NSA_KV_HEADS))
        out_p[2].append(token_major(kvf_pt, FOX_HEADS))
        out_p[3].append(jnp.swapaxes(lf_pt[:, :FOX_HEADS], 1, 2))
        out_p[4].append(token_major(kvw_pt[:, :, seq - w_keep:], NSA_KV_HEADS))
        out_p[5].append(glu_p3[:, seq - (CONV_WIDTH - 1):])
        kvw_s5 = kvw_s.reshape((n_seq, t_new) + kv5)
        out_s[0].append(kvc_s.reshape((n_seq, t_new) + kv5))
        out_s[1].append(kvs_s.reshape((n_seq, t_new) + kv5))
        out_s[2].append(kvf_s.reshape(n_seq, t_new, 2, FOX_HEADS, HEAD_DIM))
        out_s[3].append(lf_s[:, :FOX_HEADS].reshape(n_seq, t_new, FOX_HEADS))
        out_s[4].append(kvw_s5)
        out_s[5].append(glu_s.reshape(n_seq, t_new, CONV_CH))

    y_p = final_norm(xp, final_norm_g, tm_p).reshape(batch, seq, D_MODEL)
    y_s = final_norm(xs, final_norm_g, tm_s).reshape(n_seq, t_new, D_MODEL)
    outs = [y_p, y_s]
    for f in range(6):
        outs.append(jnp.stack(out_p[f]))
        new_s = jnp.stack(out_s[f])
        if f == 4:
            new_s = jnp.concatenate([state_nsa_win_kv, new_s], axis=2)[:, :, t_new:]
        if f == 5:
            new_s = jnp.concatenate([state_conv, new_s], axis=2)
            new_s = new_s[:, :, new_s.shape[2] - (CONV_WIDTH - 1):]
        outs.append(new_s)
    return tuple(outs)
```

```python
import functools

import jax
import jax.numpy as jnp
from jax import lax
from jax.experimental import pallas as pl
from jax.experimental.pallas import tpu as pltpu

D_MODEL = 1024
HEAD_DIM = 64
NSA_HEADS = 8
NSA_KV_HEADS = 2
NSA_GROUP = NSA_HEADS // NSA_KV_HEADS
NSA_BLOCK = 64
NSA_TOP_K = 16
NSA_WINDOW = 512
FOX_HEADS = 4
CONV_CH = 256
CONV_WIDTH = 31
D_FF = 2816
ROPE_THETA = 10000.0
RMS_EPS = 1e-6
LN_EPS = 1e-5
NEG_INF = -1e30
POS_INF = 1e30
TINY = 1e-30
MASKED_BELOW = -5e29
SCALE = HEAD_DIM ** -0.5

NSA_Q = NSA_HEADS * HEAD_DIM
NSA_KV = NSA_KV_HEADS * HEAD_DIM
FOX_W = FOX_HEADS * HEAD_DIM
KV_ROW = 2 * NSA_KV
FOX_ROW = 2 * FOX_W

LANE = 128
SUBLANES = 8
C_Q = 0
C_KVC = C_Q + NSA_Q
C_KVS = C_KVC + KV_ROW
C_KVW = C_KVS + KV_ROW
C_GLU = C_KVW + KV_ROW
C_QF = C_GLU + 2 * CONV_CH
C_KVF = C_QF + FOX_W
C_GN = C_KVF + FOX_ROW
C_LF = C_GN + LANE
D_IN_PAD = C_LF + LANE

TQ_PAD = 8
DECODE_SEQS = 2
VMEM_LIMIT = 56 * 1024 * 1024

F32 = jnp.float32
BF16 = jnp.bfloat16
NT = (((1,), (1,)), ((), ()))


def _params(*sem):
    return pltpu.CompilerParams(dimension_semantics=sem, vmem_limit_bytes=VMEM_LIMIT)


def _dot(a, b):
    return jnp.dot(a, b, preferred_element_type=F32)


def _dot_nt(a, b):
    return lax.dot_general(a, b, NT, preferred_element_type=F32)


def _rms_mod(x, g, shift, scale):
    y = x * lax.rsqrt(jnp.mean(x * x, axis=-1, keepdims=True) + RMS_EPS) * g
    return y * (1.0 + scale) + shift


def _sigmoid(x):
    return 1.0 / (1.0 + jnp.exp(-x))


def _silu(x):
    return x * _sigmoid(x)


def _ada_kernel(c_ref, w_ref, b_ref, o_ref):
    c = _silu(c_ref[...]).astype(BF16)
    o_ref[...] = _dot(c, w_ref[...]) + b_ref[...]


def ada_all_layers(cond, w_ada, b_ada):
    depth, _, n = w_ada.shape
    rows = cond.shape[0]
    tn = n // 8
    return pl.pallas_call(
        _ada_kernel,
        grid=(depth, n // tn),
        in_specs=[pl.BlockSpec((rows, D_MODEL), lambda l, j: (0, 0)),
                  pl.BlockSpec((None, D_MODEL, tn), lambda l, j: (l, 0, j)),
                  pl.BlockSpec((None, 1, tn), lambda l, j: (l, 0, j))],
        out_specs=pl.BlockSpec((None, rows, tn), lambda l, j: (l, 0, j)),
        out_shape=jax.ShapeDtypeStruct((depth, rows, n), F32),
        compiler_params=_params("parallel", "parallel"),
        name="ada",
    )(cond, w_ada, b_ada.reshape(depth, 1, n))


class Mod:
    def __init__(self, arr, tokens_per_cond):
        self.arr = arr
        self.tokens_per_cond = tokens_per_cond

    def spec(self, k, tm):
        if self.arr.ndim == 3:
            per = self.tokens_per_cond // tm
            return pl.BlockSpec((None, 1, D_MODEL), lambda m, *_: (m // per, 0, k))
        return pl.BlockSpec((tm, D_MODEL), lambda m, *_: (m, k))


def _ffn_kernel(x_ref, g_ref, sh_ref, sc_ref, ga_ref, wg_ref, wu_ref, wd_ref, o_ref, h_scr, acc_scr):
    j = pl.program_id(1)

    @pl.when(j == 0)
    def _():
        h_scr[...] = _rms_mod(x_ref[...], g_ref[...], sh_ref[...], sc_ref[...]).astype(BF16)
        acc_scr[...] = jnp.zeros_like(acc_scr)

    h = h_scr[...]
    gate = _dot(h, wg_ref[...])
    up = _dot(h, wu_ref[...])
    a = (_silu(gate) * up).astype(BF16)
    acc_scr[...] += _dot(a, wd_ref[...])

    @pl.when(j == pl.num_programs(1) - 1)
    def _():
        o_ref[...] = x_ref[...] + 0.5 * ga_ref[...] * acc_scr[...]


def ffn_half_step(x, mod, k0, norm_g, w_gu, w_down, layer, idx, tm):
    m = x.shape[0]
    nf = 2
    tf = D_FF // nf
    return pl.pallas_call(
        _ffn_kernel,
        grid=(m // tm, nf),
        in_specs=[pl.BlockSpec((tm, D_MODEL), lambda i, j: (i, 0)),
                  pl.BlockSpec((None, None, 1, D_MODEL), lambda i, j: (layer, 2 * idx, 0, 0)),
                  mod.spec(k0, tm), mod.spec(k0 + 1, tm), mod.spec(k0 + 2, tm),
                  pl.BlockSpec((None, None, D_MODEL, tf), lambda i, j: (layer, idx, 0, j)),
                  pl.BlockSpec((None, None, D_MODEL, tf), lambda i, j: (layer, idx, 0, nf + j)),
                  pl.BlockSpec((None, None, tf, D_MODEL), lambda i, j: (layer, idx, j, 0))],
        out_specs=pl.BlockSpec((tm, D_MODEL), lambda i, j: (i, 0)),
        out_shape=jax.ShapeDtypeStruct((m, D_MODEL), F32),
        scratch_shapes=[pltpu.VMEM((tm, D_MODEL), BF16), pltpu.VMEM((tm, D_MODEL), F32)],
        compiler_params=_params("parallel", "arbitrary"),
        name="ffn",
    )(x, norm_g, mod.arr, mod.arr, mod.arr, w_gu, w_gu, w_down)


def _rope(z, cos, sin):
    n = z.shape[1]
    lane = lax.broadcasted_iota(jnp.int32, z.shape, 1)
    first_half = (lane % HEAD_DIM) < (HEAD_DIM // 2)
    partner = jnp.where(first_half, pltpu.roll(z, n - HEAD_DIM // 2, 1), pltpu.roll(z, HEAD_DIM // 2, 1))
    reps = n // LANE
    if reps > 1:
        cos = jnp.concatenate([cos] * reps, axis=1)
        sin = jnp.concatenate([sin] * reps, axis=1)
    return z * cos + partner * sin


def _inproj_kernel(x_ref, g_ref, sh_ref, sc_ref, cos_ref, sin_ref, w_ref, b_ref,
                   q_ref, kvc_ref, kvs_ref, kvw_ref, glu_ref, qf_ref, kvf_ref, gn_ref, lf_ref,
                   *feature_major_refs):
    u = _rms_mod(x_ref[...], g_ref[...], sh_ref[...], sc_ref[...]).astype(BF16)
    cos = cos_ref[...]
    sin = sin_ref[...]

    def proj(lo, hi):
        return _dot(u, w_ref[:, lo:hi]) + b_ref[:, lo:hi]

    def as_operand(ref, val, scale=None):
        if ref.dtype == BF16 and scale is not None:
            val = val * scale
        ref[...] = val.astype(ref.dtype)

    as_operand(q_ref, _rope(proj(C_Q, C_KVC), cos, sin), SCALE)
    kv_rows = []
    for ref, lo in ((kvc_ref, C_KVC), (kvs_ref, C_KVS), (kvw_ref, C_KVW)):
        z = proj(lo, lo + KV_ROW)
        z = jnp.concatenate([_rope(z[:, :NSA_KV], cos, sin), z[:, NSA_KV:]], axis=1)
        as_operand(ref, z)
        kv_rows.append(z)
    z = proj(C_GLU, C_QF)
    glu_ref[...] = z[:, :CONV_CH] * _sigmoid(z[:, CONV_CH:])
    as_operand(qf_ref, proj(C_QF, C_KVF), SCALE)
    kvf = proj(C_KVF, C_GN)
    as_operand(kvf_ref, kvf)
    gn_ref[...] = _sigmoid(proj(C_GN, C_LF))
    z = proj(C_LF, D_IN_PAD)
    lf = jnp.minimum(z, 0.0) - jnp.log(1.0 + jnp.exp(-jnp.abs(z)))
    lf_ref[...] = lf
    if feature_major_refs:
        kvc_t, kvs_t, kvw_t, kvf_t, lf_t, kvs_c, kvw_c, kvf_c = feature_major_refs
        for ref, rows, chunked in ((kvc_t, kv_rows[0], None), (kvs_t, kv_rows[1], kvs_c),
                                   (kvw_t, kv_rows[2], kvw_c), (kvf_t, kvf, kvf_c)):
            rows_t = rows.T
            ref[...] = rows_t
            if chunked is not None:
                for c in range(chunked.shape[0]):
                    chunked[c] = rows_t[:, c * LANE:(c + 1) * LANE].astype(BF16)
        lf_t[...] = lf.T[:lf_t.shape[0], :]


def input_projection(x, mod, norm_g, w_in, b_in, cos, sin, layer, tm, pos_tiles, feature_major_batch=None):
    m = x.shape[0]
    widths = (NSA_Q, KV_ROW, KV_ROW, KV_ROW, CONV_CH, FOX_W, FOX_ROW, LANE, LANE)
    out_specs = [pl.BlockSpec((tm, w), lambda i: (i, 0)) for w in widths]
    operand = (0, 2, 3, 5, 6) if feature_major_batch else ()
    out_shape = [jax.ShapeDtypeStruct((m, w), BF16 if k in operand else F32) for k, w in enumerate(widths)]
    if feature_major_batch:
        t = m // feature_major_batch
        for rows in (KV_ROW, KV_ROW, KV_ROW, FOX_ROW, 8):
            out_specs.append(pl.BlockSpec((None, rows, tm), lambda i: (i // pos_tiles, 0, i % pos_tiles)))
            out_shape.append(jax.ShapeDtypeStruct((feature_major_batch, rows, t), F32))
        for rows in (KV_ROW, KV_ROW, FOX_ROW):
            out_specs.append(pl.BlockSpec((None, tm // LANE, rows, LANE),
                                          lambda i: (i // pos_tiles, i % pos_tiles, 0, 0)))
            out_shape.append(jax.ShapeDtypeStruct((feature_major_batch, t // LANE, rows, LANE), BF16))
    return pl.pallas_call(
        _inproj_kernel,
        grid=(m // tm,),
        in_specs=[pl.BlockSpec((tm, D_MODEL), lambda i: (i, 0)),
                  pl.BlockSpec((None, None, 1, D_MODEL), lambda i: (layer, 1, 0, 0)),
                  mod.spec(3, tm), mod.spec(4, tm),
                  pl.BlockSpec((tm, LANE), lambda i: (i % pos_tiles, 0)),
                  pl.BlockSpec((tm, LANE), lambda i: (i % pos_tiles, 0)),
                  pl.BlockSpec((None, D_MODEL, D_IN_PAD), lambda i: (layer, 0, 0)),
                  pl.BlockSpec((None, 1, D_IN_PAD), lambda i: (layer, 0, 0))],
        out_specs=out_specs,
        out_shape=out_shape,
        compiler_params=_params("parallel"),
        name="inproj",
    )(x, norm_g, mod.arr, mod.arr, cos, sin, w_in, b_in)


def _compress_kernel(x_ref, pe_ref, wk_ref, wv_ref, o_ref):
    nblk = o_ref.shape[0]
    acc_k = jnp.zeros((nblk, NSA_KV), F32)
    acc_v = jnp.zeros((nblk, NSA_KV), F32)
    for l in range(NSA_BLOCK):
        pe = pe_ref[l:l + 1, :]
        k_rows = x_ref[pl.ds(2 * l, nblk, stride=2 * NSA_BLOCK), :] + pe
        v_rows = x_ref[pl.ds(2 * l + 1, nblk, stride=2 * NSA_BLOCK), :] + pe
        acc_k = acc_k + _dot(k_rows.astype(BF16), wk_ref[l])
        acc_v = acc_v + _dot(v_rows.astype(BF16), wv_ref[l])
    o_ref[:, :NSA_KV] = acc_k
    o_ref[:, NSA_KV:] = acc_v


def nsa_compress_rows(rows, pe2, wk_bd, wv_bd, layer, blocks_per_step):
    lead = rows.shape[:-2]
    nblk = rows.shape[-2] // (2 * NSA_BLOCK)
    nb = max(d for d in range(8, blocks_per_step + 1, 8) if nblk % d == 0)
    if lead:
        x_spec = pl.BlockSpec((None, nb * 2 * NSA_BLOCK, NSA_KV), lambda i: (layer, i, 0))
    else:
        x_spec = pl.BlockSpec((nb * 2 * NSA_BLOCK, NSA_KV), lambda i: (i, 0))
    w_spec = pl.BlockSpec((None, NSA_BLOCK, NSA_KV, NSA_KV), lambda i: (layer, 0, 0, 0))
    return pl.pallas_call(
        _compress_kernel,
        grid=(nblk // nb,),
        in_specs=[x_spec, pl.BlockSpec((None, NSA_BLOCK, NSA_KV), lambda i: (layer, 0, 0)), w_spec, w_spec],
        out_specs=pl.BlockSpec((nb, KV_ROW), lambda i: (i, 0)),
        out_shape=jax.ShapeDtypeStruct((nblk, KV_ROW), F32),
        compiler_params=_params("parallel"),
        name="nsa_compress",
    )(rows, pe2, wk_bd, wv_bd)


def _compress_pool_kernel(x_ref, pe_ref, wk_ref, wv_ref, o_ref, *, blocks_per_page):
    pages = o_ref.shape[0]
    page = x_ref.shape[1]
    for c in range(2 * NSA_KV_HEADS):
        w_ref = wk_ref if c < NSA_KV_HEADS else wv_ref
        acc = jnp.zeros((pages, page), F32)
        for d in range(HEAD_DIM):
            rows = x_ref[pl.ds(c * HEAD_DIM + d, pages, stride=KV_ROW), :] + pe_ref[d:d + 1, :]
            acc = acc + _dot(rows.astype(BF16), w_ref[d])
        for j in range(blocks_per_page):
            o_ref[:, j * KV_ROW + c * HEAD_DIM:j * KV_ROW + (c + 1) * HEAD_DIM] = acc[:, j * HEAD_DIM:(j + 1) * HEAD_DIM]


def nsa_compress_pool(pool_t, pe_t, wk_t, wv_t, layer, pages_per_step):
    page = pool_t.shape[-1]
    n_pool = pool_t.shape[1] // KV_ROW
    bpp = page // NSA_BLOCK
    assert page == LANE and bpp * HEAD_DIM == page
    pp = max(d for d in range(8, pages_per_step + 1, 8) if n_pool % d == 0)
    w_spec = pl.BlockSpec((None, HEAD_DIM, page, page), lambda i: (layer, 0, 0, 0))
    out = pl.pallas_call(
        functools.partial(_compress_pool_kernel, blocks_per_page=bpp),
        grid=(n_pool // pp,),
        in_specs=[pl.BlockSpec((None, pp * KV_ROW, page), lambda i: (layer, i, 0)),
                  pl.BlockSpec((None, HEAD_DIM, page), lambda i: (layer, 0, 0)), w_spec, w_spec],
        out_specs=pl.BlockSpec((pp, bpp * KV_ROW), lambda i: (i, 0)),
        out_shape=jax.ShapeDtypeStruct((n_pool, bpp * KV_ROW), F32),
        compiler_params=_params("parallel"),
        name="nsa_compress_pool",
    )(pool_t, pe_t, wk_t, wv_t)
    return out.reshape(n_pool, 1, bpp * KV_ROW)


def _softmax_rows(logits):
    m = jnp.max(logits, axis=-1, keepdims=True)
    p = jnp.where(logits > MASKED_BELOW, jnp.exp(logits - m), 0.0)
    return p / jnp.maximum(jnp.sum(p, axis=-1, keepdims=True), TINY)


STAT_ROWS = 8


def _flash_init(m_scr, l_scr, acc_scr):
    m_scr[...] = jnp.full(m_scr.shape, NEG_INF, F32)
    l_scr[...] = jnp.zeros(l_scr.shape, F32)
    acc_scr[...] = jnp.zeros(acc_scr.shape, F32)


def _flash_stats(s_t, m_scr, l_scr):
    m = m_scr[0:1, :]
    m_new = jnp.maximum(m, jnp.max(s_t, axis=0, keepdims=True))
    alpha = jnp.exp(m - m_new)
    m_safe = jnp.where(m_new > MASKED_BELOW, m_new, 0.0)
    ps, sums = [], []
    for c in range(0, s_t.shape[1], LANE):
        p = jnp.exp(s_t[:, c:c + LANE] - m_safe[:, c:c + LANE])
        sums.append(jnp.sum(p, axis=0, keepdims=True))
        ps.append(p.astype(BF16))
    l_scr[0:1, :] = alpha * l_scr[0:1, :] + jnp.concatenate(sums, axis=1)
    m_scr[0:1, :] = m_new
    return jnp.concatenate(ps, axis=1), alpha


def _select_blocks(score, t, nb):
    blk = lax.broadcasted_iota(jnp.int32, score.shape, 1)
    cur = t // NSA_BLOCK
    forced = jnp.where(blk == 0, 1, jnp.where(blk == cur, 1, jnp.where(blk == cur - 1, 1, 0)))
    s = jnp.where(forced == 1, POS_INF, jnp.where(blk > cur, NEG_INF, score))
    rank = jnp.zeros(score.shape, jnp.int32)
    for i in range(nb):
        col = s[:, i:i + 1]
        ahead = jnp.where(col > s, 1, jnp.where(col == s, jnp.where(blk > i, 1, 0), 0))
        rank = rank + ahead
    k_sel = min(NSA_TOP_K, nb)
    return jnp.where(rank < k_sel, jnp.where(blk < nb, 1.0, 0.0), 0.0)


def _expand_blocks(sel, k0, kc):
    width = sel.shape[1]
    kblk = (k0 + lax.broadcasted_iota(jnp.int32, (width, kc), 1)) // NSA_BLOCK
    e = jnp.where(kblk == lax.broadcasted_iota(jnp.int32, (width, kc), 0), 1.0, 0.0).astype(BF16)
    return _dot(sel.astype(BF16), e)


def _select_blocks_t(score_t, t_row, nb):
    blk = lax.broadcasted_iota(jnp.int32, score_t.shape, 0)
    cur = t_row // NSA_BLOCK
    forced = jnp.where(blk == 0, 1, jnp.where(blk == cur, 1, jnp.where(blk == cur - 1, 1, 0)))
    s = jnp.where(forced == 1, POS_INF, jnp.where(blk > cur, NEG_INF, score_t))
    rank = jnp.zeros(score_t.shape, jnp.int32)
    for i in range(nb):
        other = s[i:i + 1, :]
        rank = rank + jnp.where(other > s, 1, jnp.where(other == s, jnp.where(blk > i, 1, 0), 0))
    return jnp.where(rank < min(NSA_TOP_K, nb), 1.0, 0.0)


def _nsa_prompt_kernel(q_ref, gate_ref, kcmp_ref, kvs_ref, kvs_t_ref, kvw_ref, kvw_t_ref, o_ref,
                       qbd_scr, expand_scr, bias_scr, m_scr, l_scr, acc_scr, out_scr, *, tq):
    i = pl.program_id(1)
    t0 = i * tq
    t_row = t0 + lax.broadcasted_iota(jnp.int32, (1, tq), 1)
    nc = kcmp_ref.shape[0]
    lane = lax.broadcasted_iota(jnp.int32, (tq, LANE), 1)
    low_half = lane < HEAD_DIM

    pieces = []
    for h in range(NSA_HEADS):
        g = h // NSA_GROUP
        pair = q_ref[:, (h // 2) * LANE:(h // 2 + 1) * LANE].astype(F32)
        if h % 2 != g:
            pair = pltpu.roll(pair, HEAD_DIM, 1)
        pieces.append(jnp.where(low_half, pair, 0.0) if g == 0 else jnp.where(low_half, 0.0, pair))
    qbd_scr[...] = jnp.concatenate(pieces, axis=0).astype(BF16)
    qbd = qbd_scr[...]
    gates_t = gate_ref[...].T

    kcm = kcmp_ref[:, :NSA_KV].astype(BF16)
    vcm_t = jnp.concatenate([kcmp_ref[:, NSA_KV:], jnp.zeros((LANE - nc, NSA_KV), F32)], axis=0).T
    blk_t = lax.broadcasted_iota(jnp.int32, (nc, 1), 0)
    cmp_bias_t = jnp.where((blk_t + 1) * NSA_BLOCK - 1 <= t_row, 0.0, NEG_INF)
    lc_t = _dot_nt(kcm, qbd) + jnp.concatenate([cmp_bias_t] * NSA_HEADS, axis=1)
    m_t = jnp.max(lc_t, axis=0, keepdims=True)
    p_t = jnp.where(lc_t > MASKED_BELOW, jnp.exp(lc_t - m_t), 0.0)
    pc_t = p_t / jnp.maximum(jnp.sum(p_t, axis=0, keepdims=True), TINY)
    o_cmp_t = _dot(vcm_t[:, :nc].astype(BF16), pc_t.astype(BF16))
    for h in range(NSA_HEADS):
        g = h // NSA_GROUP
        out_scr[h * HEAD_DIM:(h + 1) * HEAD_DIM, :] = (
            gates_t[3 * h:3 * h + 1, :] * o_cmp_t[g * HEAD_DIM:(g + 1) * HEAD_DIM, h * tq:(h + 1) * tq])

    kc = LANE
    qk = tq // kc
    r_key = lax.broadcasted_iota(jnp.int32, (kc, tq), 0)
    c_qry = lax.broadcasted_iota(jnp.int32, (kc, tq), 1)
    sels = []
    for g in range(NSA_KV_HEADS):
        base = g * NSA_GROUP * tq
        score_t = (pc_t[:, base:base + tq] + pc_t[:, base + tq:base + 2 * tq]
                   + pc_t[:, base + 2 * tq:base + 3 * tq] + pc_t[:, base + 3 * tq:base + 4 * tq])
        sel_t = _select_blocks_t(score_t, t_row, nc)
        sels.append(jnp.concatenate([sel_t, jnp.zeros((LANE - nc, tq), F32)], axis=0))
    sel_t = jnp.concatenate(sels, axis=1).astype(BF16)

    n_chunks = bias_scr.shape[1]
    span = min(4, n_chunks)

    @pl.when(i == 0)
    def _():
        kblk = lax.broadcasted_iota(jnp.int32, expand_scr.shape, 0) // NSA_BLOCK
        expand_scr[...] = jnp.where(kblk == lax.broadcasted_iota(jnp.int32, expand_scr.shape, 1),
                                    1.0, 0.0).astype(BF16)

    pair = min(2, n_chunks)
    n_pairs = ((i + 1) * qk - 1) // pair + 1
    t_rows = jnp.concatenate([t_row] * NSA_KV_HEADS, axis=1)
    for k in range(n_chunks // span):
        @pl.when(n_pairs * pair > k * span)
        def _(k=k):
            first = k * span * kc
            ok = _dot(expand_scr[first:first + span * kc, :], sel_t)
            kpos = first + lax.broadcasted_iota(jnp.int32, (span * kc, 1), 0)
            bias = jnp.where(kpos <= t_rows, jnp.where(ok > 0.5, 0.0, NEG_INF), NEG_INF)
            for jj in range(span):
                for g in range(NSA_KV_HEADS):
                    bias_scr[g, k * span + jj] = bias[jj * kc:(jj + 1) * kc, g * tq:(g + 1) * tq]

    group_w = NSA_GROUP * tq

    def attend_chunk(keys, vals_t, bias_per_group):
        s_t = _dot_nt(keys, qbd_scr[...])
        if bias_per_group is not None:
            s_t = s_t + jnp.concatenate([b for b in bias_per_group for _ in range(NSA_GROUP)], axis=1)
        p, alpha = _flash_stats(s_t, m_scr, l_scr)
        for g in range(NSA_KV_HEADS):
            cols = slice(g * group_w, (g + 1) * group_w)
            rows = slice(NSA_KV + g * HEAD_DIM, NSA_KV + (g + 1) * HEAD_DIM)
            v_t = jnp.concatenate([v[rows, :] for v in vals_t], axis=1)
            acc_scr[g] = alpha[:, cols] * acc_scr[g] + _dot(v_t, p[:, cols])

    def finish_branch(branch):
        inv_l = 1.0 / jnp.maximum(l_scr[0:1, :], TINY)
        for h in range(NSA_HEADS):
            g, r = divmod(h, NSA_GROUP)
            rs = slice(h * HEAD_DIM, (h + 1) * HEAD_DIM)
            o_t = acc_scr[g, :, r * tq:(r + 1) * tq] * inv_l[:, h * tq:(h + 1) * tq]
            out_scr[rs, :] = out_scr[rs, :] + gates_t[3 * h + branch:3 * h + branch + 1, :] * o_t

    _flash_init(m_scr, l_scr, acc_scr)

    def slc_body(jp, _):
        j = jp * pair
        k0 = pl.multiple_of(j * kc, pair * kc)
        attend_chunk(kvs_ref[pl.ds(k0, pair * kc), :NSA_KV], [kvs_t_ref.at[j + d] for d in range(pair)],
                     [jnp.concatenate([bias_scr[g, j + d] for d in range(pair)], axis=0)
                      for g in range(NSA_KV_HEADS)])
        return 0

    lax.fori_loop(0, n_pairs, slc_body, 0)
    finish_branch(1)

    _flash_init(m_scr, l_scr, acc_scr)
    n_win = NSA_WINDOW // kc + qk

    def win_pair(e_first, j_first):
        k0 = pl.multiple_of(j_first * kc, kc)
        biases, masked = [], False
        for e in range(e_first, e_first + pair):
            rel = e * kc - NSA_WINDOW + r_key - c_qry
            all_visible = e * kc >= tq and (e + 1) * kc <= NSA_WINDOW + 1
            masked = masked or not all_visible
            biases.append(jnp.where(rel <= 0, jnp.where(rel > -NSA_WINDOW, 0.0, NEG_INF), NEG_INF))
        attend_chunk(kvw_ref[pl.ds(k0, pair * kc), :NSA_KV], [kvw_t_ref.at[j_first + d] for d in range(pair)],
                     [jnp.concatenate(biases, axis=0)] * NSA_KV_HEADS if masked else None)

    for e in range(0, n_win, pair):
        jc = i * qk - NSA_WINDOW // kc + e
        pl.when(jc >= 0)(functools.partial(win_pair, e, jc))

    finish_branch(2)
    o_ref[...] = out_scr[...].T


def nsa_attend_prompt(q, gates, kcmp, kvs, kvs_t, kvw, kvw_t, batch, seq, tq=256):
    tq = min(tq, seq)
    nq = seq // tq
    nc = seq // NSA_BLOCK
    n_tiles = seq // LANE
    assert nc <= LANE and tq % (2 * LANE) == 0 and NSA_WINDOW % (2 * LANE) == 0 and n_tiles % min(4, n_tiles) == 0
    kern = functools.partial(_nsa_prompt_kernel, tq=tq)
    return pl.pallas_call(
        kern,
        grid=(batch, nq),
        in_specs=[pl.BlockSpec((tq, NSA_Q), lambda b, i: (b * nq + i, 0)),
                  pl.BlockSpec((tq, LANE), lambda b, i: (b * nq + i, 0)),
                  pl.BlockSpec((nc, KV_ROW), lambda b, i: (b, 0)),
                  pl.BlockSpec((seq, KV_ROW), lambda b, i: (b, 0)),
                  pl.BlockSpec((None, n_tiles, KV_ROW, LANE), lambda b, i: (b, 0, 0, 0)),
                  pl.BlockSpec((seq, KV_ROW), lambda b, i: (b, 0)),
                  pl.BlockSpec((None, n_tiles, KV_ROW, LANE), lambda b, i: (b, 0, 0, 0))],
        out_specs=pl.BlockSpec((tq, NSA_Q), lambda b, i: (b * nq + i, 0)),
        out_shape=jax.ShapeDtypeStruct((batch * seq, NSA_Q), F32),
        scratch_shapes=[pltpu.VMEM((NSA_HEADS * tq, LANE), BF16),
                        pltpu.VMEM((seq, LANE), BF16),
                        pltpu.VMEM((NSA_KV_HEADS, n_tiles, LANE, tq), F32),
                        pltpu.VMEM((STAT_ROWS, NSA_HEADS * tq), F32),
                        pltpu.VMEM((STAT_ROWS, NSA_HEADS * tq), F32),
                        pltpu.VMEM((NSA_KV_HEADS, HEAD_DIM, NSA_GROUP * tq), F32),
                        pltpu.VMEM((NSA_Q, tq), F32)],
        compiler_params=_params("parallel", "arbitrary"),
        name="nsa_prompt",
    )(q, gates, kcmp, kvs, kvs_t, kvw, kvw_t)


def _nsa_decode_kernel(pt_ref, q_ref, gate_ref, kvs_new_ref, kvw_new_ref, win_ref, *rest,
                       n_pages, page, past, seqs):
    del pt_ref
    kcmp_refs = rest[:seqs * n_pages]
    kvs_refs = rest[seqs * n_pages:2 * seqs * n_pages]
    o_ref, kcmp_scr = rest[2 * seqs * n_pages:]
    for s in range(seqs):
        _nsa_decode_one(q_ref.at[s], gate_ref.at[s], kvs_new_ref.at[s], kvw_new_ref.at[s], win_ref.at[s],
                        kcmp_refs[s * n_pages:(s + 1) * n_pages], kvs_refs[s * n_pages:(s + 1) * n_pages],
                        o_ref.at[s], kcmp_scr.at[s], n_pages=n_pages, page=page, past=past)


def _nsa_decode_one(q_ref, gate_ref, kvs_new_ref, kvw_new_ref, win_ref, kcmp_refs, kvs_refs, o_ref, kcmp_scr,
                    *, n_pages, page, past):
    tq = TQ_PAD
    rows = NSA_HEADS * tq
    blocks_per_page = page // NSA_BLOCK
    nc = n_pages * blocks_per_page
    nb = nc + 1
    for p in range(n_pages):
        for j in range(blocks_per_page):
            kcmp_scr[p * blocks_per_page + j:p * blocks_per_page + j + 1, :] = (
                kcmp_refs[p][:, j * KV_ROW:(j + 1) * KV_ROW])

    row = lax.broadcasted_iota(jnp.int32, (rows, 1), 0)
    t_rows = past + row % tq
    t1 = past + lax.broadcasted_iota(jnp.int32, (tq, 1), 0)
    lane = lax.broadcasted_iota(jnp.int32, (rows, LANE), 1)
    own_group = (lane // HEAD_DIM) == (row // (NSA_GROUP * tq))

    q = q_ref[...] * SCALE
    pieces = []
    for h in range(NSA_HEADS):
        qh = q[:, h * HEAD_DIM:(h + 1) * HEAD_DIM]
        zero = jnp.zeros_like(qh)
        pieces.append(jnp.concatenate([qh, zero] if h < NSA_GROUP else [zero, qh], axis=1))
    qbd = jnp.concatenate(pieces, axis=0).astype(BF16)

    def own_half(o):
        o = jnp.where(own_group, o, 0.0)
        return o[:, :HEAD_DIM] + o[:, HEAD_DIM:]

    kcm = kcmp_scr[:, :NSA_KV].astype(BF16)
    vcm = kcmp_scr[:, NSA_KV:].astype(BF16)
    blk = lax.broadcasted_iota(jnp.int32, (1, nc), 1)
    lc = _dot_nt(qbd, kcm) + jnp.where((blk + 1) * NSA_BLOCK - 1 <= t_rows, 0.0, NEG_INF)
    pc = _softmax_rows(lc)
    o_cmp = own_half(_dot(pc.astype(BF16), vcm))

    gates = gate_ref[...]
    new_k_pos = past + lax.broadcasted_iota(jnp.int32, (1, page), 1)
    pad = jnp.zeros((page - tq, KV_ROW), F32)
    kvs_new = jnp.concatenate([kvs_new_ref[...], pad], axis=0).astype(BF16)
    kvw_new = jnp.concatenate([kvw_new_ref[...], pad], axis=0).astype(BF16)

    n_keys = (n_pages + 1) * page
    key_pos = lax.broadcasted_iota(jnp.int32, (1, n_keys), 1)
    sels = []
    for g in range(NSA_KV_HEADS):
        pcg = pc[g * NSA_GROUP * tq:(g + 1) * NSA_GROUP * tq]
        score = pcg[0:tq] + pcg[tq:2 * tq] + pcg[2 * tq:3 * tq] + pcg[3 * tq:4 * tq]
        score = jnp.concatenate([score, jnp.zeros((tq, LANE - nc), F32)], axis=1)
        sels.append(_select_blocks(score, t1, nb))
    ok = _expand_blocks(jnp.concatenate(sels, axis=0), 0, n_keys)
    t2 = jnp.concatenate([t1] * NSA_KV_HEADS, axis=0)
    bias2 = jnp.where(key_pos <= t2, jnp.where(ok > 0.5, 0.0, NEG_INF), NEG_INF)
    bias = jnp.concatenate(
        [bias2[g * tq:(g + 1) * tq] for g in range(NSA_KV_HEADS) for _ in range(NSA_GROUP)], axis=0)
    s = jnp.concatenate([_dot(qbd, kvs_refs[p][:NSA_KV, :].astype(BF16)) for p in range(n_pages)]
                        + [_dot_nt(qbd, kvs_new[:, :NSA_KV])], axis=1) + bias
    m = jnp.max(s, axis=-1, keepdims=True)
    pr = jnp.where(s > MASKED_BELOW, jnp.exp(s - m), 0.0)
    den = jnp.maximum(jnp.sum(pr, axis=-1, keepdims=True), TINY)
    pr = pr.astype(BF16)
    acc = _dot(pr[:, n_pages * page:], kvs_new[:, NSA_KV:])
    for p in range(n_pages):
        acc = acc + _dot_nt(pr[:, p * page:(p + 1) * page], kvs_refs[p][NSA_KV:, :].astype(BF16))
    o_slc = own_half(acc / den)

    w_buf = win_ref.shape[1]
    wpos = jnp.concatenate([past - w_buf + lax.broadcasted_iota(jnp.int32, (1, w_buf), 1), new_k_pos], axis=1)
    ok = jnp.where(wpos <= t_rows, jnp.where(wpos > t_rows - NSA_WINDOW, jnp.where(wpos >= 0, 1.0, 0.0), 0.0), 0.0)
    s = jnp.concatenate([_dot(qbd, win_ref[:NSA_KV, :].astype(BF16)), _dot_nt(qbd, kvw_new[:, :NSA_KV])], axis=1)
    s = s + jnp.where(ok > 0.5, 0.0, NEG_INF)
    m = jnp.max(s, axis=-1, keepdims=True)
    pr = jnp.where(s > MASKED_BELOW, jnp.exp(s - m), 0.0)
    den = jnp.maximum(jnp.sum(pr, axis=-1, keepdims=True), TINY)
    pr = pr.astype(BF16)
    acc = _dot_nt(pr[:, :w_buf], win_ref[NSA_KV:, :].astype(BF16)) + _dot(pr[:, w_buf:], kvw_new[:, NSA_KV:])
    o_win = own_half(acc / den)

    for h in range(NSA_HEADS):
        rs = slice(h * tq, (h + 1) * tq)
        o = (gates[:, 3 * h:3 * h + 1] * o_cmp[rs] + gates[:, 3 * h + 1:3 * h + 2] * o_slc[rs]
             + gates[:, 3 * h + 2:3 * h + 3] * o_win[rs])
        o_ref[:, h * HEAD_DIM:(h + 1) * HEAD_DIM] = o


def nsa_attend_decode(page_table, q, gates, kvs_new, kvw_new, win_state, kcmp_pool, slc_pool, layer, past):
    n_seq, n_pages = page_table.shape
    page = slc_pool.shape[3]
    bpp = page // NSA_BLOCK
    w_buf = win_state.shape[3]

    seqs = DECODE_SEQS if n_seq % DECODE_SEQS == 0 else 1

    def seq_spec(width):
        return pl.BlockSpec((seqs, TQ_PAD, width), lambda b, pt: (b, 0, 0))

    def kcmp_spec(s, p):
        return pl.BlockSpec((None, 1, bpp * KV_ROW), lambda b, pt: (pt[b * seqs + s, p], 0, 0))

    def page_spec(s, p):
        return pl.BlockSpec((None, None, KV_ROW, page), lambda b, pt: (layer, pt[b * seqs + s, p], 0, 0))

    assert page % NSA_BLOCK == 0 and TQ_PAD <= NSA_BLOCK and n_pages * bpp + 1 <= LANE
    kern = functools.partial(_nsa_decode_kernel, n_pages=n_pages, page=page, past=past, seqs=seqs)
    grid_spec = pltpu.PrefetchScalarGridSpec(
        num_scalar_prefetch=1,
        grid=(n_seq // seqs,),
        in_specs=([seq_spec(NSA_Q), seq_spec(LANE), seq_spec(KV_ROW), seq_spec(KV_ROW),
                   pl.BlockSpec((None, seqs, KV_ROW, w_buf), lambda b, pt: (layer, b, 0, 0))]
                  + [kcmp_spec(s, p) for s in range(seqs) for p in range(n_pages)]
                  + [page_spec(s, p) for s in range(seqs) for p in range(n_pages)]),
        out_specs=seq_spec(NSA_Q),
        scratch_shapes=[pltpu.VMEM((seqs, n_pages * bpp, KV_ROW), F32)],
    )
    n_refs = seqs * n_pages
    return pl.pallas_call(
        kern,
        grid_spec=grid_spec,
        out_shape=jax.ShapeDtypeStruct((n_seq, TQ_PAD, NSA_Q), F32),
        compiler_params=_params("arbitrary"),
        name="nsa_decode",
    )(page_table, q, gates, kvs_new, kvw_new, win_state, *([kcmp_pool] * n_refs), *([slc_pool] * n_refs))


def _lane_cumsum(x):
    n = x.shape[-1]
    lane = lax.broadcasted_iota(jnp.int32, x.shape, x.ndim - 1)
    s = 1
    while s < n:
        x = x + jnp.where(lane >= s, pltpu.roll(x, s, x.ndim - 1), 0.0)
        s *= 2
    return x


def _cumsum_kernel(x_ref, o_ref):
    o_ref[...] = _lane_cumsum(x_ref[...])


def cumsum_lanes(x):
    b, r, t = x.shape
    return pl.pallas_call(
        _cumsum_kernel,
        grid=(b,),
        in_specs=[pl.BlockSpec((None, r, t), lambda i: (i, 0, 0))],
        out_specs=pl.BlockSpec((None, r, t), lambda i: (i, 0, 0)),
        out_shape=jax.ShapeDtypeStruct(x.shape, F32),
        compiler_params=_params("parallel"),
        name="logf_cumsum",
    )(x)


def _fox_prompt_kernel(q_ref, kv_ref, kv_t_ref, cum_ref, cumq_ref, o_ref,
                       qbd_scr, m_scr, l_scr, acc_scr, out_scr, *, tq):
    i = pl.program_id(1)
    lane = lax.broadcasted_iota(jnp.int32, (tq, FOX_W), 1)
    q = q_ref[...].astype(F32)
    qbd_scr[...] = jnp.concatenate(
        [jnp.where(lane // HEAD_DIM == h, q, 0.0) for h in range(FOX_HEADS)], axis=0).astype(BF16)
    cum_q = cum_ref[i]
    cum_q = jnp.concatenate([cum_q[h:h + 1, :] for h in range(FOX_HEADS)], axis=1)
    r_key = lax.broadcasted_iota(jnp.int32, (tq, tq), 0)
    c_qry = lax.broadcasted_iota(jnp.int32, (tq, tq), 1)
    causal_bias = jnp.concatenate([jnp.where(r_key <= c_qry, 0.0, NEG_INF)] * FOX_HEADS, axis=1)
    _flash_init(m_scr, l_scr, acc_scr)

    per_chunk = tq // LANE

    def chunk(j, causal):
        k0 = pl.multiple_of(j * tq, tq)
        cum_k = cumq_ref[pl.ds(k0, tq), :]
        cum_k = jnp.concatenate(
            [jnp.broadcast_to(cum_k[:, h:h + 1], (tq, tq)) for h in range(FOX_HEADS)], axis=1)
        s_t = _dot_nt(kv_ref[pl.ds(k0, tq), :FOX_W], qbd_scr[...]) + (cum_q - cum_k)
        if causal:
            s_t = s_t + causal_bias
        p, alpha = _flash_stats(s_t, m_scr, l_scr)
        for h in range(FOX_HEADS):
            cols = slice(h * tq, (h + 1) * tq)
            rows = slice(FOX_W + h * HEAD_DIM, FOX_W + (h + 1) * HEAD_DIM)
            v_t = jnp.concatenate([kv_t_ref[j * per_chunk + d, rows, :] for d in range(per_chunk)], axis=1)
            acc_scr[h] = alpha[:, cols] * acc_scr[h] + _dot(v_t, p[:, cols])

    def body(j, _):
        chunk(j, False)
        return 0

    lax.fori_loop(0, i, body, 0)
    chunk(i, True)
    inv_l = 1.0 / jnp.maximum(l_scr[0:1, :], TINY)
    for h in range(FOX_HEADS):
        out_scr[h * HEAD_DIM:(h + 1) * HEAD_DIM, :] = acc_scr[h] * inv_l[:, h * tq:(h + 1) * tq]
    o_ref[...] = out_scr[...].T


def fox_attend_prompt(qf, kvf, kvf_t, cum, cum_q, batch, seq, tq=256):
    tq = min(tq, seq)
    nq = seq // tq
    assert tq % LANE == 0
    kern = functools.partial(_fox_prompt_kernel, tq=tq)
    return pl.pallas_call(
        kern,
        grid=(batch, nq),
        in_specs=[pl.BlockSpec((tq, FOX_W), lambda b, i: (b * nq + i, 0)),
                  pl.BlockSpec((seq, FOX_ROW), lambda b, i: (b, 0)),
                  pl.BlockSpec((None, seq // LANE, FOX_ROW, LANE), lambda b, i: (b, 0, 0, 0)),
                  pl.BlockSpec((None, nq, 8, tq), lambda b, i: (b, 0, 0, 0)),
                  pl.BlockSpec((seq, 8), lambda b, i: (b, 0))],
        out_specs=pl.BlockSpec((tq, FOX_W), lambda b, i: (b * nq + i, 0)),
        out_shape=jax.ShapeDtypeStruct((batch * seq, FOX_W), F32),
        scratch_shapes=[pltpu.VMEM((FOX_HEADS * tq, FOX_W), BF16),
                        pltpu.VMEM((STAT_ROWS, FOX_HEADS * tq), F32),
                        pltpu.VMEM((STAT_ROWS, FOX_HEADS * tq), F32),
                        pltpu.VMEM((FOX_HEADS, HEAD_DIM, tq), F32),
                        pltpu.VMEM((FOX_W, tq), F32)],
        compiler_params=_params("parallel", "arbitrary"),
        name="fox_prompt",
    )(qf, kvf, kvf_t, jnp.swapaxes(cum.reshape(batch, 8, nq, tq), 1, 2), cum_q)


def _fox_decode_kernel(pt_ref, q_ref, kv_new_ref, lf_new_ref, *rest, n_pages, page, past, seqs):
    del pt_ref
    kv_refs = rest[:seqs * n_pages]
    lf_refs = rest[seqs * n_pages:2 * seqs * n_pages]
    o_ref = rest[2 * seqs * n_pages]
    for s in range(seqs):
        _fox_decode_one(q_ref.at[s], kv_new_ref.at[s], lf_new_ref.at[s], kv_refs[s * n_pages:(s + 1) * n_pages],
                        lf_refs[s * n_pages:(s + 1) * n_pages], o_ref.at[s], page=page, past=past)


def _fox_decode_one(q_ref, kv_new_ref, lf_new_ref, kv_refs, lf_refs, o_ref, *, page, past):
    n_pages = len(kv_refs)
    tq = TQ_PAD
    rows = FOX_HEADS * tq
    row = lax.broadcasted_iota(jnp.int32, (rows, 1), 0)
    lane = lax.broadcasted_iota(jnp.int32, (rows, FOX_W), 1)
    own_head = (lane // HEAD_DIM) == (row // tq)
    t_rows = past + row % tq

    q = q_ref[...] * SCALE
    qbd = jnp.where(own_head, jnp.concatenate([q] * FOX_HEADS, axis=0), 0.0).astype(BF16)

    def head_rows(a):
        return jnp.concatenate(
            [jnp.broadcast_to(a[h:h + 1, :], (tq, a.shape[1])) for h in range(FOX_HEADS)], axis=0)

    lf_past = jnp.concatenate([r[...] for r in lf_refs], axis=1)
    cum_past = _lane_cumsum(lf_past)
    cum_new = cum_past[:, past - 1:past] + _lane_cumsum(lf_new_ref[...])
    ck_past = head_rows(cum_past)
    ck_new = head_rows(cum_new)
    new_lane = lax.broadcasted_iota(jnp.int32, (rows, page), 1)
    cq = jnp.sum(jnp.where(new_lane == row % tq, ck_new, 0.0), axis=-1, keepdims=True)

    kv_new = jnp.concatenate([kv_new_ref[...], jnp.zeros((page - tq, FOX_ROW), F32)], axis=0).astype(BF16)
    key_pos = lax.broadcasted_iota(jnp.int32, (1, past + page), 1)
    s = jnp.concatenate([_dot(qbd, kv_refs[p][:FOX_W, :].astype(BF16)) for p in range(n_pages)]
                        + [_dot_nt(qbd, kv_new[:, :FOX_W])], axis=1)
    s = s + (cq - jnp.concatenate([ck_past, ck_new], axis=1)) + jnp.where(key_pos <= t_rows, 0.0, NEG_INF)
    m = jnp.max(s, axis=-1, keepdims=True)
    pr = jnp.where(s > MASKED_BELOW, jnp.exp(s - m), 0.0)
    den = jnp.maximum(jnp.sum(pr, axis=-1, keepdims=True), TINY)
    pr = pr.astype(BF16)
    acc = _dot(pr[:, past:], kv_new[:, FOX_W:])
    for p in range(n_pages):
        acc = acc + _dot_nt(pr[:, p * page:(p + 1) * page], kv_refs[p][FOX_W:, :].astype(BF16))
    o = jnp.where(own_head, acc / den, 0.0)
    o_ref[...] = o[0:tq] + o[tq:2 * tq] + o[2 * tq:3 * tq] + o[3 * tq:4 * tq]


def fox_attend_decode(page_table, qf, kvf_new, lf_new_t, fox_pool, lf_pool_t, layer, past):
    n_seq, n_pages = page_table.shape
    page = fox_pool.shape[3]

    seqs = DECODE_SEQS if n_seq % DECODE_SEQS == 0 else 1

    def seq_spec(r, width):
        return pl.BlockSpec((seqs, r, width), lambda b, pt: (b, 0, 0))

    def page_spec(shape, s, p):
        return pl.BlockSpec((None, None) + shape, lambda b, pt: (layer, pt[b * seqs + s, p], 0, 0))

    assert TQ_PAD <= page
    kern = functools.partial(_fox_decode_kernel, n_pages=n_pages, page=page, past=past, seqs=seqs)
    grid_spec = pltpu.PrefetchScalarGridSpec(
        num_scalar_prefetch=1,
        grid=(n_seq // seqs,),
        in_specs=([seq_spec(TQ_PAD, FOX_W), seq_spec(TQ_PAD, FOX_ROW), seq_spec(8, page)]
                  + [page_spec((FOX_ROW, page), s, p) for s in range(seqs) for p in range(n_pages)]
                  + [page_spec((8, page), s, p) for s in range(seqs) for p in range(n_pages)]),
        out_specs=seq_spec(TQ_PAD, FOX_W),
    )
    n_refs = seqs * n_pages
    return pl.pallas_call(
        kern,
        grid_spec=grid_spec,
        out_shape=jax.ShapeDtypeStruct((n_seq, TQ_PAD, FOX_W), F32),
        compiler_params=_params("arbitrary"),
        name="fox_decode",
    )(page_table, qf, kvf_new, lf_new_t, *([fox_pool] * n_refs), *([lf_pool_t] * n_refs))


CONV_PAD = 32
CONV_SUB = 64


def _ln_silu(y, g, b):
    yc = y - jnp.mean(y, axis=-1, keepdims=True)
    var = jnp.mean(yc * yc, axis=-1, keepdims=True)
    return _silu(yc * lax.rsqrt(var + LN_EPS) * g + b)


def _conv_prompt_kernel(prev_ref, cur_ref, w_ref, b_ref, g_ref, bb_ref, o_ref, ctx_scr, shift_scr, *, tc):
    k = pl.program_id(1)
    ctx_scr[0:CONV_PAD, :] = jnp.where(k > 0, prev_ref[...], 0.0)
    ctx_scr[CONV_PAD:, :] = cur_ref[...]
    lead = CONV_PAD - (CONV_WIDTH - 1)
    n_fill = tc + CONV_PAD - SUBLANES
    for r in range(1, SUBLANES):
        shift_scr[r - 1, 0:n_fill, :] = ctx_scr[pl.ds(r, n_fill), :]
    for sub in range(tc // CONV_SUB):
        acc = jnp.zeros((CONV_SUB, CONV_CH), F32) + b_ref[...]
        for w in range(CONV_WIDTH):
            q, r = divmod(lead + w, SUBLANES)
            src = ctx_scr if r == 0 else shift_scr.at[r - 1]
            acc = acc + src[pl.ds(sub * CONV_SUB + q * SUBLANES, CONV_SUB), :] * w_ref[w:w + 1, :]
        o_ref[sub * CONV_SUB:(sub + 1) * CONV_SUB, :] = _ln_silu(acc, g_ref[...], bb_ref[...])


def conv_module_prompt(glu, dw_w, dw_b, ln_g, ln_b, layer, batch, seq, tc=256):
    tc = min(tc, seq)
    nt = seq // tc
    per = tc // CONV_PAD

    def vec_spec():
        return pl.BlockSpec((None, 1, CONV_CH), lambda b, k: (layer, 0, 0))

    return pl.pallas_call(
        functools.partial(_conv_prompt_kernel, tc=tc),
        grid=(batch, nt),
        in_specs=[pl.BlockSpec((CONV_PAD, CONV_CH), lambda b, k: (jnp.maximum((b * nt + k) * per - 1, 0), 0)),
                  pl.BlockSpec((tc, CONV_CH), lambda b, k: (b * nt + k, 0)),
                  pl.BlockSpec((None, CONV_WIDTH, CONV_CH), lambda b, k: (layer, 0, 0)),
                  vec_spec(), vec_spec(), vec_spec()],
        out_specs=pl.BlockSpec((tc, CONV_CH), lambda b, k: (b * nt + k, 0)),
        out_shape=jax.ShapeDtypeStruct((batch * seq, CONV_CH), F32),
        scratch_shapes=[pltpu.VMEM((CONV_PAD + tc, CONV_CH), F32),
                        pltpu.VMEM((SUBLANES - 1, CONV_PAD + tc, CONV_CH), F32)],
        compiler_params=_params("parallel", "arbitrary"),
        name="conv_prompt",
    )(glu, glu, dw_w, dw_b, ln_g, ln_b)


def _conv_decode_kernel(state_ref, glu_ref, w_ref, b_ref, g_ref, bb_ref, o_ref):
    n_state = state_ref.shape[0]
    t_new = glu_ref.shape[0]
    ctx = [state_ref[i] for i in range(n_state)] + [glu_ref[i] for i in range(t_new)]
    for i in range(t_new):
        acc = jnp.zeros(ctx[0].shape, F32) + b_ref[...]
        for w in range(CONV_WIDTH):
            acc = acc + ctx[i + w] * w_ref[w:w + 1, :]
        o_ref[i] = _ln_silu(acc, g_ref[...], bb_ref[...])


def conv_module_decode(state_t, glu_t, dw_w, dw_b, ln_g, ln_b, layer):
    t_new, n_seq, _ = glu_t.shape

    def vec_spec():
        return pl.BlockSpec((None, 1, CONV_CH), lambda i: (layer, 0, 0))

    return pl.pallas_call(
        _conv_decode_kernel,
        grid=(1,),
        in_specs=[pl.BlockSpec(state_t.shape, lambda i: (0, 0, 0)),
                  pl.BlockSpec(glu_t.shape, lambda i: (0, 0, 0)),
                  pl.BlockSpec((None, CONV_WIDTH, CONV_CH), lambda i: (layer, 0, 0)),
                  vec_spec(), vec_spec(), vec_spec()],
        out_specs=pl.BlockSpec((t_new, n_seq, CONV_CH), lambda i: (0, 0, 0)),
        out_shape=jax.ShapeDtypeStruct((t_new, n_seq, CONV_CH), F32),
        compiler_params=_params("arbitrary"),
        name="conv_decode",
    )(state_t, glu_t, dw_w, dw_b, ln_g, ln_b)


def _merge_kernel(x_ref, g_ref, sh_ref, sc_ref, ga_ref, on_ref, yc_ref, of_ref,
                  wg_ref, bg_ref, wn_ref, wc_ref, wf_ref, wo_ref, o_ref):
    x = x_ref[...]
    u = _rms_mod(x, g_ref[...], sh_ref[...], sc_ref[...]).astype(BF16)

    def gate(k):
        return _sigmoid(_dot(u, wg_ref[:, k * D_MODEL:(k + 1) * D_MODEL]) + bg_ref[:, k * D_MODEL:(k + 1) * D_MODEL])

    merged = (gate(0) * _dot(on_ref[...].astype(BF16), wn_ref[...])
              + gate(1) * _dot(yc_ref[...].astype(BF16), wc_ref[...])
              + gate(2) * _dot(of_ref[...].astype(BF16), wf_ref[...]))
    o_ref[...] = x + ga_ref[...] * _dot(merged.astype(BF16), wo_ref[...])


def merge_and_project(x, mod, norm_g, o_nsa, y_conv, o_fox, w_mg, b_mg, wb_nsa, wb_conv, wb_fox, w_out, layer, tm):
    m = x.shape[0]

    def row_spec(width):
        return pl.BlockSpec((tm, width), lambda i: (i, 0))

    def w_spec(rows, cols=D_MODEL):
        return pl.BlockSpec((None, rows, cols), lambda i: (layer, 0, 0))

    return pl.pallas_call(
        _merge_kernel,
        grid=(m // tm,),
        in_specs=[row_spec(D_MODEL), pl.BlockSpec((None, None, 1, D_MODEL), lambda i: (layer, 1, 0, 0)),
                  mod.spec(3, tm), mod.spec(4, tm), mod.spec(5, tm),
                  row_spec(NSA_Q), row_spec(CONV_CH), row_spec(FOX_W),
                  w_spec(D_MODEL, 3 * D_MODEL), w_spec(1, 3 * D_MODEL),
                  w_spec(NSA_Q), w_spec(CONV_CH), w_spec(FOX_W), w_spec(D_MODEL)],
        out_specs=row_spec(D_MODEL),
        out_shape=jax.ShapeDtypeStruct((m, D_MODEL), F32),
        compiler_params=_params("parallel"),
        name="merge",
    )(x, norm_g, mod.arr, mod.arr, mod.arr, o_nsa, y_conv, o_fox, w_mg, b_mg, wb_nsa, wb_conv, wb_fox, w_out)


def _final_norm_kernel(x_ref, g_ref, o_ref):
    x = x_ref[...]
    o_ref[...] = x * lax.rsqrt(jnp.mean(x * x, axis=-1, keepdims=True) + RMS_EPS) * g_ref[...]


def final_norm(x, g, tm):
    m = x.shape[0]
    return pl.pallas_call(
        _final_norm_kernel,
        grid=(m // tm,),
        in_specs=[pl.BlockSpec((tm, D_MODEL), lambda i: (i, 0)), pl.BlockSpec((1, D_MODEL), lambda i: (0, 0))],
        out_specs=pl.BlockSpec((tm, D_MODEL), lambda i: (i, 0)),
        out_shape=jax.ShapeDtypeStruct((m, D_MODEL), F32),
        compiler_params=_params("parallel"),
        name="final_norm",
    )(x, g.reshape(1, D_MODEL))


def _rope_tables(pos):
    half = HEAD_DIM // 2
    inv_freq = ROPE_THETA ** (-jnp.arange(half, dtype=F32) / half)
    ang = pos.astype(F32)[:, None] * inv_freq[None, :]
    cos, sin = jnp.cos(ang), jnp.sin(ang)
    cos = jnp.concatenate([cos, cos] * (LANE // HEAD_DIM), axis=1)
    sin = jnp.concatenate([-sin, sin] * (LANE // HEAD_DIM), axis=1)
    return cos, sin


def _aligned_in_proj(w_in, b_in):
    o_gn = NSA_Q + 3 * KV_ROW
    o_glu = o_gn + 3 * NSA_HEADS
    o_qkvf = o_glu + 2 * CONV_CH
    o_lf = o_qkvf + 3 * FOX_W
    o_mg = o_lf + FOX_HEADS

    def cols(a):
        pad_gn = jnp.zeros(a.shape[:-1] + (LANE - 3 * NSA_HEADS,), a.dtype)
        pad_lf = jnp.zeros(a.shape[:-1] + (LANE - FOX_HEADS,), a.dtype)
        return jnp.concatenate([a[..., :o_gn], a[..., o_glu:o_lf],
                                a[..., o_gn:o_glu], pad_gn, a[..., o_lf:o_mg], pad_lf], axis=-1)

    return (cols(w_in).astype(BF16), cols(b_in)[:, None, :],
            w_in[..., o_mg:].astype(BF16), b_in[:, None, o_mg:])


def _compress_weights(pe, wk, wv):
    pe2 = jnp.concatenate([pe, pe], axis=-1)

    def block_diag(w):
        z = jnp.zeros_like(w)
        return jnp.concatenate([jnp.concatenate([w, z], axis=-1), jnp.concatenate([z, w], axis=-1)],
                               axis=-2).astype(BF16)

    return pe2, block_diag(wk), block_diag(wv)


def _pad_tokens(a, n_seq, t_new):
    a = a.reshape(n_seq, t_new, a.shape[-1])
    return jnp.pad(a, ((0, 0), (0, TQ_PAD - t_new), (0, 0)))


def kernel(x_prompt, x_sample, cache_nsa_cmp_kv, cache_nsa_slc_kv, cache_fox_kv, cache_fox_logf,
           state_nsa_win_kv, state_conv, page_table, c_prompt, c_sample, w_ada, b_ada, norm_g,
           w_ffn_gu, w_ffn_down, w_in, b_in, nsa_cmp_pe, nsa_cmp_wk, nsa_cmp_wv, conv_dw_w, conv_dw_b,
           conv_ln_g, conv_ln_b, w_branch_nsa, w_branch_conv, w_branch_fox, w_out, final_norm_g):
    batch, seq, _ = x_prompt.shape
    n_seq, t_new, _ = x_sample.shape
    depth = w_ada.shape[0]
    n_pool, page = cache_nsa_cmp_kv.shape[1], cache_nsa_cmp_kv.shape[2]
    n_pages = page_table.shape[1]
    past = n_pages * page
    w_buf = state_nsa_win_kv.shape[2]
    mp, ms = batch * seq, n_seq * t_new
    tm_p = min(512, seq)
    tm_s = min(256, ms)

    w_ada_b = w_ada.astype(BF16)
    w_gu_b = w_ffn_gu.astype(BF16)
    w_down_b = w_ffn_down.astype(BF16)
    w_in_b, b_in_p, w_mg_b, b_mg = _aligned_in_proj(w_in, b_in)
    pe2, wk_bd, wv_bd = _compress_weights(nsa_cmp_pe, nsa_cmp_wk, nsa_cmp_wv)
    pe_t, wk_t, wv_t = _compress_weights(jnp.swapaxes(nsa_cmp_pe, 1, 2), jnp.swapaxes(nsa_cmp_wk, 1, 2),
                                         jnp.swapaxes(nsa_cmp_wv, 1, 2))
    wb_nsa_b, wb_conv_b = w_branch_nsa.astype(BF16), w_branch_conv.astype(BF16)
    wb_fox_b, w_out_b = w_branch_fox.astype(BF16), w_out.astype(BF16)
    norm_g4 = norm_g[:, :, None, :]
    dw_b3, ln_g3, ln_b3 = conv_dw_b[:, None, :], conv_ln_g[:, None, :], conv_ln_b[:, None, :]

    to_feature_major = (0, 1, 3, 4, 5, 2)
    cmp_pool = jnp.transpose(cache_nsa_cmp_kv, to_feature_major).reshape(depth, n_pool * KV_ROW, page)
    slc_pool = jnp.transpose(cache_nsa_slc_kv, to_feature_major).reshape(depth, n_pool, KV_ROW, page)
    fox_pool = jnp.transpose(cache_fox_kv, to_feature_major).reshape(depth, n_pool, FOX_ROW, page)
    lf_pool_t = jnp.pad(jnp.swapaxes(cache_fox_logf, 2, 3).astype(F32),
                        ((0, 0), (0, 0), (0, 8 - FOX_HEADS), (0, 0)))
    win_state = jnp.transpose(state_nsa_win_kv, to_feature_major).reshape(depth, n_seq, KV_ROW, w_buf)

    cos_p, sin_p = _rope_tables(jnp.arange(seq, dtype=jnp.int32))
    cos_s, sin_s = _rope_tables(past + jnp.arange(t_new, dtype=jnp.int32))
    cos_s, sin_s = jnp.tile(cos_s, (n_seq, 1)), jnp.tile(sin_s, (n_seq, 1))

    mod_all = ada_all_layers(jnp.concatenate([c_prompt, c_sample], axis=0), w_ada_b, b_ada)

    xp = x_prompt.reshape(mp, D_MODEL)
    xs = x_sample.reshape(ms, D_MODEL)
    out_p = [[] for _ in range(6)]
    out_s = [[] for _ in range(6)]

    for l in range(depth):
        mod_p = Mod(mod_all[l, :batch, None, :], seq)
        mod_s = Mod(jnp.repeat(mod_all[l, batch:], t_new, axis=0), 1)

        xp = ffn_half_step(xp, mod_p, 0, norm_g4, w_gu_b, w_down_b, l, 0, tm_p)
        xs = ffn_half_step(xs, mod_s, 0, norm_g4, w_gu_b, w_down_b, l, 0, tm_s)

        tm_in = min(512, seq)
        (q_p, kvc_p, kvs_p, kvw_p, glu_p, qf_p, kvf_p, gn_p, lf_p,
         kvc_pt, kvs_pt, kvw_pt, kvf_pt, lf_pt, kvs_pc, kvw_pc, kvf_pc) = input_projection(
            xp, mod_p, norm_g4, w_in_b, b_in_p, cos_p, sin_p, l, tm_in, seq // tm_in, feature_major_batch=batch)
        (q_s, kvc_s, kvs_s, kvw_s, glu_s, qf_s, kvf_s, gn_s, lf_s) = input_projection(
            xs, mod_s, norm_g4, w_in_b, b_in_p, cos_s, sin_s, l, tm_s, ms // tm_s)

        kcmp_p = nsa_compress_rows(kvc_p.reshape(mp * 2, NSA_KV), pe2, wk_bd, wv_bd, l, 128)
        o_nsa_p = nsa_attend_prompt(q_p, gn_p, kcmp_p, kvs_p, kvs_pc, kvw_p, kvw_pc, batch, seq)
        cum_p = cumsum_lanes(lf_pt)
        cum_q = jnp.swapaxes(cum_p, 1, 2).reshape(mp, 8)
        o_fox_p = fox_attend_prompt(qf_p, kvf_p, kvf_pc, cum_p, cum_q, batch, seq)
        y_conv_p = conv_module_prompt(glu_p, conv_dw_w, dw_b3, ln_g3, ln_b3, l, batch, seq)

        kcmp_pool = nsa_compress_pool(cmp_pool, pe_t, wk_t, wv_t, l, 128)
        o_nsa_s = nsa_attend_decode(page_table, _pad_tokens(q_s, n_seq, t_new), _pad_tokens(gn_s, n_seq, t_new),
                                    _pad_tokens(kvs_s, n_seq, t_new), _pad_tokens(kvw_s, n_seq, t_new),
                                    win_state, kcmp_pool, slc_pool, l, past)
        o_nsa_s = o_nsa_s[:, :t_new].reshape(ms, NSA_Q)
        lf_new_t = jnp.swapaxes(lf_s[:, :FOX_HEADS].reshape(n_seq, t_new, FOX_HEADS), 1, 2)
        lf_new_t = jnp.pad(lf_new_t, ((0, 0), (0, 8 - FOX_HEADS), (0, page - t_new)))
        o_fox_s = fox_attend_decode(page_table, _pad_tokens(qf_s, n_seq, t_new), _pad_tokens(kvf_s, n_seq, t_new),
                                    lf_new_t, fox_pool, lf_pool_t, l, past)
        o_fox_s = o_fox_s[:, :t_new].reshape(ms, FOX_W)
        state_t = jnp.swapaxes(state_conv[l], 0, 1)
        glu_t = jnp.swapaxes(glu_s.reshape(n_seq, t_new, CONV_CH), 0, 1)
        y_conv_s = conv_module_decode(state_t, glu_t, conv_dw_w, dw_b3, ln_g3, ln_b3, l)
        y_conv_s = jnp.swapaxes(y_conv_s, 0, 1).reshape(ms, CONV_CH)

        xp = merge_and_project(xp, mod_p, norm_g4, o_nsa_p, y_conv_p, o_fox_p, w_mg_b, b_mg,
                               wb_nsa_b, wb_conv_b, wb_fox_b, w_out_b, l, tm_p)
        xs = merge_and_project(xs, mod_s, norm_g4, o_nsa_s, y_conv_s, o_fox_s, w_mg_b, b_mg,
                               wb_nsa_b, wb_conv_b, wb_fox_b, w_out_b, l, tm_s)
        xp = ffn_half_step(xp, mod_p, 6, norm_g4, w_gu_b, w_down_b, l, 1, tm_p)
        xs = ffn_half_step(xs, mod_s, 6, norm_g4, w_gu_b, w_down_b, l, 1, tm_s)

        kv5 = (2, NSA_KV_HEADS, HEAD_DIM)
        glu_p3 = glu_p.reshape(batch, seq, CONV_CH)
        w_keep = min(NSA_WINDOW, seq)

        def token_major(a, heads):
            a = a.reshape(batch, 2, heads, HEAD_DIM, a.shape[-1])
            return jnp.transpose(a, (0, 4, 1, 2, 3))

        out_p[0].append(token_major(kvc_pt, NSA_KV_HEADS))
        out_p[1].append(token_major(kvs_pt, NSA_KV_HEADS))
        out_p[2].append(token_major(kvf_pt, FOX_HEADS))
        out_p[3].append(jnp.swapaxes(lf_pt[:, :FOX_HEADS], 1, 2))
        out_p[4].append(token_major(kvw_pt[:, :, seq - w_keep:], NSA_KV_HEADS))
        out_p[5].append(glu_p3[:, seq - (CONV_WIDTH - 1):])
        kvw_s5 = kvw_s.reshape((n_seq, t_new) + kv5)
        out_s[0].append(kvc_s.reshape((n_seq, t_new) + kv5))
        out_s[1].append(kvs_s.reshape((n_seq, t_new) + kv5))
        out_s[2].append(kvf_s.reshape(n_seq, t_new, 2, FOX_HEADS, HEAD_DIM))
        out_s[3].append(lf_s[:, :FOX_HEADS].reshape(n_seq, t_new, FOX_HEADS))
        out_s[4].append(kvw_s5)
        out_s[5].append(glu_s.reshape(n_seq, t_new, CONV_CH))

    y_p = final_norm(xp, final_norm_g, tm_p).reshape(batch, seq, D_MODEL)
    y_s = final_norm(xs, final_norm_g, tm_s).reshape(n_seq, t_new, D_MODEL)
    outs = [y_p, y_s]
    for f in range(6):
        outs.append(jnp.stack(out_p[f]))
        new_s = jnp.stack(out_s[f])
        if f == 4:
            new_s = jnp.concatenate([state_nsa_win_kv, new_s], axis=2)[:, :, t_new:]
        if f == 5:
            new_s = jnp.concatenate([state_conv, new_s], axis=2)
            new_s = new_s[:, :, new_s.shape[2] - (CONV_WIDTH - 1):]
        outs.append(new_s)
    return tuple(outs)
```

```python
import functools

import jax
import jax.numpy as jnp
from jax import lax
from jax.experimental import pallas as pl
from jax.experimental.pallas import tpu as pltpu

D_MODEL = 1024
HEAD_DIM = 64
NSA_HEADS = 8
NSA_KV_HEADS = 2
NSA_GROUP = NSA_HEADS // NSA_KV_HEADS
NSA_BLOCK = 64
NSA_TOP_K = 16
NSA_WINDOW = 512
FOX_HEADS = 4
CONV_CH = 256
CONV_WIDTH = 31
D_FF = 2816
ROPE_THETA = 10000.0
RMS_EPS = 1e-6
LN_EPS = 1e-5
NEG_INF = -1e30
POS_INF = 1e30
TINY = 1e-30
MASKED_BELOW = -5e29
SCALE = HEAD_DIM ** -0.5
LOG2E = 1.4426950408889634
SCALE_LOG2 = SCALE * LOG2E

NSA_Q = NSA_HEADS * HEAD_DIM
NSA_KV = NSA_KV_HEADS * HEAD_DIM
FOX_W = FOX_HEADS * HEAD_DIM
KV_ROW = 2 * NSA_KV
FOX_ROW = 2 * FOX_W

LANE = 128
SUBLANES = 8
C_Q = 0
C_KVC = C_Q + NSA_Q
C_KVS = C_KVC + KV_ROW
C_KVW = C_KVS + KV_ROW
C_GLU = C_KVW + KV_ROW
C_QF = C_GLU + 2 * CONV_CH
C_KVF = C_QF + FOX_W
C_GN = C_KVF + FOX_ROW
C_LF = C_GN + LANE
D_IN_PAD = C_LF + LANE

TQ_PAD = 8
DECODE_SEQS = 4
VMEM_LIMIT = 56 * 1024 * 1024

F32 = jnp.float32
BF16 = jnp.bfloat16
NT = (((1,), (1,)), ((), ()))


def _params(*sem):
    return pltpu.CompilerParams(dimension_semantics=sem, vmem_limit_bytes=VMEM_LIMIT)


def _dot(a, b):
    return jnp.dot(a, b, preferred_element_type=F32)


def _dot_nt(a, b):
    return lax.dot_general(a, b, NT, preferred_element_type=F32)


def _rms_mod(x, g, shift, scale):
    y = x * lax.rsqrt(jnp.mean(x * x, axis=-1, keepdims=True) + RMS_EPS) * g
    return y * (1.0 + scale) + shift


def _sigmoid(x):
    return 1.0 / (1.0 + jnp.exp(-x))


def _silu(x):
    return x * _sigmoid(x)


def _ada_kernel(c_ref, w_ref, b_ref, o_ref):
    c = _silu(c_ref[...]).astype(BF16)
    o_ref[...] = _dot(c, w_ref[...]) + b_ref[...]


def ada_all_layers(cond, w_ada, b_ada):
    depth, _, n = w_ada.shape
    rows = cond.shape[0]
    tn = n // 8
    return pl.pallas_call(
        _ada_kernel,
        grid=(depth, n // tn),
        in_specs=[pl.BlockSpec((rows, D_MODEL), lambda l, j: (0, 0)),
                  pl.BlockSpec((None, D_MODEL, tn), lambda l, j: (l, 0, j)),
                  pl.BlockSpec((None, 1, tn), lambda l, j: (l, 0, j))],
        out_specs=pl.BlockSpec((None, rows, tn), lambda l, j: (l, 0, j)),
        out_shape=jax.ShapeDtypeStruct((depth, rows, n), F32),
        compiler_params=_params("parallel", "parallel"),
        name="ada",
    )(cond, w_ada, b_ada.reshape(depth, 1, n))


class Mod:
    def __init__(self, arr, tokens_per_cond):
        self.arr = arr
        self.tokens_per_cond = tokens_per_cond

    def spec(self, k, tm):
        if self.arr.ndim == 3:
            per = self.tokens_per_cond // tm
            return pl.BlockSpec((None, 1, D_MODEL), lambda m, *_: (m // per, 0, k))
        return pl.BlockSpec((tm, D_MODEL), lambda m, *_: (m, k))


def _ffn_kernel(x_ref, g_ref, sh_ref, sc_ref, ga_ref, wg_ref, wu_ref, wd_ref, o_ref, h_scr, acc_scr):
    j = pl.program_id(1)

    @pl.when(j == 0)
    def _():
        h_scr[...] = _rms_mod(x_ref[...], g_ref[...], sh_ref[...], sc_ref[...]).astype(BF16)
        acc_scr[...] = jnp.zeros_like(acc_scr)

    h = h_scr[...]
    gate = _dot(h, wg_ref[...])
    up = _dot(h, wu_ref[...])
    a = (_silu(gate) * up).astype(BF16)
    acc_scr[...] += _dot(a, wd_ref[...])

    @pl.when(j == pl.num_programs(1) - 1)
    def _():
        o_ref[...] = x_ref[...] + 0.5 * ga_ref[...] * acc_scr[...]


def ffn_half_step(x, mod, k0, norm_g, w_gu, w_down, layer, idx, tm):
    m = x.shape[0]
    nf = 2
    tf = D_FF // nf
    return pl.pallas_call(
        _ffn_kernel,
        grid=(m // tm, nf),
        in_specs=[pl.BlockSpec((tm, D_MODEL), lambda i, j: (i, 0)),
                  pl.BlockSpec((None, None, 1, D_MODEL), lambda i, j: (layer, 2 * idx, 0, 0)),
                  mod.spec(k0, tm), mod.spec(k0 + 1, tm), mod.spec(k0 + 2, tm),
                  pl.BlockSpec((None, None, D_MODEL, tf), lambda i, j: (layer, idx, 0, j)),
                  pl.BlockSpec((None, None, D_MODEL, tf), lambda i, j: (layer, idx, 0, nf + j)),
                  pl.BlockSpec((None, None, tf, D_MODEL), lambda i, j: (layer, idx, j, 0))],
        out_specs=pl.BlockSpec((tm, D_MODEL), lambda i, j: (i, 0)),
        out_shape=jax.ShapeDtypeStruct((m, D_MODEL), F32),
        scratch_shapes=[pltpu.VMEM((tm, D_MODEL), BF16), pltpu.VMEM((tm, D_MODEL), F32)],
        compiler_params=_params("parallel", "arbitrary"),
        name="ffn",
    )(x, norm_g, mod.arr, mod.arr, mod.arr, w_gu, w_gu, w_down)


def _rope(z, cos, sin):
    n = z.shape[1]
    lane = lax.broadcasted_iota(jnp.int32, z.shape, 1)
    first_half = (lane % HEAD_DIM) < (HEAD_DIM // 2)
    partner = jnp.where(first_half, pltpu.roll(z, n - HEAD_DIM // 2, 1), pltpu.roll(z, HEAD_DIM // 2, 1))
    reps = n // LANE
    if reps > 1:
        cos = jnp.concatenate([cos] * reps, axis=1)
        sin = jnp.concatenate([sin] * reps, axis=1)
    return z * cos + partner * sin


def _inproj_kernel(x_ref, g_ref, sh_ref, sc_ref, cos_ref, sin_ref, w_ref, b_ref, *rest, n_alias=0):
    rest = rest[n_alias:]
    q_ref, kvc_ref, kvs_ref, kvw_ref, glu_ref, qf_ref, kvf_ref, gn_ref, lf_ref = rest[:9]
    feature_major_refs = rest[9:]
    u = _rms_mod(x_ref[...], g_ref[...], sh_ref[...], sc_ref[...]).astype(BF16)
    cos = cos_ref[...]
    sin = sin_ref[...]

    def proj(lo, hi):
        return _dot(u, w_ref[:, lo:hi]) + b_ref[:, lo:hi]

    def as_operand(ref, val, scale=None):
        if ref.dtype == BF16 and scale is not None:
            val = val * scale
        ref[...] = val.astype(ref.dtype)

    as_operand(q_ref, _rope(proj(C_Q, C_KVC), cos, sin), SCALE_LOG2)
    kv_rows = []
    for ref, lo in ((kvc_ref, C_KVC), (kvs_ref, C_KVS), (kvw_ref, C_KVW)):
        z = proj(lo, lo + KV_ROW)
        z = jnp.concatenate([_rope(z[:, :NSA_KV], cos, sin), z[:, NSA_KV:]], axis=1)
        as_operand(ref, z)
        kv_rows.append(z)
    z = proj(C_GLU, C_QF)
    glu_ref[...] = z[:, :CONV_CH] * _sigmoid(z[:, CONV_CH:])
    as_operand(qf_ref, proj(C_QF, C_KVF), SCALE_LOG2)
    kvf = proj(C_KVF, C_GN)
    as_operand(kvf_ref, kvf)
    gn_ref[...] = _sigmoid(proj(C_GN, C_LF))
    z = proj(C_LF, D_IN_PAD)
    lf = jnp.minimum(z, 0.0) - jnp.log(1.0 + jnp.exp(-jnp.abs(z)))
    lf_ref[...] = lf
    if feature_major_refs:
        kvc_t, kvs_t, kvw_t, kvf_t, lf_t, kvs_c, kvw_c, kvf_c = feature_major_refs
        for ref, rows, chunked in ((kvc_t, kv_rows[0], None), (kvs_t, kv_rows[1], kvs_c),
                                   (kvw_t, kv_rows[2], kvw_c), (kvf_t, kvf, kvf_c)):
            rows_t = rows.T
            ref[...] = rows_t
            if chunked is not None:
                for c in range(chunked.shape[0]):
                    chunked[c] = rows_t[:, c * LANE:(c + 1) * LANE].astype(BF16)
        lf_t[...] = lf.T[:lf_t.shape[0], :]


def input_projection(x, mod, norm_g, w_in, b_in, cos, sin, layer, tm, pos_tiles, feature_major_batch=None,
                     depth=1, stacked=()):
    m = x.shape[0]
    widths = (NSA_Q, KV_ROW, KV_ROW, KV_ROW, CONV_CH, FOX_W, FOX_ROW, LANE, LANE)
    out_specs = [pl.BlockSpec((tm, w), lambda i: (i, 0)) for w in widths]
    operand = (0, 2, 3, 5, 6) if feature_major_batch else ()
    out_shape = [jax.ShapeDtypeStruct((m, w), BF16 if k in operand else F32) for k, w in enumerate(widths)]
    if feature_major_batch:
        t = m // feature_major_batch
        for rows in (KV_ROW, KV_ROW, KV_ROW, FOX_ROW, 8):
            out_specs.append(pl.BlockSpec((None, None, rows, tm),
                                          lambda i: (layer, i // pos_tiles, 0, i % pos_tiles)))
            out_shape.append(jax.ShapeDtypeStruct((depth, feature_major_batch, rows, t), F32))
        for rows in (KV_ROW, KV_ROW, FOX_ROW):
            out_specs.append(pl.BlockSpec((None, tm // LANE, rows, LANE),
                                          lambda i: (i // pos_tiles, i % pos_tiles, 0, 0)))
            out_shape.append(jax.ShapeDtypeStruct((feature_major_batch, t // LANE, rows, LANE), BF16))
    n_in = 8
    return pl.pallas_call(
        functools.partial(_inproj_kernel, n_alias=len(stacked)),
        grid=(m // tm,),
        in_specs=[pl.BlockSpec((tm, D_MODEL), lambda i: (i, 0)),
                  pl.BlockSpec((None, None, 1, D_MODEL), lambda i: (layer, 1, 0, 0)),
                  mod.spec(3, tm), mod.spec(4, tm),
                  pl.BlockSpec((tm, LANE), lambda i: (i % pos_tiles, 0)),
                  pl.BlockSpec((tm, LANE), lambda i: (i % pos_tiles, 0)),
                  pl.BlockSpec((None, D_MODEL, D_IN_PAD), lambda i: (layer, 0, 0)),
                  pl.BlockSpec((None, 1, D_IN_PAD), lambda i: (layer, 0, 0))]
        + [pl.BlockSpec(memory_space=pl.ANY)] * len(stacked),
        out_specs=out_specs,
        out_shape=out_shape,
        input_output_aliases={n_in + k: len(widths) + k for k in range(len(stacked))},
        compiler_params=_params("parallel"),
        name="inproj",
    )(x, norm_g, mod.arr, mod.arr, cos, sin, w_in, b_in, *stacked)


def _compress_kernel(x_ref, pe_ref, wk_ref, wv_ref, o_ref):
    nblk = o_ref.shape[0]
    acc_k = jnp.zeros((nblk, NSA_KV), F32)
    acc_v = jnp.zeros((nblk, NSA_KV), F32)
    for l in range(NSA_BLOCK):
        pe = pe_ref[l:l + 1, :]
        k_rows = x_ref[pl.ds(2 * l, nblk, stride=2 * NSA_BLOCK), :] + pe
        v_rows = x_ref[pl.ds(2 * l + 1, nblk, stride=2 * NSA_BLOCK), :] + pe
        acc_k = acc_k + _dot(k_rows.astype(BF16), wk_ref[l])
        acc_v = acc_v + _dot(v_rows.astype(BF16), wv_ref[l])
    o_ref[:, :NSA_KV] = acc_k
    o_ref[:, NSA_KV:] = acc_v


def nsa_compress_rows(rows, pe2, wk_bd, wv_bd, layer, blocks_per_step):
    lead = rows.shape[:-2]
    nblk = rows.shape[-2] // (2 * NSA_BLOCK)
    nb = max(d for d in range(8, blocks_per_step + 1, 8) if nblk % d == 0)
    if lead:
        x_spec = pl.BlockSpec((None, nb * 2 * NSA_BLOCK, NSA_KV), lambda i: (layer, i, 0))
    else:
        x_spec = pl.BlockSpec((nb * 2 * NSA_BLOCK, NSA_KV), lambda i: (i, 0))
    w_spec = pl.BlockSpec((None, NSA_BLOCK, NSA_KV, NSA_KV), lambda i: (layer, 0, 0, 0))
    return pl.pallas_call(
        _compress_kernel,
        grid=(nblk // nb,),
        in_specs=[x_spec, pl.BlockSpec((None, NSA_BLOCK, NSA_KV), lambda i: (layer, 0, 0)), w_spec, w_spec],
        out_specs=pl.BlockSpec((nb, KV_ROW), lambda i: (i, 0)),
        out_shape=jax.ShapeDtypeStruct((nblk, KV_ROW), F32),
        compiler_params=_params("parallel"),
        name="nsa_compress",
    )(rows, pe2, wk_bd, wv_bd)


def _compress_pool_kernel(x_ref, pe_ref, wk_ref, wv_ref, o_ref, *, blocks_per_page):
    pages = o_ref.shape[0]
    page = x_ref.shape[1]
    for c in range(2 * NSA_KV_HEADS):
        w_ref = wk_ref if c < NSA_KV_HEADS else wv_ref
        acc = jnp.zeros((pages, page), F32)
        for d in range(HEAD_DIM):
            rows = x_ref[pl.ds(c * HEAD_DIM + d, pages, stride=KV_ROW), :] + pe_ref[d:d + 1, :]
            acc = acc + _dot(rows.astype(BF16), w_ref[d])
        for j in range(blocks_per_page):
            o_ref[:, j * KV_ROW + c * HEAD_DIM:j * KV_ROW + (c + 1) * HEAD_DIM] = acc[:, j * HEAD_DIM:(j + 1) * HEAD_DIM]


def nsa_compress_pool(pool_t, pe_t, wk_t, wv_t, layer, pages_per_step):
    page = pool_t.shape[-1]
    n_pool = pool_t.shape[1] // KV_ROW
    bpp = page // NSA_BLOCK
    assert page == LANE and bpp * HEAD_DIM == page
    pp = max(d for d in range(8, pages_per_step + 1, 8) if n_pool % d == 0)
    w_spec = pl.BlockSpec((None, HEAD_DIM, page, page), lambda i: (layer, 0, 0, 0))
    out = pl.pallas_call(
        functools.partial(_compress_pool_kernel, blocks_per_page=bpp),
        grid=(n_pool // pp,),
        in_specs=[pl.BlockSpec((None, pp * KV_ROW, page), lambda i: (layer, i, 0)),
                  pl.BlockSpec((None, HEAD_DIM, page), lambda i: (layer, 0, 0)), w_spec, w_spec],
        out_specs=pl.BlockSpec((pp, bpp * KV_ROW), lambda i: (i, 0)),
        out_shape=jax.ShapeDtypeStruct((n_pool, bpp * KV_ROW), F32),
        compiler_params=_params("parallel"),
        name="nsa_compress_pool",
    )(pool_t, pe_t, wk_t, wv_t)
    return out.reshape(n_pool, 1, bpp * KV_ROW)


def _softmax_rows(logits):
    m = jnp.max(logits, axis=-1, keepdims=True)
    p = jnp.where(logits > MASKED_BELOW, jnp.exp(logits - m), 0.0)
    return p / jnp.maximum(jnp.sum(p, axis=-1, keepdims=True), TINY)


STAT_ROWS = 8


def _flash_init(m_scr, l_scr, acc_scr):
    m_scr[...] = jnp.full(m_scr.shape, NEG_INF, F32)
    l_scr[...] = jnp.zeros(l_scr.shape, F32)
    acc_scr[...] = jnp.zeros(acc_scr.shape, F32)


def _flash_stats(s_t, m_scr, l_scr):
    m = m_scr[0:1, :]
    m_new = jnp.maximum(m, jnp.max(s_t, axis=0, keepdims=True))
    alpha = jnp.exp2(m - m_new)
    m_safe = jnp.where(m_new > MASKED_BELOW, m_new, 0.0)
    ps, sums = [], []
    for c in range(0, s_t.shape[1], LANE):
        p = jnp.exp2(s_t[:, c:c + LANE] - m_safe[:, c:c + LANE])
        sums.append(jnp.sum(p, axis=0, keepdims=True))
        ps.append(p.astype(BF16))
    l_scr[0:1, :] = alpha * l_scr[0:1, :] + jnp.concatenate(sums, axis=1)
    m_scr[0:1, :] = m_new
    return jnp.concatenate(ps, axis=1), alpha


def _select_blocks(score, t, nb):
    blk = lax.broadcasted_iota(jnp.int32, score.shape, 1)
    cur = t // NSA_BLOCK
    forced = jnp.where(blk == 0, 1, jnp.where(blk == cur, 1, jnp.where(blk == cur - 1, 1, 0)))
    s = jnp.where(forced == 1, POS_INF, jnp.where(blk > cur, NEG_INF, score))
    rank = jnp.zeros(score.shape, jnp.int32)
    for i in range(nb):
        col = s[:, i:i + 1]
        ahead = jnp.where(col > s, 1, jnp.where(col == s, jnp.where(blk > i, 1, 0), 0))
        rank = rank + ahead
    k_sel = min(NSA_TOP_K, nb)
    return jnp.where(rank < k_sel, jnp.where(blk < nb, 1.0, 0.0), 0.0)


def _expand_blocks(sel, k0, kc):
    width = sel.shape[1]
    kblk = (k0 + lax.broadcasted_iota(jnp.int32, (width, kc), 1)) // NSA_BLOCK
    e = jnp.where(kblk == lax.broadcasted_iota(jnp.int32, (width, kc), 0), 1.0, 0.0).astype(BF16)
    return _dot(sel.astype(BF16), e)


def _select_blocks_t(score_t, t_row, nb):
    blk = lax.broadcasted_iota(jnp.int32, score_t.shape, 0)
    cur = t_row // NSA_BLOCK
    forced = jnp.where(blk == 0, 1, jnp.where(blk == cur, 1, jnp.where(blk == cur - 1, 1, 0)))
    s = jnp.where(forced == 1, POS_INF, jnp.where(blk > cur, NEG_INF, score_t))
    rank = jnp.zeros(score_t.shape, jnp.int32)
    for i in range(nb):
        other = s[i:i + 1, :]
        rank = rank + jnp.where(other > s, 1, jnp.where(other == s, jnp.where(blk > i, 1, 0), 0))
    return jnp.where(rank < min(NSA_TOP_K, nb), 1.0, 0.0)


def _nsa_prompt_kernel(q_ref, gate_ref, kcmp_ref, kvs_ref, kvs_t_ref, kvw_ref, kvw_t_ref, o_ref,
                       qbd_scr, expand_scr, bias_scr, m_scr, l_scr, acc_scr, out_scr, *, tq):
    i = pl.program_id(1)
    t0 = i * tq
    t_row = t0 + lax.broadcasted_iota(jnp.int32, (1, tq), 1)
    nc = kcmp_ref.shape[0]
    lane = lax.broadcasted_iota(jnp.int32, (tq, LANE), 1)
    low_half = lane < HEAD_DIM

    pieces = []
    for h in range(NSA_HEADS):
        g = h // NSA_GROUP
        pair = q_ref[:, (h // 2) * LANE:(h // 2 + 1) * LANE].astype(F32)
        if h % 2 != g:
            pair = pltpu.roll(pair, HEAD_DIM, 1)
        pieces.append(jnp.where(low_half, pair, 0.0) if g == 0 else jnp.where(low_half, 0.0, pair))
    qbd_scr[...] = jnp.concatenate(pieces, axis=0).astype(BF16)
    qbd = qbd_scr[...]
    gates_t = gate_ref[...].T

    kcm = kcmp_ref[:, :NSA_KV].astype(BF16)
    vcm_t = jnp.concatenate([kcmp_ref[:, NSA_KV:], jnp.zeros((LANE - nc, NSA_KV), F32)], axis=0).T
    blk_t = lax.broadcasted_iota(jnp.int32, (nc, 1), 0)
    cmp_bias_t = jnp.where((blk_t + 1) * NSA_BLOCK - 1 <= t_row, 0.0, NEG_INF)
    lc_t = _dot_nt(kcm, qbd) + jnp.concatenate([cmp_bias_t] * NSA_HEADS, axis=1)
    m_t = jnp.max(lc_t, axis=0, keepdims=True)
    p_t = jnp.where(lc_t > MASKED_BELOW, jnp.exp2(lc_t - m_t), 0.0)
    pc_t = p_t / jnp.maximum(jnp.sum(p_t, axis=0, keepdims=True), TINY)
    o_cmp_t = _dot(vcm_t[:, :nc].astype(BF16), pc_t.astype(BF16))
    for h in range(NSA_HEADS):
        g = h // NSA_GROUP
        out_scr[h * HEAD_DIM:(h + 1) * HEAD_DIM, :] = (
            gates_t[3 * h:3 * h + 1, :] * o_cmp_t[g * HEAD_DIM:(g + 1) * HEAD_DIM, h * tq:(h + 1) * tq])

    kc = LANE
    qk = tq // kc
    r_key = lax.broadcasted_iota(jnp.int32, (kc, tq), 0)
    c_qry = lax.broadcasted_iota(jnp.int32, (kc, tq), 1)
    sels = []
    for g in range(NSA_KV_HEADS):
        base = g * NSA_GROUP * tq
        score_t = (pc_t[:, base:base + tq] + pc_t[:, base + tq:base + 2 * tq]
                   + pc_t[:, base + 2 * tq:base + 3 * tq] + pc_t[:, base + 3 * tq:base + 4 * tq])
        sel_t = _select_blocks_t(score_t, t_row, nc)
        sels.append(jnp.concatenate([sel_t, jnp.zeros((LANE - nc, tq), F32)], axis=0))
    sel_t = jnp.concatenate(sels, axis=1).astype(BF16)

    n_chunks = bias_scr.shape[1]
    span = min(4, n_chunks)

    @pl.when(i == 0)
    def _():
        kblk = lax.broadcasted_iota(jnp.int32, expand_scr.shape, 0) // NSA_BLOCK
        expand_scr[...] = jnp.where(kblk == lax.broadcasted_iota(jnp.int32, expand_scr.shape, 1),
                                    1.0, 0.0).astype(BF16)

    pair = min(2, n_chunks)
    n_pairs = ((i + 1) * qk - 1) // pair + 1
    t_rows = jnp.concatenate([t_row] * NSA_KV_HEADS, axis=1)
    for k in range(n_chunks // span):
        @pl.when(n_pairs * pair > k * span)
        def _(k=k):
            first = k * span * kc
            ok = _dot(expand_scr[first:first + span * kc, :], sel_t)
            kpos = first + lax.broadcasted_iota(jnp.int32, (span * kc, 1), 0)
            bias = jnp.where(kpos <= t_rows, jnp.where(ok > 0.5, 0.0, NEG_INF), NEG_INF)
            for jj in range(span):
                for g in range(NSA_KV_HEADS):
                    bias_scr[g, k * span + jj] = bias[jj * kc:(jj + 1) * kc, g * tq:(g + 1) * tq]

    group_w = NSA_GROUP * tq

    def attend_chunk(keys, vals_t, bias_per_group):
        s_t = _dot_nt(keys, qbd_scr[...])
        if bias_per_group is not None:
            s_t = s_t + jnp.concatenate([b for b in bias_per_group for _ in range(NSA_GROUP)], axis=1)
        p, alpha = _flash_stats(s_t, m_scr, l_scr)
        for g in range(NSA_KV_HEADS):
            cols = slice(g * group_w, (g + 1) * group_w)
            rows = slice(NSA_KV + g * HEAD_DIM, NSA_KV + (g + 1) * HEAD_DIM)
            v_t = jnp.concatenate([v[rows, :] for v in vals_t], axis=1)
            acc_scr[g] = alpha[:, cols] * acc_scr[g] + _dot(v_t, p[:, cols])

    def finish_branch(branch):
        inv_l = 1.0 / jnp.maximum(l_scr[0:1, :], TINY)
        for h in range(NSA_HEADS):
            g, r = divmod(h, NSA_GROUP)
            rs = slice(h * HEAD_DIM, (h + 1) * HEAD_DIM)
            o_t = acc_scr[g, :, r * tq:(r + 1) * tq] * inv_l[:, h * tq:(h + 1) * tq]
            out_scr[rs, :] = out_scr[rs, :] + gates_t[3 * h + branch:3 * h + branch + 1, :] * o_t

    _flash_init(m_scr, l_scr, acc_scr)

    def slc_body(jp, _):
        j = jp * pair
        k0 = pl.multiple_of(j * kc, pair * kc)
        attend_chunk(kvs_ref[pl.ds(k0, pair * kc), :NSA_KV], [kvs_t_ref.at[j + d] for d in range(pair)],
                     [jnp.concatenate([bias_scr[g, j + d] for d in range(pair)], axis=0)
                      for g in range(NSA_KV_HEADS)])
        return 0

    lax.fori_loop(0, n_pairs, slc_body, 0)
    finish_branch(1)

    _flash_init(m_scr, l_scr, acc_scr)
    n_win = NSA_WINDOW // kc + qk

    def win_pair(e_first, j_first):
        k0 = pl.multiple_of(j_first * kc, kc)
        biases, masked = [], False
        for e in range(e_first, e_first + pair):
            rel = e * kc - NSA_WINDOW + r_key - c_qry
            all_visible = e * kc >= tq and (e + 1) * kc <= NSA_WINDOW + 1
            masked = masked or not all_visible
            biases.append(jnp.where(rel <= 0, jnp.where(rel > -NSA_WINDOW, 0.0, NEG_INF), NEG_INF))
        attend_chunk(kvw_ref[pl.ds(k0, pair * kc), :NSA_KV], [kvw_t_ref.at[j_first + d] for d in range(pair)],
                     [jnp.concatenate(biases, axis=0)] * NSA_KV_HEADS if masked else None)

    for e in range(0, n_win, pair):
        jc = i * qk - NSA_WINDOW // kc + e
        pl.when(jc >= 0)(functools.partial(win_pair, e, jc))

    finish_branch(2)
    o_ref[...] = out_scr[...].T


def nsa_attend_prompt(q, gates, kcmp, kvs, kvs_t, kvw, kvw_t, batch, seq, tq=256):
    tq = min(tq, seq)
    nq = seq // tq
    nc = seq // NSA_BLOCK
    n_tiles = seq // LANE
    assert nc <= LANE and tq % (2 * LANE) == 0 and NSA_WINDOW % (2 * LANE) == 0 and n_tiles % min(4, n_tiles) == 0
    kern = functools.partial(_nsa_prompt_kernel, tq=tq)
    return pl.pallas_call(
        kern,
        grid=(batch, nq),
        in_specs=[pl.BlockSpec((tq, NSA_Q), lambda b, i: (b * nq + i, 0)),
                  pl.BlockSpec((tq, LANE), lambda b, i: (b * nq + i, 0)),
                  pl.BlockSpec((nc, KV_ROW), lambda b, i: (b, 0)),
                  pl.BlockSpec((seq, KV_ROW), lambda b, i: (b, 0)),
                  pl.BlockSpec((None, n_tiles, KV_ROW, LANE), lambda b, i: (b, 0, 0, 0)),
                  pl.BlockSpec((seq, KV_ROW), lambda b, i: (b, 0)),
                  pl.BlockSpec((None, n_tiles, KV_ROW, LANE), lambda b, i: (b, 0, 0, 0))],
        out_specs=pl.BlockSpec((tq, NSA_Q), lambda b, i: (b * nq + i, 0)),
        out_shape=jax.ShapeDtypeStruct((batch * seq, NSA_Q), F32),
        scratch_shapes=[pltpu.VMEM((NSA_HEADS * tq, LANE), BF16),
                        pltpu.VMEM((seq, LANE), BF16),
                        pltpu.VMEM((NSA_KV_HEADS, n_tiles, LANE, tq), F32),
                        pltpu.VMEM((STAT_ROWS, NSA_HEADS * tq), F32),
                        pltpu.VMEM((STAT_ROWS, NSA_HEADS * tq), F32),
                        pltpu.VMEM((NSA_KV_HEADS, HEAD_DIM, NSA_GROUP * tq), F32),
                        pltpu.VMEM((NSA_Q, tq), F32)],
        compiler_params=_params("parallel", "arbitrary"),
        name="nsa_prompt",
    )(q, gates, kcmp, kvs, kvs_t, kvw, kvw_t)


def _nsa_decode_kernel(pt_ref, q_ref, gate_ref, kvs_new_ref, kvw_new_ref, win_ref, *rest,
                       n_pages, page, past, seqs):
    del pt_ref
    kcmp_refs = rest[:seqs * n_pages]
    kvs_refs = rest[seqs * n_pages:2 * seqs * n_pages]
    o_ref, kcmp_scr = rest[2 * seqs * n_pages:]
    for s in range(seqs):
        _nsa_decode_one(q_ref.at[s], gate_ref.at[s], kvs_new_ref.at[s], kvw_new_ref.at[s], win_ref.at[s],
                        kcmp_refs[s * n_pages:(s + 1) * n_pages], kvs_refs[s * n_pages:(s + 1) * n_pages],
                        o_ref.at[s], kcmp_scr.at[s], n_pages=n_pages, page=page, past=past)


def _nsa_decode_one(q_ref, gate_ref, kvs_new_ref, kvw_new_ref, win_ref, kcmp_refs, kvs_refs, o_ref, kcmp_scr,
                    *, n_pages, page, past):
    tq = TQ_PAD
    rows = NSA_HEADS * tq
    blocks_per_page = page // NSA_BLOCK
    nc = n_pages * blocks_per_page
    nb = nc + 1
    for p in range(n_pages):
        for j in range(blocks_per_page):
            kcmp_scr[p * blocks_per_page + j:p * blocks_per_page + j + 1, :] = (
                kcmp_refs[p][:, j * KV_ROW:(j + 1) * KV_ROW])

    row = lax.broadcasted_iota(jnp.int32, (rows, 1), 0)
    t_rows = past + row % tq
    t1 = past + lax.broadcasted_iota(jnp.int32, (tq, 1), 0)
    lane = lax.broadcasted_iota(jnp.int32, (rows, LANE), 1)
    own_group = (lane // HEAD_DIM) == (row // (NSA_GROUP * tq))

    q = q_ref[...] * SCALE
    pieces = []
    for h in range(NSA_HEADS):
        qh = q[:, h * HEAD_DIM:(h + 1) * HEAD_DIM]
        zero = jnp.zeros_like(qh)
        pieces.append(jnp.concatenate([qh, zero] if h < NSA_GROUP else [zero, qh], axis=1))
    qbd = jnp.concatenate(pieces, axis=0).astype(BF16)

    def own_half(o):
        o = jnp.where(own_group, o, 0.0)
        return o[:, :HEAD_DIM] + o[:, HEAD_DIM:]

    kcm = kcmp_scr[:, :NSA_KV].astype(BF16)
    vcm = kcmp_scr[:, NSA_KV:].astype(BF16)
    blk = lax.broadcasted_iota(jnp.int32, (1, nc), 1)
    lc = _dot_nt(qbd, kcm) + jnp.where((blk + 1) * NSA_BLOCK - 1 <= t_rows, 0.0, NEG_INF)
    pc = _softmax_rows(lc)
    o_cmp = own_half(_dot(pc.astype(BF16), vcm))

    gates = gate_ref[...]
    new_k_pos = past + lax.broadcasted_iota(jnp.int32, (1, page), 1)
    pad = jnp.zeros((page - tq, KV_ROW), F32)
    kvs_new = jnp.concatenate([kvs_new_ref[...], pad], axis=0).astype(BF16)
    kvw_new = jnp.concatenate([kvw_new_ref[...], pad], axis=0).astype(BF16)

    n_keys = (n_pages + 1) * page
    key_pos = lax.broadcasted_iota(jnp.int32, (1, n_keys), 1)
    sels = []
    for g in range(NSA_KV_HEADS):
        pcg = pc[g * NSA_GROUP * tq:(g + 1) * NSA_GROUP * tq]
        score = pcg[0:tq] + pcg[tq:2 * tq] + pcg[2 * tq:3 * tq] + pcg[3 * tq:4 * tq]
        score = jnp.concatenate([score, jnp.zeros((tq, LANE - nc), F32)], axis=1)
        sels.append(_select_blocks(score, t1, nb))
    ok = _expand_blocks(jnp.concatenate(sels, axis=0), 0, n_keys)
    t2 = jnp.concatenate([t1] * NSA_KV_HEADS, axis=0)
    bias2 = jnp.where(key_pos <= t2, jnp.where(ok > 0.5, 0.0, NEG_INF), NEG_INF)
    bias = jnp.concatenate(
        [bias2[g * tq:(g + 1) * tq] for g in range(NSA_KV_HEADS) for _ in range(NSA_GROUP)], axis=0)
    s = jnp.concatenate([_dot(qbd, kvs_refs[p][:NSA_KV, :].astype(BF16)) for p in range(n_pages)]
                        + [_dot_nt(qbd, kvs_new[:, :NSA_KV])], axis=1) + bias
    m = jnp.max(s, axis=-1, keepdims=True)
    pr = jnp.where(s > MASKED_BELOW, jnp.exp(s - m), 0.0)
    den = jnp.maximum(jnp.sum(pr, axis=-1, keepdims=True), TINY)
    pr = pr.astype(BF16)
    acc = _dot(pr[:, n_pages * page:], kvs_new[:, NSA_KV:])
    for p in range(n_pages):
        acc = acc + _dot_nt(pr[:, p * page:(p + 1) * page], kvs_refs[p][NSA_KV:, :].astype(BF16))
    o_slc = own_half(acc / den)

    w_buf = win_ref.shape[1]
    wpos = jnp.concatenate([past - w_buf + lax.broadcasted_iota(jnp.int32, (1, w_buf), 1), new_k_pos], axis=1)
    ok = jnp.where(wpos <= t_rows, jnp.where(wpos > t_rows - NSA_WINDOW, jnp.where(wpos >= 0, 1.0, 0.0), 0.0), 0.0)
    s = jnp.concatenate([_dot(qbd, win_ref[:NSA_KV, :].astype(BF16)), _dot_nt(qbd, kvw_new[:, :NSA_KV])], axis=1)
    s = s + jnp.where(ok > 0.5, 0.0, NEG_INF)
    m = jnp.max(s, axis=-1, keepdims=True)
    pr = jnp.where(s > MASKED_BELOW, jnp.exp(s - m), 0.0)
    den = jnp.maximum(jnp.sum(pr, axis=-1, keepdims=True), TINY)
    pr = pr.astype(BF16)
    acc = _dot_nt(pr[:, :w_buf], win_ref[NSA_KV:, :].astype(BF16)) + _dot(pr[:, w_buf:], kvw_new[:, NSA_KV:])
    o_win = own_half(acc / den)

    for h in range(NSA_HEADS):
        rs = slice(h * tq, (h + 1) * tq)
        o = (gates[:, 3 * h:3 * h + 1] * o_cmp[rs] + gates[:, 3 * h + 1:3 * h + 2] * o_slc[rs]
             + gates[:, 3 * h + 2:3 * h + 3] * o_win[rs])
        o_ref[:, h * HEAD_DIM:(h + 1) * HEAD_DIM] = o


def nsa_attend_decode(page_table, q, gates, kvs_new, kvw_new, win_state, kcmp_pool, slc_pool, layer, past):
    n_seq, n_pages = page_table.shape
    page = slc_pool.shape[3]
    bpp = page // NSA_BLOCK
    w_buf = win_state.shape[3]

    seqs = DECODE_SEQS if n_seq % DECODE_SEQS == 0 else 1

    def seq_spec(width):
        return pl.BlockSpec((seqs, TQ_PAD, width), lambda b, pt: (b, 0, 0))

    def kcmp_spec(s, p):
        return pl.BlockSpec((None, 1, bpp * KV_ROW), lambda b, pt: (pt[b * seqs + s, p], 0, 0))

    def page_spec(s, p):
        return pl.BlockSpec((None, None, KV_ROW, page), lambda b, pt: (layer, pt[b * seqs + s, p], 0, 0))

    assert page % NSA_BLOCK == 0 and TQ_PAD <= NSA_BLOCK and n_pages * bpp + 1 <= LANE
    kern = functools.partial(_nsa_decode_kernel, n_pages=n_pages, page=page, past=past, seqs=seqs)
    grid_spec = pltpu.PrefetchScalarGridSpec(
        num_scalar_prefetch=1,
        grid=(n_seq // seqs,),
        in_specs=([seq_spec(NSA_Q), seq_spec(LANE), seq_spec(KV_ROW), seq_spec(KV_ROW),
                   pl.BlockSpec((None, seqs, KV_ROW, w_buf), lambda b, pt: (layer, b, 0, 0))]
                  + [kcmp_spec(s, p) for s in range(seqs) for p in range(n_pages)]
                  + [page_spec(s, p) for s in range(seqs) for p in range(n_pages)]),
        out_specs=seq_spec(NSA_Q),
        scratch_shapes=[pltpu.VMEM((seqs, n_pages * bpp, KV_ROW), F32)],
    )
    n_refs = seqs * n_pages
    return pl.pallas_call(
        kern,
        grid_spec=grid_spec,
        out_shape=jax.ShapeDtypeStruct((n_seq, TQ_PAD, NSA_Q), F32),
        compiler_params=_params("arbitrary"),
        name="nsa_decode",
    )(page_table, q, gates, kvs_new, kvw_new, win_state, *([kcmp_pool] * n_refs), *([slc_pool] * n_refs))


def _lane_cumsum(x):
    n = x.shape[-1]
    lane = lax.broadcasted_iota(jnp.int32, x.shape, x.ndim - 1)
    s = 1
    while s < n:
        x = x + jnp.where(lane >= s, pltpu.roll(x, s, x.ndim - 1), 0.0)
        s *= 2
    return x


def _cumsum_kernel(x_ref, o_ref):
    o_ref[...] = _lane_cumsum(x_ref[...]) * LOG2E


def cumsum_lanes(x):
    b, r, t = x.shape
    return pl.pallas_call(
        _cumsum_kernel,
        grid=(b,),
        in_specs=[pl.BlockSpec((None, r, t), lambda i: (i, 0, 0))],
        out_specs=pl.BlockSpec((None, r, t), lambda i: (i, 0, 0)),
        out_shape=jax.ShapeDtypeStruct(x.shape, F32),
        compiler_params=_params("parallel"),
        name="logf_cumsum",
    )(x)


def _fox_prompt_kernel(q_ref, kv_ref, kv_t_ref, cum_ref, cumq_ref, o_ref,
                       qbd_scr, m_scr, l_scr, acc_scr, out_scr, *, tq):
    i = pl.program_id(1)
    lane = lax.broadcasted_iota(jnp.int32, (tq, FOX_W), 1)
    q = q_ref[...].astype(F32)
    qbd_scr[...] = jnp.concatenate(
        [jnp.where(lane // HEAD_DIM == h, q, 0.0) for h in range(FOX_HEADS)], axis=0).astype(BF16)
    cum_q = cum_ref[i]
    cum_q = jnp.concatenate([cum_q[h:h + 1, :] for h in range(FOX_HEADS)], axis=1)
    r_key = lax.broadcasted_iota(jnp.int32, (tq, tq), 0)
    c_qry = lax.broadcasted_iota(jnp.int32, (tq, tq), 1)
    causal_bias = jnp.concatenate([jnp.where(r_key <= c_qry, 0.0, NEG_INF)] * FOX_HEADS, axis=1)
    _flash_init(m_scr, l_scr, acc_scr)

    per_chunk = tq // LANE

    def chunk(j, causal):
        k0 = pl.multiple_of(j * tq, tq)
        cum_k = cumq_ref[pl.ds(k0, tq), :]
        cum_k = jnp.concatenate(
            [jnp.broadcast_to(cum_k[:, h:h + 1], (tq, tq)) for h in range(FOX_HEADS)], axis=1)
        s_t = _dot_nt(kv_ref[pl.ds(k0, tq), :FOX_W], qbd_scr[...]) + (cum_q - cum_k)
        if causal:
            s_t = s_t + causal_bias
        p, alpha = _flash_stats(s_t, m_scr, l_scr)
        for h in range(FOX_HEADS):
            cols = slice(h * tq, (h + 1) * tq)
            rows = slice(FOX_W + h * HEAD_DIM, FOX_W + (h + 1) * HEAD_DIM)
            v_t = jnp.concatenate([kv_t_ref[j * per_chunk + d, rows, :] for d in range(per_chunk)], axis=1)
            acc_scr[h] = alpha[:, cols] * acc_scr[h] + _dot(v_t, p[:, cols])

    def body(j, _):
        chunk(j, False)
        return 0

    lax.fori_loop(0, i, body, 0)
    chunk(i, True)
    inv_l = 1.0 / jnp.maximum(l_scr[0:1, :], TINY)
    for h in range(FOX_HEADS):
        out_scr[h * HEAD_DIM:(h + 1) * HEAD_DIM, :] = acc_scr[h] * inv_l[:, h * tq:(h + 1) * tq]
    o_ref[...] = out_scr[...].T


def fox_attend_prompt(qf, kvf, kvf_t, cum, cum_q, batch, seq, tq=256):
    tq = min(tq, seq)
    nq = seq // tq
    assert tq % LANE == 0
    kern = functools.partial(_fox_prompt_kernel, tq=tq)
    return pl.pallas_call(
        kern,
        grid=(batch, nq),
        in_specs=[pl.BlockSpec((tq, FOX_W), lambda b, i: (b * nq + i, 0)),
                  pl.BlockSpec((seq, FOX_ROW), lambda b, i: (b, 0)),
                  pl.BlockSpec((None, seq // LANE, FOX_ROW, LANE), lambda b, i: (b, 0, 0, 0)),
                  pl.BlockSpec((None, nq, 8, tq), lambda b, i: (b, 0, 0, 0)),
                  pl.BlockSpec((seq, 8), lambda b, i: (b, 0))],
        out_specs=pl.BlockSpec((tq, FOX_W), lambda b, i: (b * nq + i, 0)),
        out_shape=jax.ShapeDtypeStruct((batch * seq, FOX_W), F32),
        scratch_shapes=[pltpu.VMEM((FOX_HEADS * tq, FOX_W), BF16),
                        pltpu.VMEM((STAT_ROWS, FOX_HEADS * tq), F32),
                        pltpu.VMEM((STAT_ROWS, FOX_HEADS * tq), F32),
                        pltpu.VMEM((FOX_HEADS, HEAD_DIM, tq), F32),
                        pltpu.VMEM((FOX_W, tq), F32)],
        compiler_params=_params("parallel", "arbitrary"),
        name="fox_prompt",
    )(qf, kvf, kvf_t, jnp.swapaxes(cum.reshape(batch, 8, nq, tq), 1, 2), cum_q)


def _fox_decode_kernel(pt_ref, q_ref, kv_new_ref, lf_new_ref, *rest, n_pages, page, past, seqs):
    del pt_ref
    kv_refs = rest[:seqs * n_pages]
    lf_refs = rest[seqs * n_pages:2 * seqs * n_pages]
    o_ref = rest[2 * seqs * n_pages]
    for s in range(seqs):
        _fox_decode_one(q_ref.at[s], kv_new_ref.at[s], lf_new_ref.at[s], kv_refs[s * n_pages:(s + 1) * n_pages],
                        lf_refs[s * n_pages:(s + 1) * n_pages], o_ref.at[s], page=page, past=past)


def _fox_decode_one(q_ref, kv_new_ref, lf_new_ref, kv_refs, lf_refs, o_ref, *, page, past):
    n_pages = len(kv_refs)
    tq = TQ_PAD
    rows = FOX_HEADS * tq
    row = lax.broadcasted_iota(jnp.int32, (rows, 1), 0)
    lane = lax.broadcasted_iota(jnp.int32, (rows, FOX_W), 1)
    own_head = (lane // HEAD_DIM) == (row // tq)
    t_rows = past + row % tq

    q = q_ref[...] * SCALE
    qbd = jnp.where(own_head, jnp.concatenate([q] * FOX_HEADS, axis=0), 0.0).astype(BF16)

    def head_rows(a):
        return jnp.concatenate(
            [jnp.broadcast_to(a[h:h + 1, :], (tq, a.shape[1])) for h in range(FOX_HEADS)], axis=0)

    lf_past = jnp.concatenate([r[...] for r in lf_refs], axis=1)
    cum_past = _lane_cumsum(lf_past)
    cum_new = cum_past[:, past - 1:past] + _lane_cumsum(lf_new_ref[...])
    ck_past = head_rows(cum_past)
    ck_new = head_rows(cum_new)
    new_lane = lax.broadcasted_iota(jnp.int32, (rows, page), 1)
    cq = jnp.sum(jnp.where(new_lane == row % tq, ck_new, 0.0), axis=-1, keepdims=True)

    kv_new = jnp.concatenate([kv_new_ref[...], jnp.zeros((page - tq, FOX_ROW), F32)], axis=0).astype(BF16)
    key_pos = lax.broadcasted_iota(jnp.int32, (1, past + page), 1)
    s = jnp.concatenate([_dot(qbd, kv_refs[p][:FOX_W, :].astype(BF16)) for p in range(n_pages)]
                        + [_dot_nt(qbd, kv_new[:, :FOX_W])], axis=1)
    s = s + (cq - jnp.concatenate([ck_past, ck_new], axis=1)) + jnp.where(key_pos <= t_rows, 0.0, NEG_INF)
    m = jnp.max(s, axis=-1, keepdims=True)
    pr = jnp.where(s > MASKED_BELOW, jnp.exp(s - m), 0.0)
    den = jnp.maximum(jnp.sum(pr, axis=-1, keepdims=True), TINY)
    pr = pr.astype(BF16)
    acc = _dot(pr[:, past:], kv_new[:, FOX_W:])
    for p in range(n_pages):
        acc = acc + _dot_nt(pr[:, p * page:(p + 1) * page], kv_refs[p][FOX_W:, :].astype(BF16))
    o = jnp.where(own_head, acc / den, 0.0)
    o_ref[...] = o[0:tq] + o[tq:2 * tq] + o[2 * tq:3 * tq] + o[3 * tq:4 * tq]


def fox_attend_decode(page_table, qf, kvf_new, lf_new_t, fox_pool, lf_pool_t, layer, past):
    n_seq, n_pages = page_table.shape
    page = fox_pool.shape[3]

    seqs = DECODE_SEQS if n_seq % DECODE_SEQS == 0 else 1

    def seq_spec(r, width):
        return pl.BlockSpec((seqs, r, width), lambda b, pt: (b, 0, 0))

    def page_spec(shape, s, p):
        return pl.BlockSpec((None, None) + shape, lambda b, pt: (layer, pt[b * seqs + s, p], 0, 0))

    assert TQ_PAD <= page
    kern = functools.partial(_fox_decode_kernel, n_pages=n_pages, page=page, past=past, seqs=seqs)
    grid_spec = pltpu.PrefetchScalarGridSpec(
        num_scalar_prefetch=1,
        grid=(n_seq // seqs,),
        in_specs=([seq_spec(TQ_PAD, FOX_W), seq_spec(TQ_PAD, FOX_ROW), seq_spec(8, page)]
                  + [page_spec((FOX_ROW, page), s, p) for s in range(seqs) for p in range(n_pages)]
                  + [page_spec((8, page), s, p) for s in range(seqs) for p in range(n_pages)]),
        out_specs=seq_spec(TQ_PAD, FOX_W),
    )
    n_refs = seqs * n_pages
    return pl.pallas_call(
        kern,
        grid_spec=grid_spec,
        out_shape=jax.ShapeDtypeStruct((n_seq, TQ_PAD, FOX_W), F32),
        compiler_params=_params("arbitrary"),
        name="fox_decode",
    )(page_table, qf, kvf_new, lf_new_t, *([fox_pool] * n_refs), *([lf_pool_t] * n_refs))


CONV_PAD = 32
CONV_SUB = 64


def _ln_silu(y, g, b):
    yc = y - jnp.mean(y, axis=-1, keepdims=True)
    var = jnp.mean(yc * yc, axis=-1, keepdims=True)
    return _silu(yc * lax.rsqrt(var + LN_EPS) * g + b)


def _conv_prompt_kernel(prev_ref, cur_ref, w_ref, b_ref, g_ref, bb_ref, o_ref, ctx_scr, shift_scr, *, tc):
    k = pl.program_id(1)
    ctx_scr[0:CONV_PAD, :] = jnp.where(k > 0, prev_ref[...], 0.0)
    ctx_scr[CONV_PAD:, :] = cur_ref[...]
    lead = CONV_PAD - (CONV_WIDTH - 1)
    n_fill = tc + CONV_PAD - SUBLANES
    for r in range(1, SUBLANES):
        shift_scr[r - 1, 0:n_fill, :] = ctx_scr[pl.ds(r, n_fill), :]
    for sub in range(tc // CONV_SUB):
        acc = jnp.zeros((CONV_SUB, CONV_CH), F32) + b_ref[...]
        for w in range(CONV_WIDTH):
            q, r = divmod(lead + w, SUBLANES)
            src = ctx_scr if r == 0 else shift_scr.at[r - 1]
            acc = acc + src[pl.ds(sub * CONV_SUB + q * SUBLANES, CONV_SUB), :] * w_ref[w:w + 1, :]
        o_ref[sub * CONV_SUB:(sub + 1) * CONV_SUB, :] = _ln_silu(acc, g_ref[...], bb_ref[...])


def conv_module_prompt(glu, dw_w, dw_b, ln_g, ln_b, layer, batch, seq, tc=256):
    tc = min(tc, seq)
    nt = seq // tc
    per = tc // CONV_PAD

    def vec_spec():
        return pl.BlockSpec((None, 1, CONV_CH), lambda b, k: (layer, 0, 0))

    return pl.pallas_call(
        functools.partial(_conv_prompt_kernel, tc=tc),
        grid=(batch, nt),
        in_specs=[pl.BlockSpec((CONV_PAD, CONV_CH), lambda b, k: (jnp.maximum((b * nt + k) * per - 1, 0), 0)),
                  pl.BlockSpec((tc, CONV_CH), lambda b, k: (b * nt + k, 0)),
                  pl.BlockSpec((None, CONV_WIDTH, CONV_CH), lambda b, k: (layer, 0, 0)),
                  vec_spec(), vec_spec(), vec_spec()],
        out_specs=pl.BlockSpec((tc, CONV_CH), lambda b, k: (b * nt + k, 0)),
        out_shape=jax.ShapeDtypeStruct((batch * seq, CONV_CH), F32),
        scratch_shapes=[pltpu.VMEM((CONV_PAD + tc, CONV_CH), F32),
                        pltpu.VMEM((SUBLANES - 1, CONV_PAD + tc, CONV_CH), F32)],
        compiler_params=_params("parallel", "arbitrary"),
        name="conv_prompt",
    )(glu, glu, dw_w, dw_b, ln_g, ln_b)


def _conv_decode_kernel(state_ref, glu_ref, w_ref, b_ref, g_ref, bb_ref, o_ref):
    n_state = state_ref.shape[0]
    t_new = glu_ref.shape[0]
    ctx = [state_ref[i] for i in range(n_state)] + [glu_ref[i] for i in range(t_new)]
    for i in range(t_new):
        acc = jnp.zeros(ctx[0].shape, F32) + b_ref[...]
        for w in range(CONV_WIDTH):
            acc = acc + ctx[i + w] * w_ref[w:w + 1, :]
        o_ref[i] = _ln_silu(acc, g_ref[...], bb_ref[...])


def conv_module_decode(state_t, glu_t, dw_w, dw_b, ln_g, ln_b, layer):
    t_new, n_seq, _ = glu_t.shape

    def vec_spec():
        return pl.BlockSpec((None, 1, CONV_CH), lambda i: (layer, 0, 0))

    return pl.pallas_call(
        _conv_decode_kernel,
        grid=(1,),
        in_specs=[pl.BlockSpec(state_t.shape, lambda i: (0, 0, 0)),
                  pl.BlockSpec(glu_t.shape, lambda i: (0, 0, 0)),
                  pl.BlockSpec((None, CONV_WIDTH, CONV_CH), lambda i: (layer, 0, 0)),
                  vec_spec(), vec_spec(), vec_spec()],
        out_specs=pl.BlockSpec((t_new, n_seq, CONV_CH), lambda i: (0, 0, 0)),
        out_shape=jax.ShapeDtypeStruct((t_new, n_seq, CONV_CH), F32),
        compiler_params=_params("arbitrary"),
        name="conv_decode",
    )(state_t, glu_t, dw_w, dw_b, ln_g, ln_b)


def _merge_kernel(x_ref, g_ref, sh_ref, sc_ref, ga_ref, on_ref, yc_ref, of_ref,
                  wg_ref, bg_ref, wn_ref, wc_ref, wf_ref, wo_ref, o_ref):
    x = x_ref[...]
    u = _rms_mod(x, g_ref[...], sh_ref[...], sc_ref[...]).astype(BF16)

    def gate(k):
        return _sigmoid(_dot(u, wg_ref[:, k * D_MODEL:(k + 1) * D_MODEL]) + bg_ref[:, k * D_MODEL:(k + 1) * D_MODEL])

    merged = (gate(0) * _dot(on_ref[...].astype(BF16), wn_ref[...])
              + gate(1) * _dot(yc_ref[...].astype(BF16), wc_ref[...])
              + gate(2) * _dot(of_ref[...].astype(BF16), wf_ref[...]))
    o_ref[...] = x + ga_ref[...] * _dot(merged.astype(BF16), wo_ref[...])


def merge_and_project(x, mod, norm_g, o_nsa, y_conv, o_fox, w_mg, b_mg, wb_nsa, wb_conv, wb_fox, w_out, layer, tm):
    m = x.shape[0]

    def row_spec(width):
        return pl.BlockSpec((tm, width), lambda i: (i, 0))

    def w_spec(rows, cols=D_MODEL):
        return pl.BlockSpec((None, rows, cols), lambda i: (layer, 0, 0))

    return pl.pallas_call(
        _merge_kernel,
        grid=(m // tm,),
        in_specs=[row_spec(D_MODEL), pl.BlockSpec((None, None, 1, D_MODEL), lambda i: (layer, 1, 0, 0)),
                  mod.spec(3, tm), mod.spec(4, tm), mod.spec(5, tm),
                  row_spec(NSA_Q), row_spec(CONV_CH), row_spec(FOX_W),
                  w_spec(D_MODEL, 3 * D_MODEL), w_spec(1, 3 * D_MODEL),
                  w_spec(NSA_Q), w_spec(CONV_CH), w_spec(FOX_W), w_spec(D_MODEL)],
        out_specs=row_spec(D_MODEL),
        out_shape=jax.ShapeDtypeStruct((m, D_MODEL), F32),
        compiler_params=_params("parallel"),
        name="merge",
    )(x, norm_g, mod.arr, mod.arr, mod.arr, o_nsa, y_conv, o_fox, w_mg, b_mg, wb_nsa, wb_conv, wb_fox, w_out)


def _final_norm_kernel(x_ref, g_ref, o_ref):
    x = x_ref[...]
    o_ref[...] = x * lax.rsqrt(jnp.mean(x * x, axis=-1, keepdims=True) + RMS_EPS) * g_ref[...]


def final_norm(x, g, tm):
    m = x.shape[0]
    return pl.pallas_call(
        _final_norm_kernel,
        grid=(m // tm,),
        in_specs=[pl.BlockSpec((tm, D_MODEL), lambda i: (i, 0)), pl.BlockSpec((1, D_MODEL), lambda i: (0, 0))],
        out_specs=pl.BlockSpec((tm, D_MODEL), lambda i: (i, 0)),
        out_shape=jax.ShapeDtypeStruct((m, D_MODEL), F32),
        compiler_params=_params("parallel"),
        name="final_norm",
    )(x, g.reshape(1, D_MODEL))


def _rope_tables(pos):
    half = HEAD_DIM // 2
    inv_freq = ROPE_THETA ** (-jnp.arange(half, dtype=F32) / half)
    ang = pos.astype(F32)[:, None] * inv_freq[None, :]
    cos, sin = jnp.cos(ang), jnp.sin(ang)
    cos = jnp.concatenate([cos, cos] * (LANE // HEAD_DIM), axis=1)
    sin = jnp.concatenate([-sin, sin] * (LANE // HEAD_DIM), axis=1)
    return cos, sin


def _aligned_in_proj(w_in, b_in):
    o_gn = NSA_Q + 3 * KV_ROW
    o_glu = o_gn + 3 * NSA_HEADS
    o_qkvf = o_glu + 2 * CONV_CH
    o_lf = o_qkvf + 3 * FOX_W
    o_mg = o_lf + FOX_HEADS

    def cols(a):
        pad_gn = jnp.zeros(a.shape[:-1] + (LANE - 3 * NSA_HEADS,), a.dtype)
        pad_lf = jnp.zeros(a.shape[:-1] + (LANE - FOX_HEADS,), a.dtype)
        return jnp.concatenate([a[..., :o_gn], a[..., o_glu:o_lf],
                                a[..., o_gn:o_glu], pad_gn, a[..., o_lf:o_mg], pad_lf], axis=-1)

    return (cols(w_in).astype(BF16), cols(b_in)[:, None, :],
            w_in[..., o_mg:].astype(BF16), b_in[:, None, o_mg:])


def _compress_weights(pe, wk, wv):
    pe2 = jnp.concatenate([pe, pe], axis=-1)

    def block_diag(w):
        z = jnp.zeros_like(w)
        return jnp.concatenate([jnp.concatenate([w, z], axis=-1), jnp.concatenate([z, w], axis=-1)],
                               axis=-2).astype(BF16)

    return pe2, block_diag(wk), block_diag(wv)


def _pad_tokens(a, n_seq, t_new):
    a = a.reshape(n_seq, t_new, a.shape[-1])
    return jnp.pad(a, ((0, 0), (0, TQ_PAD - t_new), (0, 0)))


def kernel(x_prompt, x_sample, cache_nsa_cmp_kv, cache_nsa_slc_kv, cache_fox_kv, cache_fox_logf,
           state_nsa_win_kv, state_conv, page_table, c_prompt, c_sample, w_ada, b_ada, norm_g,
           w_ffn_gu, w_ffn_down, w_in, b_in, nsa_cmp_pe, nsa_cmp_wk, nsa_cmp_wv, conv_dw_w, conv_dw_b,
           conv_ln_g, conv_ln_b, w_branch_nsa, w_branch_conv, w_branch_fox, w_out, final_norm_g):
    batch, seq, _ = x_prompt.shape
    n_seq, t_new, _ = x_sample.shape
    depth = w_ada.shape[0]
    n_pool, page = cache_nsa_cmp_kv.shape[1], cache_nsa_cmp_kv.shape[2]
    n_pages = page_table.shape[1]
    past = n_pages * page
    w_buf = state_nsa_win_kv.shape[2]
    mp, ms = batch * seq, n_seq * t_new
    tm_p = min(512, seq)
    tm_s = min(256, ms)

    w_ada_b = w_ada.astype(BF16)
    w_gu_b = w_ffn_gu.astype(BF16)
    w_down_b = w_ffn_down.astype(BF16)
    w_in_b, b_in_p, w_mg_b, b_mg = _aligned_in_proj(w_in, b_in)
    pe2, wk_bd, wv_bd = _compress_weights(nsa_cmp_pe, nsa_cmp_wk, nsa_cmp_wv)
    pe_t, wk_t, wv_t = _compress_weights(jnp.swapaxes(nsa_cmp_pe, 1, 2), jnp.swapaxes(nsa_cmp_wk, 1, 2),
                                         jnp.swapaxes(nsa_cmp_wv, 1, 2))
    wb_nsa_b, wb_conv_b = w_branch_nsa.astype(BF16), w_branch_conv.astype(BF16)
    wb_fox_b, w_out_b = w_branch_fox.astype(BF16), w_out.astype(BF16)
    norm_g4 = norm_g[:, :, None, :]
    dw_b3, ln_g3, ln_b3 = conv_dw_b[:, None, :], conv_ln_g[:, None, :], conv_ln_b[:, None, :]

    to_feature_major = (0, 1, 3, 4, 5, 2)
    cmp_pool = jnp.transpose(cache_nsa_cmp_kv, to_feature_major).reshape(depth, n_pool * KV_ROW, page)
    slc_pool = jnp.transpose(cache_nsa_slc_kv, to_feature_major).reshape(depth, n_pool, KV_ROW, page)
    fox_pool = jnp.transpose(cache_fox_kv, to_feature_major).reshape(depth, n_pool, FOX_ROW, page)
    lf_pool_t = jnp.pad(jnp.swapaxes(cache_fox_logf, 2, 3).astype(F32),
                        ((0, 0), (0, 0), (0, 8 - FOX_HEADS), (0, 0)))
    win_state = jnp.transpose(state_nsa_win_kv, to_feature_major).reshape(depth, n_seq, KV_ROW, w_buf)

    cos_p, sin_p = _rope_tables(jnp.arange(seq, dtype=jnp.int32))
    cos_s, sin_s = _rope_tables(past + jnp.arange(t_new, dtype=jnp.int32))
    cos_s, sin_s = jnp.tile(cos_s, (n_seq, 1)), jnp.tile(sin_s, (n_seq, 1))

    mod_all = ada_all_layers(jnp.concatenate([c_prompt, c_sample], axis=0), w_ada_b, b_ada)

    xp = x_prompt.reshape(mp, D_MODEL)
    xs = x_sample.reshape(ms, D_MODEL)
    stacked_p, conv_state_p = (), []
    out_s = [[] for _ in range(6)]

    for l in range(depth):
        mod_p = Mod(mod_all[l, :batch, None, :], seq)
        mod_s = Mod(jnp.repeat(mod_all[l, batch:], t_new, axis=0), 1)

        xp = ffn_half_step(xp, mod_p, 0, norm_g4, w_gu_b, w_down_b, l, 0, tm_p)
        xs = ffn_half_step(xs, mod_s, 0, norm_g4, w_gu_b, w_down_b, l, 0, tm_s)

        tm_in = min(512, seq)
        (q_p, kvc_p, kvs_p, kvw_p, glu_p, qf_p, kvf_p, gn_p, lf_p, *new_rows_p) = input_projection(
            xp, mod_p, norm_g4, w_in_b, b_in_p, cos_p, sin_p, l, tm_in, seq // tm_in, feature_major_batch=batch,
            depth=depth, stacked=stacked_p)
        stacked_p = tuple(new_rows_p[:5])
        kvs_pc, kvw_pc, kvf_pc = new_rows_p[5:]
        (q_s, kvc_s, kvs_s, kvw_s, glu_s, qf_s, kvf_s, gn_s, lf_s) = input_projection(
            xs, mod_s, norm_g4, w_in_b, b_in_p, cos_s, sin_s, l, tm_s, ms // tm_s)

        kcmp_p = nsa_compress_rows(kvc_p.reshape(mp * 2, NSA_KV), pe2, wk_bd, wv_bd, l, 128)
        o_nsa_p = nsa_attend_prompt(q_p, gn_p, kcmp_p, kvs_p, kvs_pc, kvw_p, kvw_pc, batch, seq)
        cum_p = cumsum_lanes(stacked_p[4][l])
        cum_q = jnp.swapaxes(cum_p, 1, 2).reshape(mp, 8)
        o_fox_p = fox_attend_prompt(qf_p, kvf_p, kvf_pc, cum_p, cum_q, batch, seq)
        y_conv_p = conv_module_prompt(glu_p, conv_dw_w, dw_b3, ln_g3, ln_b3, l, batch, seq)

        kcmp_pool = nsa_compress_pool(cmp_pool, pe_t, wk_t, wv_t, l, 128)
        o_nsa_s = nsa_attend_decode(page_table, _pad_tokens(q_s, n_seq, t_new), _pad_tokens(gn_s, n_seq, t_new),
                                    _pad_tokens(kvs_s, n_seq, t_new), _pad_tokens(kvw_s, n_seq, t_new),
                                    win_state, kcmp_pool, slc_pool, l, past)
        o_nsa_s = o_nsa_s[:, :t_new].reshape(ms, NSA_Q)
        lf_new_t = jnp.swapaxes(lf_s[:, :FOX_HEADS].reshape(n_seq, t_new, FOX_HEADS), 1, 2)
        lf_new_t = jnp.pad(lf_new_t, ((0, 0), (0, 8 - FOX_HEADS), (0, page - t_new)))
        o_fox_s = fox_attend_decode(page_table, _pad_tokens(qf_s, n_seq, t_new), _pad_tokens(kvf_s, n_seq, t_new),
                                    lf_new_t, fox_pool, lf_pool_t, l, past)
        o_fox_s = o_fox_s[:, :t_new].reshape(ms, FOX_W)
        state_t = jnp.swapaxes(state_conv[l], 0, 1)
        glu_t = jnp.swapaxes(glu_s.reshape(n_seq, t_new, CONV_CH), 0, 1)
        y_conv_s = conv_module_decode(state_t, glu_t, conv_dw_w, dw_b3, ln_g3, ln_b3, l)
        y_conv_s = jnp.swapaxes(y_conv_s, 0, 1).reshape(ms, CONV_CH)

        xp = merge_and_project(xp, mod_p, norm_g4, o_nsa_p, y_conv_p, o_fox_p, w_mg_b, b_mg,
                               wb_nsa_b, wb_conv_b, wb_fox_b, w_out_b, l, tm_p)
        xs = merge_and_project(xs, mod_s, norm_g4, o_nsa_s, y_conv_s, o_fox_s, w_mg_b, b_mg,
                               wb_nsa_b, wb_conv_b, wb_fox_b, w_out_b, l, tm_s)
        xp = ffn_half_step(xp, mod_p, 6, norm_g4, w_gu_b, w_down_b, l, 1, tm_p)
        xs = ffn_half_step(xs, mod_s, 6, norm_g4, w_gu_b, w_down_b, l, 1, tm_s)

        kv5 = (2, NSA_KV_HEADS, HEAD_DIM)
        conv_state_p.append(glu_p.reshape(batch, seq, CONV_CH)[:, seq - (CONV_WIDTH - 1):])
        kvw_s5 = kvw_s.reshape((n_seq, t_new) + kv5)
        out_s[0].append(kvc_s.reshape((n_seq, t_new) + kv5))
        out_s[1].append(kvs_s.reshape((n_seq, t_new) + kv5))
        out_s[2].append(kvf_s.reshape(n_seq, t_new, 2, FOX_HEADS, HEAD_DIM))
        out_s[3].append(lf_s[:, :FOX_HEADS].reshape(n_seq, t_new, FOX_HEADS))
        out_s[4].append(kvw_s5)
        out_s[5].append(glu_s.reshape(n_seq, t_new, CONV_CH))

    y_p = final_norm(xp, final_norm_g, tm_p).reshape(batch, seq, D_MODEL)
    y_s = final_norm(xs, final_norm_g, tm_s).reshape(n_seq, t_new, D_MODEL)
    def token_major(a, heads):
        a = a.reshape(depth, batch, 2, heads, HEAD_DIM, a.shape[-1])
        return jnp.transpose(a, (0, 1, 5, 2, 3, 4))

    kvc_all, kvs_all, kvw_all, kvf_all, lf_all = stacked_p
    w_keep = min(NSA_WINDOW, seq)
    out_p = [token_major(kvc_all, NSA_KV_HEADS), token_major(kvs_all, NSA_KV_HEADS),
             token_major(kvf_all, FOX_HEADS), jnp.swapaxes(lf_all[:, :, :FOX_HEADS], 2, 3),
             token_major(kvw_all[:, :, :, seq - w_keep:], NSA_KV_HEADS), jnp.stack(conv_state_p)]
    outs = [y_p, y_s]
    for f in range(6):
        outs.append(out_p[f])
        new_s = jnp.stack(out_s[f])
        if f == 4:
            new_s = jnp.concatenate([state_nsa_win_kv, new_s], axis=2)[:, :, t_new:]
        if f == 5:
            new_s = jnp.concatenate([state_conv, new_s], axis=2)
            new_s = new_s[:, :, new_s.shape[2] - (CONV_WIDTH - 1):]
        outs.append(new_s)
    return tuple(outs)
```

```python
import functools

import jax
import jax.numpy as jnp
from jax import lax
from jax.experimental import pallas as pl
from jax.experimental.pallas import tpu as pltpu

D_MODEL = 1024
HEAD_DIM = 64
NSA_HEADS = 8
NSA_KV_HEADS = 2
NSA_GROUP = NSA_HEADS // NSA_KV_HEADS
NSA_BLOCK = 64
NSA_TOP_K = 16
NSA_WINDOW = 512
FOX_HEADS = 4
CONV_CH = 256
CONV_WIDTH = 31
D_FF = 2816
ROPE_THETA = 10000.0
RMS_EPS = 1e-6
LN_EPS = 1e-5
NEG_INF = -1e30
POS_INF = 1e30
TINY = 1e-30
MASKED_BELOW = -5e29
SCALE = HEAD_DIM ** -0.5
LOG2E = 1.4426950408889634
SCALE_LOG2 = SCALE * LOG2E

NSA_Q = NSA_HEADS * HEAD_DIM
NSA_KV = NSA_KV_HEADS * HEAD_DIM
FOX_W = FOX_HEADS * HEAD_DIM
KV_ROW = 2 * NSA_KV
FOX_ROW = 2 * FOX_W

LANE = 128
SUBLANES = 8
C_Q = 0
C_KVC = C_Q + NSA_Q
C_KVS = C_KVC + KV_ROW
C_KVW = C_KVS + KV_ROW
C_GLU = C_KVW + KV_ROW
C_QF = C_GLU + 2 * CONV_CH
C_KVF = C_QF + FOX_W
C_GN = C_KVF + FOX_ROW
C_LF = C_GN + LANE
D_IN_PAD = C_LF + LANE

TQ_PAD = 8
DECODE_SEQS = 4
VMEM_LIMIT = 56 * 1024 * 1024

F32 = jnp.float32
BF16 = jnp.bfloat16
NT = (((1,), (1,)), ((), ()))


def _params(*sem):
    return pltpu.CompilerParams(dimension_semantics=sem, vmem_limit_bytes=VMEM_LIMIT)


def _dot(a, b):
    return jnp.dot(a, b, preferred_element_type=F32)


def _dot_nt(a, b):
    return lax.dot_general(a, b, NT, preferred_element_type=F32)


def _rms_mod(x, g, shift, scale):
    y = x * lax.rsqrt(jnp.mean(x * x, axis=-1, keepdims=True) + RMS_EPS) * g
    return y * (1.0 + scale) + shift


def _sigmoid(x):
    return 1.0 / (1.0 + jnp.exp(-x))


def _silu(x):
    return x * _sigmoid(x)


def _ada_kernel(c_ref, w_ref, b_ref, o_ref):
    c = _silu(c_ref[...]).astype(BF16)
    o_ref[...] = _dot(c, w_ref[...]) + b_ref[...]


def ada_all_layers(cond, w_ada, b_ada):
    depth, _, n = w_ada.shape
    rows = cond.shape[0]
    tn = n // 8
    return pl.pallas_call(
        _ada_kernel,
        grid=(depth, n // tn),
        in_specs=[pl.BlockSpec((rows, D_MODEL), lambda l, j: (0, 0)),
                  pl.BlockSpec((None, D_MODEL, tn), lambda l, j: (l, 0, j)),
                  pl.BlockSpec((None, 1, tn), lambda l, j: (l, 0, j))],
        out_specs=pl.BlockSpec((None, rows, tn), lambda l, j: (l, 0, j)),
        out_shape=jax.ShapeDtypeStruct((depth, rows, n), F32),
        compiler_params=_params("parallel", "parallel"),
        name="ada",
    )(cond, w_ada, b_ada.reshape(depth, 1, n))


class Mod:
    def __init__(self, arr, tokens_per_cond):
        self.arr = arr
        self.tokens_per_cond = tokens_per_cond

    def spec(self, k, tm):
        if self.arr.ndim == 3:
            per = self.tokens_per_cond // tm
            return pl.BlockSpec((None, 1, D_MODEL), lambda m, *_: (m // per, 0, k))
        return pl.BlockSpec((tm, D_MODEL), lambda m, *_: (m, k))


def _ffn_kernel(x_ref, g_ref, sh_ref, sc_ref, ga_ref, wg_ref, wu_ref, wd_ref, o_ref, h_scr, acc_scr):
    j = pl.program_id(1)

    @pl.when(j == 0)
    def _():
        h_scr[...] = _rms_mod(x_ref[...], g_ref[...], sh_ref[...], sc_ref[...]).astype(BF16)
        acc_scr[...] = jnp.zeros_like(acc_scr)

    h = h_scr[...]
    gate = _dot(h, wg_ref[...])
    up = _dot(h, wu_ref[...])
    a = (_silu(gate) * up).astype(BF16)
    acc_scr[...] += _dot(a, wd_ref[...])

    @pl.when(j == pl.num_programs(1) - 1)
    def _():
        o_ref[...] = x_ref[...] + 0.5 * ga_ref[...] * acc_scr[...]


def ffn_half_step(x, mod, k0, norm_g, w_gu, w_down, layer, idx, tm):
    m = x.shape[0]
    nf = 2
    tf = D_FF // nf
    return pl.pallas_call(
        _ffn_kernel,
        grid=(m // tm, nf),
        in_specs=[pl.BlockSpec((tm, D_MODEL), lambda i, j: (i, 0)),
                  pl.BlockSpec((None, None, 1, D_MODEL), lambda i, j: (layer, 2 * idx, 0, 0)),
                  mod.spec(k0, tm), mod.spec(k0 + 1, tm), mod.spec(k0 + 2, tm),
                  pl.BlockSpec((None, None, D_MODEL, tf), lambda i, j: (layer, idx, 0, j)),
                  pl.BlockSpec((None, None, D_MODEL, tf), lambda i, j: (layer, idx, 0, nf + j)),
                  pl.BlockSpec((None, None, tf, D_MODEL), lambda i, j: (layer, idx, j, 0))],
        out_specs=pl.BlockSpec((tm, D_MODEL), lambda i, j: (i, 0)),
        out_shape=jax.ShapeDtypeStruct((m, D_MODEL), F32),
        scratch_shapes=[pltpu.VMEM((tm, D_MODEL), BF16), pltpu.VMEM((tm, D_MODEL), F32)],
        compiler_params=_params("parallel", "arbitrary"),
        name="ffn",
    )(x, norm_g, mod.arr, mod.arr, mod.arr, w_gu, w_gu, w_down)


def _rope(z, cos, sin):
    n = z.shape[1]
    lane = lax.broadcasted_iota(jnp.int32, z.shape, 1)
    first_half = (lane % HEAD_DIM) < (HEAD_DIM // 2)
    partner = jnp.where(first_half, pltpu.roll(z, n - HEAD_DIM // 2, 1), pltpu.roll(z, HEAD_DIM // 2, 1))
    reps = n // LANE
    if reps > 1:
        cos = jnp.concatenate([cos] * reps, axis=1)
        sin = jnp.concatenate([sin] * reps, axis=1)
    return z * cos + partner * sin


def _inproj_kernel(x_ref, g_ref, sh_ref, sc_ref, cos_ref, sin_ref, w_ref, b_ref, *rest, n_alias=0):
    rest = rest[n_alias:]
    q_ref, kvc_ref, kvs_ref, kvw_ref, glu_ref, qf_ref, kvf_ref, gn_ref, lf_ref = rest[:9]
    feature_major_refs = rest[9:]
    u = _rms_mod(x_ref[...], g_ref[...], sh_ref[...], sc_ref[...]).astype(BF16)
    cos = cos_ref[...]
    sin = sin_ref[...]

    def proj(lo, hi):
        return _dot(u, w_ref[:, lo:hi]) + b_ref[:, lo:hi]

    def as_operand(ref, val, scale=None):
        if ref.dtype == BF16 and scale is not None:
            val = val * scale
        ref[...] = val.astype(ref.dtype)

    as_operand(q_ref, _rope(proj(C_Q, C_KVC), cos, sin), SCALE_LOG2)
    kv_rows = []
    for ref, lo in ((kvc_ref, C_KVC), (kvs_ref, C_KVS), (kvw_ref, C_KVW)):
        z = proj(lo, lo + KV_ROW)
        z = jnp.concatenate([_rope(z[:, :NSA_KV], cos, sin), z[:, NSA_KV:]], axis=1)
        as_operand(ref, z)
        kv_rows.append(z)
    z = proj(C_GLU, C_QF)
    glu_ref[...] = z[:, :CONV_CH] * _sigmoid(z[:, CONV_CH:])
    as_operand(qf_ref, proj(C_QF, C_KVF), SCALE_LOG2)
    kvf = proj(C_KVF, C_GN)
    as_operand(kvf_ref, kvf)
    gn_ref[...] = _sigmoid(proj(C_GN, C_LF))
    z = proj(C_LF, D_IN_PAD)
    lf = jnp.minimum(z, 0.0) - jnp.log(1.0 + jnp.exp(-jnp.abs(z)))
    lf_ref[...] = lf
    if feature_major_refs:
        kvc_t, kvs_t, kvw_t, kvf_t, lf_t, kvs_c, kvw_c, kvf_c = feature_major_refs
        for ref, rows, chunked in ((kvc_t, kv_rows[0], None), (kvs_t, kv_rows[1], kvs_c),
                                   (kvw_t, kv_rows[2], kvw_c), (kvf_t, kvf, kvf_c)):
            rows_t = rows.T
            ref[...] = rows_t
            if chunked is not None:
                for c in range(chunked.shape[0]):
                    chunked[c] = rows_t[:, c * LANE:(c + 1) * LANE].astype(BF16)
        lf_t[...] = lf.T[:lf_t.shape[0], :]


def input_projection(x, mod, norm_g, w_in, b_in, cos, sin, layer, tm, pos_tiles, feature_major_batch=None,
                     depth=1, stacked=()):
    m = x.shape[0]
    widths = (NSA_Q, KV_ROW, KV_ROW, KV_ROW, CONV_CH, FOX_W, FOX_ROW, LANE, LANE)
    out_specs = [pl.BlockSpec((tm, w), lambda i: (i, 0)) for w in widths]
    operand = (0, 2, 3, 5, 6) if feature_major_batch else ()
    out_shape = [jax.ShapeDtypeStruct((m, w), BF16 if k in operand else F32) for k, w in enumerate(widths)]
    if feature_major_batch:
        t = m // feature_major_batch
        for rows in (KV_ROW, KV_ROW, KV_ROW, FOX_ROW, 8):
            out_specs.append(pl.BlockSpec((None, None, rows, tm),
                                          lambda i: (layer, i // pos_tiles, 0, i % pos_tiles)))
            out_shape.append(jax.ShapeDtypeStruct((depth, feature_major_batch, rows, t), F32))
        for rows in (KV_ROW, KV_ROW, FOX_ROW):
            out_specs.append(pl.BlockSpec((None, tm // LANE, rows, LANE),
                                          lambda i: (i // pos_tiles, i % pos_tiles, 0, 0)))
            out_shape.append(jax.ShapeDtypeStruct((feature_major_batch, t // LANE, rows, LANE), BF16))
    n_in = 8
    return pl.pallas_call(
        functools.partial(_inproj_kernel, n_alias=len(stacked)),
        grid=(m // tm,),
        in_specs=[pl.BlockSpec((tm, D_MODEL), lambda i: (i, 0)),
                  pl.BlockSpec((None, None, 1, D_MODEL), lambda i: (layer, 1, 0, 0)),
                  mod.spec(3, tm), mod.spec(4, tm),
                  pl.BlockSpec((tm, LANE), lambda i: (i % pos_tiles, 0)),
                  pl.BlockSpec((tm, LANE), lambda i: (i % pos_tiles, 0)),
                  pl.BlockSpec((None, D_MODEL, D_IN_PAD), lambda i: (layer, 0, 0)),
                  pl.BlockSpec((None, 1, D_IN_PAD), lambda i: (layer, 0, 0))]
        + [pl.BlockSpec(memory_space=pl.ANY)] * len(stacked),
        out_specs=out_specs,
        out_shape=out_shape,
        input_output_aliases={n_in + k: len(widths) + k for k in range(len(stacked))},
        compiler_params=_params("parallel"),
        name="inproj",
    )(x, norm_g, mod.arr, mod.arr, cos, sin, w_in, b_in, *stacked)


def _compress_kernel(x_ref, pe_ref, wk_ref, wv_ref, o_ref):
    nblk = o_ref.shape[0]
    acc_k = jnp.zeros((nblk, NSA_KV), F32)
    acc_v = jnp.zeros((nblk, NSA_KV), F32)
    for l in range(NSA_BLOCK):
        pe = pe_ref[l:l + 1, :]
        k_rows = x_ref[pl.ds(2 * l, nblk, stride=2 * NSA_BLOCK), :] + pe
        v_rows = x_ref[pl.ds(2 * l + 1, nblk, stride=2 * NSA_BLOCK), :] + pe
        acc_k = acc_k + _dot(k_rows.astype(BF16), wk_ref[l])
        acc_v = acc_v + _dot(v_rows.astype(BF16), wv_ref[l])
    o_ref[:, :NSA_KV] = acc_k
    o_ref[:, NSA_KV:] = acc_v


def nsa_compress_rows(rows, pe2, wk_bd, wv_bd, layer, blocks_per_step):
    lead = rows.shape[:-2]
    nblk = rows.shape[-2] // (2 * NSA_BLOCK)
    nb = max(d for d in range(8, blocks_per_step + 1, 8) if nblk % d == 0)
    if lead:
        x_spec = pl.BlockSpec((None, nb * 2 * NSA_BLOCK, NSA_KV), lambda i: (layer, i, 0))
    else:
        x_spec = pl.BlockSpec((nb * 2 * NSA_BLOCK, NSA_KV), lambda i: (i, 0))
    w_spec = pl.BlockSpec((None, NSA_BLOCK, NSA_KV, NSA_KV), lambda i: (layer, 0, 0, 0))
    return pl.pallas_call(
        _compress_kernel,
        grid=(nblk // nb,),
        in_specs=[x_spec, pl.BlockSpec((None, NSA_BLOCK, NSA_KV), lambda i: (layer, 0, 0)), w_spec, w_spec],
        out_specs=pl.BlockSpec((nb, KV_ROW), lambda i: (i, 0)),
        out_shape=jax.ShapeDtypeStruct((nblk, KV_ROW), F32),
        compiler_params=_params("parallel"),
        name="nsa_compress",
    )(rows, pe2, wk_bd, wv_bd)


def _compress_pool_kernel(x_ref, pe_ref, wk_ref, wv_ref, o_ref, *, blocks_per_page):
    pages = o_ref.shape[0]
    page = x_ref.shape[1]
    for c in range(2 * NSA_KV_HEADS):
        w_ref = wk_ref if c < NSA_KV_HEADS else wv_ref
        acc = jnp.zeros((pages, page), F32)
        for d in range(HEAD_DIM):
            rows = x_ref[pl.ds(c * HEAD_DIM + d, pages, stride=KV_ROW), :] + pe_ref[d:d + 1, :]
            acc = acc + _dot(rows.astype(BF16), w_ref[d])
        for j in range(blocks_per_page):
            o_ref[:, j * KV_ROW + c * HEAD_DIM:j * KV_ROW + (c + 1) * HEAD_DIM] = acc[:, j * HEAD_DIM:(j + 1) * HEAD_DIM]


def nsa_compress_pool(pool_t, pe_t, wk_t, wv_t, layer, pages_per_step):
    page = pool_t.shape[-1]
    n_pool = pool_t.shape[1] // KV_ROW
    bpp = page // NSA_BLOCK
    assert page == LANE and bpp * HEAD_DIM == page
    pp = max(d for d in range(8, pages_per_step + 1, 8) if n_pool % d == 0)
    w_spec = pl.BlockSpec((None, HEAD_DIM, page, page), lambda i: (layer, 0, 0, 0))
    out = pl.pallas_call(
        functools.partial(_compress_pool_kernel, blocks_per_page=bpp),
        grid=(n_pool // pp,),
        in_specs=[pl.BlockSpec((None, pp * KV_ROW, page), lambda i: (layer, i, 0)),
                  pl.BlockSpec((None, HEAD_DIM, page), lambda i: (layer, 0, 0)), w_spec, w_spec],
        out_specs=pl.BlockSpec((pp, bpp * KV_ROW), lambda i: (i, 0)),
        out_shape=jax.ShapeDtypeStruct((n_pool, bpp * KV_ROW), F32),
        compiler_params=_params("parallel"),
        name="nsa_compress_pool",
    )(pool_t, pe_t, wk_t, wv_t)
    return out.reshape(n_pool, 1, bpp * KV_ROW)


def _softmax_rows(logits):
    m = jnp.max(logits, axis=-1, keepdims=True)
    p = jnp.where(logits > MASKED_BELOW, jnp.exp(logits - m), 0.0)
    return p / jnp.maximum(jnp.sum(p, axis=-1, keepdims=True), TINY)


STAT_ROWS = 8


def _flash_init(m_scr, l_scr, acc_scr):
    m_scr[...] = jnp.full(m_scr.shape, NEG_INF, F32)
    l_scr[...] = jnp.zeros(l_scr.shape, F32)
    acc_scr[...] = jnp.zeros(acc_scr.shape, F32)


def _flash_stats(s_t, m_scr, l_scr):
    m = m_scr[0:1, :]
    m_new = jnp.maximum(m, jnp.max(s_t, axis=0, keepdims=True))
    alpha = jnp.exp2(m - m_new)
    m_safe = jnp.where(m_new > MASKED_BELOW, m_new, 0.0)
    ps, sums = [], []
    for c in range(0, s_t.shape[1], LANE):
        p = jnp.exp2(s_t[:, c:c + LANE] - m_safe[:, c:c + LANE])
        sums.append(jnp.sum(p, axis=0, keepdims=True))
        ps.append(p.astype(BF16))
    l_scr[0:1, :] = alpha * l_scr[0:1, :] + jnp.concatenate(sums, axis=1)
    m_scr[0:1, :] = m_new
    return jnp.concatenate(ps, axis=1), alpha


def _select_blocks(score, t, nb):
    blk = lax.broadcasted_iota(jnp.int32, score.shape, 1)
    cur = t // NSA_BLOCK
    forced = jnp.where(blk == 0, 1, jnp.where(blk == cur, 1, jnp.where(blk == cur - 1, 1, 0)))
    s = jnp.where(forced == 1, POS_INF, jnp.where(blk > cur, NEG_INF, score))
    rank = jnp.zeros(score.shape, jnp.int32)
    for i in range(nb):
        col = s[:, i:i + 1]
        ahead = jnp.where(col > s, 1, jnp.where(col == s, jnp.where(blk > i, 1, 0), 0))
        rank = rank + ahead
    k_sel = min(NSA_TOP_K, nb)
    return jnp.where(rank < k_sel, jnp.where(blk < nb, 1.0, 0.0), 0.0)


def _expand_blocks(sel, k0, kc):
    width = sel.shape[1]
    kblk = (k0 + lax.broadcasted_iota(jnp.int32, (width, kc), 1)) // NSA_BLOCK
    e = jnp.where(kblk == lax.broadcasted_iota(jnp.int32, (width, kc), 0), 1.0, 0.0).astype(BF16)
    return _dot(sel.astype(BF16), e)


def _select_blocks_t(score_t, t_row, nb):
    blk = lax.broadcasted_iota(jnp.int32, score_t.shape, 0)
    cur = t_row // NSA_BLOCK
    forced = jnp.where(blk == 0, 1, jnp.where(blk == cur, 1, jnp.where(blk == cur - 1, 1, 0)))
    s = jnp.where(forced == 1, POS_INF, jnp.where(blk > cur, NEG_INF, score_t))
    rank = jnp.zeros(score_t.shape, jnp.int32)
    for i in range(nb):
        other = s[i:i + 1, :]
        rank = rank + jnp.where(other > s, 1, jnp.where(other == s, jnp.where(blk > i, 1, 0), 0))
    return jnp.where(rank < min(NSA_TOP_K, nb), 1.0, 0.0)


def _nsa_prompt_kernel(q_ref, gate_ref, kcmp_ref, kvs_ref, kvs_t_ref, kvw_ref, kvw_t_ref, o_ref,
                       qbd_scr, expand_scr, bias_scr, m_scr, l_scr, acc_scr, out_scr, *, tq):
    i = pl.program_id(1)
    t0 = i * tq
    t_row = t0 + lax.broadcasted_iota(jnp.int32, (1, tq), 1)
    nc = kcmp_ref.shape[0]
    lane = lax.broadcasted_iota(jnp.int32, (tq, LANE), 1)
    low_half = lane < HEAD_DIM

    pieces = []
    for h in range(NSA_HEADS):
        g = h // NSA_GROUP
        pair = q_ref[:, (h // 2) * LANE:(h // 2 + 1) * LANE].astype(F32)
        if h % 2 != g:
            pair = pltpu.roll(pair, HEAD_DIM, 1)
        pieces.append(jnp.where(low_half, pair, 0.0) if g == 0 else jnp.where(low_half, 0.0, pair))
    qbd_scr[...] = jnp.concatenate(pieces, axis=0).astype(BF16)
    qbd = qbd_scr[...]
    gates_t = gate_ref[...].T

    kcm = kcmp_ref[:, :NSA_KV].astype(BF16)
    vcm_t = jnp.concatenate([kcmp_ref[:, NSA_KV:], jnp.zeros((LANE - nc, NSA_KV), F32)], axis=0).T
    blk_t = lax.broadcasted_iota(jnp.int32, (nc, 1), 0)
    cmp_bias_t = jnp.where((blk_t + 1) * NSA_BLOCK - 1 <= t_row, 0.0, NEG_INF)
    lc_t = _dot_nt(kcm, qbd) + jnp.concatenate([cmp_bias_t] * NSA_HEADS, axis=1)
    m_t = jnp.max(lc_t, axis=0, keepdims=True)
    p_t = jnp.where(lc_t > MASKED_BELOW, jnp.exp2(lc_t - m_t), 0.0)
    pc_t = p_t / jnp.maximum(jnp.sum(p_t, axis=0, keepdims=True), TINY)
    o_cmp_t = _dot(vcm_t[:, :nc].astype(BF16), pc_t.astype(BF16))
    for h in range(NSA_HEADS):
        g = h // NSA_GROUP
        out_scr[h * HEAD_DIM:(h + 1) * HEAD_DIM, :] = (
            gates_t[3 * h:3 * h + 1, :] * o_cmp_t[g * HEAD_DIM:(g + 1) * HEAD_DIM, h * tq:(h + 1) * tq])

    kc = LANE
    qk = tq // kc
    r_key = lax.broadcasted_iota(jnp.int32, (kc, tq), 0)
    c_qry = lax.broadcasted_iota(jnp.int32, (kc, tq), 1)
    sels = []
    for g in range(NSA_KV_HEADS):
        base = g * NSA_GROUP * tq
        score_t = (pc_t[:, base:base + tq] + pc_t[:, base + tq:base + 2 * tq]
                   + pc_t[:, base + 2 * tq:base + 3 * tq] + pc_t[:, base + 3 * tq:base + 4 * tq])
        sel_t = _select_blocks_t(score_t, t_row, nc)
        sels.append(jnp.concatenate([sel_t, jnp.zeros((LANE - nc, tq), F32)], axis=0))
    sel_t = jnp.concatenate(sels, axis=1).astype(BF16)

    n_chunks = bias_scr.shape[1]
    span = min(4, n_chunks)

    @pl.when(i == 0)
    def _():
        kblk = lax.broadcasted_iota(jnp.int32, expand_scr.shape, 0) // NSA_BLOCK
        expand_scr[...] = jnp.where(kblk == lax.broadcasted_iota(jnp.int32, expand_scr.shape, 1),
                                    1.0, 0.0).astype(BF16)

    pair = min(2, n_chunks)
    n_pairs = ((i + 1) * qk - 1) // pair + 1
    t_rows = jnp.concatenate([t_row] * NSA_KV_HEADS, axis=1)
    for k in range(n_chunks // span):
        @pl.when(n_pairs * pair > k * span)
        def _(k=k):
            first = k * span * kc
            ok = _dot(expand_scr[first:first + span * kc, :], sel_t)
            kpos = first + lax.broadcasted_iota(jnp.int32, (span * kc, 1), 0)
            bias = jnp.where(kpos <= t_rows, jnp.where(ok > 0.5, 0.0, NEG_INF), NEG_INF)
            for jj in range(span):
                for g in range(NSA_KV_HEADS):
                    bias_scr[g, k * span + jj] = bias[jj * kc:(jj + 1) * kc, g * tq:(g + 1) * tq]

    group_w = NSA_GROUP * tq

    def attend_chunk(keys, vals_t, bias_per_group):
        s_t = _dot_nt(keys, qbd_scr[...])
        if bias_per_group is not None:
            s_t = s_t + jnp.concatenate([b for b in bias_per_group for _ in range(NSA_GROUP)], axis=1)
        p, alpha = _flash_stats(s_t, m_scr, l_scr)
        for g in range(NSA_KV_HEADS):
            cols = slice(g * group_w, (g + 1) * group_w)
            rows = slice(NSA_KV + g * HEAD_DIM, NSA_KV + (g + 1) * HEAD_DIM)
            v_t = jnp.concatenate([v[rows, :] for v in vals_t], axis=1)
            acc_scr[g] = alpha[:, cols] * acc_scr[g] + _dot(v_t, p[:, cols])

    def finish_branch(branch):
        inv_l = 1.0 / jnp.maximum(l_scr[0:1, :], TINY)
        for h in range(NSA_HEADS):
            g, r = divmod(h, NSA_GROUP)
            rs = slice(h * HEAD_DIM, (h + 1) * HEAD_DIM)
            o_t = acc_scr[g, :, r * tq:(r + 1) * tq] * inv_l[:, h * tq:(h + 1) * tq]
            out_scr[rs, :] = out_scr[rs, :] + gates_t[3 * h + branch:3 * h + branch + 1, :] * o_t

    _flash_init(m_scr, l_scr, acc_scr)

    def slc_body(jp, _):
        j = jp * pair
        k0 = pl.multiple_of(j * kc, pair * kc)
        attend_chunk(kvs_ref[pl.ds(k0, pair * kc), :NSA_KV], [kvs_t_ref.at[j + d] for d in range(pair)],
                     [jnp.concatenate([bias_scr[g, j + d] for d in range(pair)], axis=0)
                      for g in range(NSA_KV_HEADS)])
        return 0

    lax.fori_loop(0, n_pairs, slc_body, 0)
    finish_branch(1)

    _flash_init(m_scr, l_scr, acc_scr)
    n_win = NSA_WINDOW // kc + qk

    def win_pair(e_first, j_first):
        k0 = pl.multiple_of(j_first * kc, kc)
        biases, masked = [], False
        for e in range(e_first, e_first + pair):
            rel = e * kc - NSA_WINDOW + r_key - c_qry
            all_visible = e * kc >= tq and (e + 1) * kc <= NSA_WINDOW + 1
            masked = masked or not all_visible
            biases.append(jnp.where(rel <= 0, jnp.where(rel > -NSA_WINDOW, 0.0, NEG_INF), NEG_INF))
        attend_chunk(kvw_ref[pl.ds(k0, pair * kc), :NSA_KV], [kvw_t_ref.at[j_first + d] for d in range(pair)],
                     [jnp.concatenate(biases, axis=0)] * NSA_KV_HEADS if masked else None)

    for e in range(0, n_win, pair):
        jc = i * qk - NSA_WINDOW // kc + e
        pl.when(jc >= 0)(functools.partial(win_pair, e, jc))

    finish_branch(2)
    o_ref[...] = out_scr[...].T


def nsa_attend_prompt(q, gates, kcmp, kvs, kvs_t, kvw, kvw_t, batch, seq, tq=256):
    tq = min(tq, seq)
    nq = seq // tq
    nc = seq // NSA_BLOCK
    n_tiles = seq // LANE
    assert nc <= LANE and tq % (2 * LANE) == 0 and NSA_WINDOW % (2 * LANE) == 0 and n_tiles % min(4, n_tiles) == 0
    kern = functools.partial(_nsa_prompt_kernel, tq=tq)
    return pl.pallas_call(
        kern,
        grid=(batch, nq),
        in_specs=[pl.BlockSpec((tq, NSA_Q), lambda b, i: (b * nq + i, 0)),
                  pl.BlockSpec((tq, LANE), lambda b, i: (b * nq + i, 0)),
                  pl.BlockSpec((nc, KV_ROW), lambda b, i: (b, 0)),
                  pl.BlockSpec((seq, KV_ROW), lambda b, i: (b, 0)),
                  pl.BlockSpec((None, n_tiles, KV_ROW, LANE), lambda b, i: (b, 0, 0, 0)),
                  pl.BlockSpec((seq, KV_ROW), lambda b, i: (b, 0)),
                  pl.BlockSpec((None, n_tiles, KV_ROW, LANE), lambda b, i: (b, 0, 0, 0))],
        out_specs=pl.BlockSpec((tq, NSA_Q), lambda b, i: (b * nq + i, 0)),
        out_shape=jax.ShapeDtypeStruct((batch * seq, NSA_Q), F32),
        scratch_shapes=[pltpu.VMEM((NSA_HEADS * tq, LANE), BF16),
                        pltpu.VMEM((seq, LANE), BF16),
                        pltpu.VMEM((NSA_KV_HEADS, n_tiles, LANE, tq), F32),
                        pltpu.VMEM((STAT_ROWS, NSA_HEADS * tq), F32),
                        pltpu.VMEM((STAT_ROWS, NSA_HEADS * tq), F32),
                        pltpu.VMEM((NSA_KV_HEADS, HEAD_DIM, NSA_GROUP * tq), F32),
                        pltpu.VMEM((NSA_Q, tq), F32)],
        compiler_params=_params("parallel", "arbitrary"),
        name="nsa_prompt",
    )(q, gates, kcmp, kvs, kvs_t, kvw, kvw_t)


def _nsa_decode_kernel(pt_ref, q_ref, gate_ref, kvs_new_ref, kvw_new_ref, win_ref, *rest,
                       n_pages, page, past, seqs):
    del pt_ref
    kcmp_refs = rest[:seqs * n_pages]
    kvs_refs = rest[seqs * n_pages:2 * seqs * n_pages]
    o_ref, kcmp_scr = rest[2 * seqs * n_pages:]
    for s in range(seqs):
        _nsa_decode_one(q_ref.at[s], gate_ref.at[s], kvs_new_ref.at[s], kvw_new_ref.at[s], win_ref.at[s],
                        kcmp_refs[s * n_pages:(s + 1) * n_pages], kvs_refs[s * n_pages:(s + 1) * n_pages],
                        o_ref.at[s], kcmp_scr.at[s], n_pages=n_pages, page=page, past=past)


def _nsa_decode_one(q_ref, gate_ref, kvs_new_ref, kvw_new_ref, win_ref, kcmp_refs, kvs_refs, o_ref, kcmp_scr,
                    *, n_pages, page, past):
    tq = TQ_PAD
    rows = NSA_HEADS * tq
    blocks_per_page = page // NSA_BLOCK
    nc = n_pages * blocks_per_page
    nb = nc + 1
    for p in range(n_pages):
        for j in range(blocks_per_page):
            kcmp_scr[p * blocks_per_page + j:p * blocks_per_page + j + 1, :] = (
                kcmp_refs[p][:, j * KV_ROW:(j + 1) * KV_ROW])

    row = lax.broadcasted_iota(jnp.int32, (rows, 1), 0)
    t_rows = past + row % tq
    t1 = past + lax.broadcasted_iota(jnp.int32, (tq, 1), 0)
    lane = lax.broadcasted_iota(jnp.int32, (rows, LANE), 1)
    own_group = (lane // HEAD_DIM) == (row // (NSA_GROUP * tq))

    q = q_ref[...] * SCALE
    pieces = []
    for h in range(NSA_HEADS):
        qh = q[:, h * HEAD_DIM:(h + 1) * HEAD_DIM]
        zero = jnp.zeros_like(qh)
        pieces.append(jnp.concatenate([qh, zero] if h < NSA_GROUP else [zero, qh], axis=1))
    qbd = jnp.concatenate(pieces, axis=0).astype(BF16)

    def own_half(o):
        o = jnp.where(own_group, o, 0.0)
        return o[:, :HEAD_DIM] + o[:, HEAD_DIM:]

    kcm = kcmp_scr[:, :NSA_KV].astype(BF16)
    vcm = kcmp_scr[:, NSA_KV:].astype(BF16)
    blk = lax.broadcasted_iota(jnp.int32, (1, nc), 1)
    lc = _dot_nt(qbd, kcm) + jnp.where((blk + 1) * NSA_BLOCK - 1 <= t_rows, 0.0, NEG_INF)
    pc = _softmax_rows(lc)
    o_cmp = own_half(_dot(pc.astype(BF16), vcm))

    gates = gate_ref[...]
    new_k_pos = past + lax.broadcasted_iota(jnp.int32, (1, page), 1)
    pad = jnp.zeros((page - tq, KV_ROW), F32)
    kvs_new = jnp.concatenate([kvs_new_ref[...], pad], axis=0).astype(BF16)
    kvw_new = jnp.concatenate([kvw_new_ref[...], pad], axis=0).astype(BF16)

    n_keys = (n_pages + 1) * page
    key_pos = lax.broadcasted_iota(jnp.int32, (1, n_keys), 1)
    sels = []
    for g in range(NSA_KV_HEADS):
        pcg = pc[g * NSA_GROUP * tq:(g + 1) * NSA_GROUP * tq]
        score = pcg[0:tq] + pcg[tq:2 * tq] + pcg[2 * tq:3 * tq] + pcg[3 * tq:4 * tq]
        score = jnp.concatenate([score, jnp.zeros((tq, LANE - nc), F32)], axis=1)
        sels.append(_select_blocks(score, t1, nb))
    ok = _expand_blocks(jnp.concatenate(sels, axis=0), 0, n_keys)
    t2 = jnp.concatenate([t1] * NSA_KV_HEADS, axis=0)
    bias2 = jnp.where(key_pos <= t2, jnp.where(ok > 0.5, 0.0, NEG_INF), NEG_INF)
    bias = jnp.concatenate(
        [bias2[g * tq:(g + 1) * tq] for g in range(NSA_KV_HEADS) for _ in range(NSA_GROUP)], axis=0)
    s = jnp.concatenate([_dot(qbd, kvs_refs[p][:NSA_KV, :].astype(BF16)) for p in range(n_pages)]
                        + [_dot_nt(qbd, kvs_new[:, :NSA_KV])], axis=1) + bias
    m = jnp.max(s, axis=-1, keepdims=True)
    pr = jnp.where(s > MASKED_BELOW, jnp.exp(s - m), 0.0)
    den = jnp.maximum(jnp.sum(pr, axis=-1, keepdims=True), TINY)
    pr = pr.astype(BF16)
    acc = _dot(pr[:, n_pages * page:], kvs_new[:, NSA_KV:])
    for p in range(n_pages):
        acc = acc + _dot_nt(pr[:, p * page:(p + 1) * page], kvs_refs[p][NSA_KV:, :].astype(BF16))
    o_slc = own_half(acc / den)

    w_buf = win_ref.shape[1]
    wpos = jnp.concatenate([past - w_buf + lax.broadcasted_iota(jnp.int32, (1, w_buf), 1), new_k_pos], axis=1)
    ok = jnp.where(wpos <= t_rows, jnp.where(wpos > t_rows - NSA_WINDOW, jnp.where(wpos >= 0, 1.0, 0.0), 0.0), 0.0)
    s = jnp.concatenate([_dot(qbd, win_ref[:NSA_KV, :].astype(BF16)), _dot_nt(qbd, kvw_new[:, :NSA_KV])], axis=1)
    s = s + jnp.where(ok > 0.5, 0.0, NEG_INF)
    m = jnp.max(s, axis=-1, keepdims=True)
    pr = jnp.where(s > MASKED_BELOW, jnp.exp(s - m), 0.0)
    den = jnp.maximum(jnp.sum(pr, axis=-1, keepdims=True), TINY)
    pr = pr.astype(BF16)
    acc = _dot_nt(pr[:, :w_buf], win_ref[NSA_KV:, :].astype(BF16)) + _dot(pr[:, w_buf:], kvw_new[:, NSA_KV:])
    o_win = own_half(acc / den)

    for h in range(NSA_HEADS):
        rs = slice(h * tq, (h + 1) * tq)
        o = (gates[:, 3 * h:3 * h + 1] * o_cmp[rs] + gates[:, 3 * h + 1:3 * h + 2] * o_slc[rs]
             + gates[:, 3 * h + 2:3 * h + 3] * o_win[rs])
        o_ref[:, h * HEAD_DIM:(h + 1) * HEAD_DIM] = o


def nsa_attend_decode(page_table, q, gates, kvs_new, kvw_new, win_state, kcmp_pool, slc_pool, layer, past):
    n_seq, n_pages = page_table.shape
    page = slc_pool.shape[3]
    bpp = page // NSA_BLOCK
    w_buf = win_state.shape[3]

    seqs = DECODE_SEQS if n_seq % DECODE_SEQS == 0 else 1

    def seq_spec(width):
        return pl.BlockSpec((seqs, TQ_PAD, width), lambda b, pt: (b, 0, 0))

    def kcmp_spec(s, p):
        return pl.BlockSpec((None, 1, bpp * KV_ROW), lambda b, pt: (pt[b * seqs + s, p], 0, 0))

    def page_spec(s, p):
        return pl.BlockSpec((None, None, KV_ROW, page), lambda b, pt: (layer, pt[b * seqs + s, p], 0, 0))

    assert page % NSA_BLOCK == 0 and TQ_PAD <= NSA_BLOCK and n_pages * bpp + 1 <= LANE
    kern = functools.partial(_nsa_decode_kernel, n_pages=n_pages, page=page, past=past, seqs=seqs)
    grid_spec = pltpu.PrefetchScalarGridSpec(
        num_scalar_prefetch=1,
        grid=(n_seq // seqs,),
        in_specs=([seq_spec(NSA_Q), seq_spec(LANE), seq_spec(KV_ROW), seq_spec(KV_ROW),
                   pl.BlockSpec((None, seqs, KV_ROW, w_buf), lambda b, pt: (layer, b, 0, 0))]
                  + [kcmp_spec(s, p) for s in range(seqs) for p in range(n_pages)]
                  + [page_spec(s, p) for s in range(seqs) for p in range(n_pages)]),
        out_specs=seq_spec(NSA_Q),
        scratch_shapes=[pltpu.VMEM((seqs, n_pages * bpp, KV_ROW), F32)],
    )
    n_refs = seqs * n_pages
    return pl.pallas_call(
        kern,
        grid_spec=grid_spec,
        out_shape=jax.ShapeDtypeStruct((n_seq, TQ_PAD, NSA_Q), F32),
        compiler_params=_params("arbitrary"),
        name="nsa_decode",
    )(page_table, q, gates, kvs_new, kvw_new, win_state, *([kcmp_pool] * n_refs), *([slc_pool] * n_refs))


def _lane_cumsum(x):
    n = x.shape[-1]
    lane = lax.broadcasted_iota(jnp.int32, x.shape, x.ndim - 1)
    s = 1
    while s < n:
        x = x + jnp.where(lane >= s, pltpu.roll(x, s, x.ndim - 1), 0.0)
        s *= 2
    return x


def _cumsum_kernel(x_ref, o_ref):
    o_ref[...] = _lane_cumsum(x_ref[...]) * LOG2E


def cumsum_lanes(x):
    b, r, t = x.shape
    return pl.pallas_call(
        _cumsum_kernel,
        grid=(b,),
        in_specs=[pl.BlockSpec((None, r, t), lambda i: (i, 0, 0))],
        out_specs=pl.BlockSpec((None, r, t), lambda i: (i, 0, 0)),
        out_shape=jax.ShapeDtypeStruct(x.shape, F32),
        compiler_params=_params("parallel"),
        name="logf_cumsum",
    )(x)


def _fox_prompt_kernel(q_ref, kv_ref, kv_t_ref, cum_ref, cumq_ref, o_ref,
                       qbd_scr, m_scr, l_scr, acc_scr, out_scr, *, tq):
    i = pl.program_id(1)
    lane = lax.broadcasted_iota(jnp.int32, (tq, FOX_W), 1)
    q = q_ref[...].astype(F32)
    qbd_scr[...] = jnp.concatenate(
        [jnp.where(lane // HEAD_DIM == h, q, 0.0) for h in range(FOX_HEADS)], axis=0).astype(BF16)
    cum_q = cum_ref[i]
    cum_q = jnp.concatenate([cum_q[h:h + 1, :] for h in range(FOX_HEADS)], axis=1)
    r_key = lax.broadcasted_iota(jnp.int32, (tq, tq), 0)
    c_qry = lax.broadcasted_iota(jnp.int32, (tq, tq), 1)
    causal_bias = jnp.concatenate([jnp.where(r_key <= c_qry, 0.0, NEG_INF)] * FOX_HEADS, axis=1)
    _flash_init(m_scr, l_scr, acc_scr)

    per_chunk = tq // LANE

    def chunk(j, causal):
        k0 = pl.multiple_of(j * tq, tq)
        cum_k = cumq_ref[pl.ds(k0, tq), :]
        cum_k = jnp.concatenate(
            [jnp.broadcast_to(cum_k[:, h:h + 1], (tq, tq)) for h in range(FOX_HEADS)], axis=1)
        s_t = _dot_nt(kv_ref[pl.ds(k0, tq), :FOX_W], qbd_scr[...]) + (cum_q - cum_k)
        if causal:
            s_t = s_t + causal_bias
        p, alpha = _flash_stats(s_t, m_scr, l_scr)
        for h in range(FOX_HEADS):
            cols = slice(h * tq, (h + 1) * tq)
            rows = slice(FOX_W + h * HEAD_DIM, FOX_W + (h + 1) * HEAD_DIM)
            v_t = jnp.concatenate([kv_t_ref[j * per_chunk + d, rows, :] for d in range(per_chunk)], axis=1)
            acc_scr[h] = alpha[:, cols] * acc_scr[h] + _dot(v_t, p[:, cols])

    def body(j, _):
        chunk(j, False)
        return 0

    lax.fori_loop(0, i, body, 0)
    chunk(i, True)
    inv_l = 1.0 / jnp.maximum(l_scr[0:1, :], TINY)
    for h in range(FOX_HEADS):
        out_scr[h * HEAD_DIM:(h + 1) * HEAD_DIM, :] = acc_scr[h] * inv_l[:, h * tq:(h + 1) * tq]
    o_ref[...] = out_scr[...].T


def fox_attend_prompt(qf, kvf, kvf_t, cum, cum_q, batch, seq, tq=256):
    tq = min(tq, seq)
    nq = seq // tq
    assert tq % LANE == 0
    kern = functools.partial(_fox_prompt_kernel, tq=tq)
    return pl.pallas_call(
        kern,
        grid=(batch, nq),
        in_specs=[pl.BlockSpec((tq, FOX_W), lambda b, i: (b * nq + i, 0)),
                  pl.BlockSpec((seq, FOX_ROW), lambda b, i: (b, 0)),
                  pl.BlockSpec((None, seq // LANE, FOX_ROW, LANE), lambda b, i: (b, 0, 0, 0)),
                  pl.BlockSpec((None, nq, 8, tq), lambda b, i: (b, 0, 0, 0)),
                  pl.BlockSpec((seq, 8), lambda b, i: (b, 0))],
        out_specs=pl.BlockSpec((tq, FOX_W), lambda b, i: (b * nq + i, 0)),
        out_shape=jax.ShapeDtypeStruct((batch * seq, FOX_W), F32),
        scratch_shapes=[pltpu.VMEM((FOX_HEADS * tq, FOX_W), BF16),
                        pltpu.VMEM((STAT_ROWS, FOX_HEADS * tq), F32),
                        pltpu.VMEM((STAT_ROWS, FOX_HEADS * tq), F32),
                        pltpu.VMEM((FOX_HEADS, HEAD_DIM, tq), F32),
                        pltpu.VMEM((FOX_W, tq), F32)],
        compiler_params=_params("parallel", "arbitrary"),
        name="fox_prompt",
    )(qf, kvf, kvf_t, jnp.swapaxes(cum.reshape(batch, 8, nq, tq), 1, 2), cum_q)


def _fox_decode_kernel(pt_ref, q_ref, kv_new_ref, lf_new_ref, *rest, n_pages, page, past, seqs):
    del pt_ref
    kv_refs = rest[:seqs * n_pages]
    lf_refs = rest[seqs * n_pages:2 * seqs * n_pages]
    o_ref = rest[2 * seqs * n_pages]
    for s in range(seqs):
        _fox_decode_one(q_ref.at[s], kv_new_ref.at[s], lf_new_ref.at[s], kv_refs[s * n_pages:(s + 1) * n_pages],
                        lf_refs[s * n_pages:(s + 1) * n_pages], o_ref.at[s], page=page, past=past)


def _fox_decode_one(q_ref, kv_new_ref, lf_new_ref, kv_refs, lf_refs, o_ref, *, page, past):
    n_pages = len(kv_refs)
    tq = TQ_PAD
    rows = FOX_HEADS * tq
    row = lax.broadcasted_iota(jnp.int32, (rows, 1), 0)
    lane = lax.broadcasted_iota(jnp.int32, (rows, FOX_W), 1)
    own_head = (lane // HEAD_DIM) == (row // tq)
    t_rows = past + row % tq

    q = q_ref[...] * SCALE
    qbd = jnp.where(own_head, jnp.concatenate([q] * FOX_HEADS, axis=0), 0.0).astype(BF16)

    def head_rows(a):
        return jnp.concatenate(
            [jnp.broadcast_to(a[h:h + 1, :], (tq, a.shape[1])) for h in range(FOX_HEADS)], axis=0)

    lf_past = jnp.concatenate([r[...] for r in lf_refs], axis=1)
    cum_past = _lane_cumsum(lf_past)
    cum_new = cum_past[:, past - 1:past] + _lane_cumsum(lf_new_ref[...])
    ck_past = head_rows(cum_past)
    ck_new = head_rows(cum_new)
    new_lane = lax.broadcasted_iota(jnp.int32, (rows, page), 1)
    cq = jnp.sum(jnp.where(new_lane == row % tq, ck_new, 0.0), axis=-1, keepdims=True)

    kv_new = jnp.concatenate([kv_new_ref[...], jnp.zeros((page - tq, FOX_ROW), F32)], axis=0).astype(BF16)
    key_pos = lax.broadcasted_iota(jnp.int32, (1, past + page), 1)
    s = jnp.concatenate([_dot(qbd, kv_refs[p][:FOX_W, :].astype(BF16)) for p in range(n_pages)]
                        + [_dot_nt(qbd, kv_new[:, :FOX_W])], axis=1)
    s = s + (cq - jnp.concatenate([ck_past, ck_new], axis=1)) + jnp.where(key_pos <= t_rows, 0.0, NEG_INF)
    m = jnp.max(s, axis=-1, keepdims=True)
    pr = jnp.where(s > MASKED_BELOW, jnp.exp(s - m), 0.0)
    den = jnp.maximum(jnp.sum(pr, axis=-1, keepdims=True), TINY)
    pr = pr.astype(BF16)
    acc = _dot(pr[:, past:], kv_new[:, FOX_W:])
    for p in range(n_pages):
        acc = acc + _dot_nt(pr[:, p * page:(p + 1) * page], kv_refs[p][FOX_W:, :].astype(BF16))
    o = jnp.where(own_head, acc / den, 0.0)
    o_ref[...] = o[0:tq] + o[tq:2 * tq] + o[2 * tq:3 * tq] + o[3 * tq:4 * tq]


def fox_attend_decode(page_table, qf, kvf_new, lf_new_t, fox_pool, lf_pool_t, layer, past):
    n_seq, n_pages = page_table.shape
    page = fox_pool.shape[3]

    seqs = DECODE_SEQS if n_seq % DECODE_SEQS == 0 else 1

    def seq_spec(r, width):
        return pl.BlockSpec((seqs, r, width), lambda b, pt: (b, 0, 0))

    def page_spec(shape, s, p):
        return pl.BlockSpec((None, None) + shape, lambda b, pt: (layer, pt[b * seqs + s, p], 0, 0))

    assert TQ_PAD <= page
    kern = functools.partial(_fox_decode_kernel, n_pages=n_pages, page=page, past=past, seqs=seqs)
    grid_spec = pltpu.PrefetchScalarGridSpec(
        num_scalar_prefetch=1,
        grid=(n_seq // seqs,),
        in_specs=([seq_spec(TQ_PAD, FOX_W), seq_spec(TQ_PAD, FOX_ROW), seq_spec(8, page)]
                  + [page_spec((FOX_ROW, page), s, p) for s in range(seqs) for p in range(n_pages)]
                  + [page_spec((8, page), s, p) for s in range(seqs) for p in range(n_pages)]),
        out_specs=seq_spec(TQ_PAD, FOX_W),
    )
    n_refs = seqs * n_pages
    return pl.pallas_call(
        kern,
        grid_spec=grid_spec,
        out_shape=jax.ShapeDtypeStruct((n_seq, TQ_PAD, FOX_W), F32),
        compiler_params=_params("arbitrary"),
        name="fox_decode",
    )(page_table, qf, kvf_new, lf_new_t, *([fox_pool] * n_refs), *([lf_pool_t] * n_refs))


CONV_PAD = 32
CONV_SUB = 64


def _ln_silu(y, g, b):
    yc = y - jnp.mean(y, axis=-1, keepdims=True)
    var = jnp.mean(yc * yc, axis=-1, keepdims=True)
    return _silu(yc * lax.rsqrt(var + LN_EPS) * g + b)


def _conv_prompt_kernel(prev_ref, cur_ref, w_ref, b_ref, g_ref, bb_ref, o_ref, ctx_scr, shift_scr, *, tc):
    k = pl.program_id(1)
    ctx_scr[0:CONV_PAD, :] = jnp.where(k > 0, prev_ref[...], 0.0)
    ctx_scr[CONV_PAD:, :] = cur_ref[...]
    lead = CONV_PAD - (CONV_WIDTH - 1)
    n_fill = tc + CONV_PAD - SUBLANES
    for r in range(1, SUBLANES):
        shift_scr[r - 1, 0:n_fill, :] = ctx_scr[pl.ds(r, n_fill), :]
    for sub in range(tc // CONV_SUB):
        acc = jnp.zeros((CONV_SUB, CONV_CH), F32) + b_ref[...]
        for w in range(CONV_WIDTH):
            q, r = divmod(lead + w, SUBLANES)
            src = ctx_scr if r == 0 else shift_scr.at[r - 1]
            acc = acc + src[pl.ds(sub * CONV_SUB + q * SUBLANES, CONV_SUB), :] * w_ref[w:w + 1, :]
        o_ref[sub * CONV_SUB:(sub + 1) * CONV_SUB, :] = _ln_silu(acc, g_ref[...], bb_ref[...])


def conv_module_prompt(glu, dw_w, dw_b, ln_g, ln_b, layer, batch, seq, tc=256):
    tc = min(tc, seq)
    nt = seq // tc
    per = tc // CONV_PAD

    def vec_spec():
        return pl.BlockSpec((None, 1, CONV_CH), lambda b, k: (layer, 0, 0))

    return pl.pallas_call(
        functools.partial(_conv_prompt_kernel, tc=tc),
        grid=(batch, nt),
        in_specs=[pl.BlockSpec((CONV_PAD, CONV_CH), lambda b, k: (jnp.maximum((b * nt + k) * per - 1, 0), 0)),
                  pl.BlockSpec((tc, CONV_CH), lambda b, k: (b * nt + k, 0)),
                  pl.BlockSpec((None, CONV_WIDTH, CONV_CH), lambda b, k: (layer, 0, 0)),
                  vec_spec(), vec_spec(), vec_spec()],
        out_specs=pl.BlockSpec((tc, CONV_CH), lambda b, k: (b * nt + k, 0)),
        out_shape=jax.ShapeDtypeStruct((batch * seq, CONV_CH), F32),
        scratch_shapes=[pltpu.VMEM((CONV_PAD + tc, CONV_CH), F32),
                        pltpu.VMEM((SUBLANES - 1, CONV_PAD + tc, CONV_CH), F32)],
        compiler_params=_params("parallel", "arbitrary"),
        name="conv_prompt",
    )(glu, glu, dw_w, dw_b, ln_g, ln_b)


def _conv_decode_kernel(state_ref, glu_ref, w_ref, b_ref, g_ref, bb_ref, o_ref):
    n_state = state_ref.shape[0]
    t_new = glu_ref.shape[0]
    ctx = [state_ref[i] for i in range(n_state)] + [glu_ref[i] for i in range(t_new)]
    for i in range(t_new):
        acc = jnp.zeros(ctx[0].shape, F32) + b_ref[...]
        for w in range(CONV_WIDTH):
            acc = acc + ctx[i + w] * w_ref[w:w + 1, :]
        o_ref[i] = _ln_silu(acc, g_ref[...], bb_ref[...])


def conv_module_decode(state_t, glu_t, dw_w, dw_b, ln_g, ln_b, layer):
    t_new, n_seq, _ = glu_t.shape

    def vec_spec():
        return pl.BlockSpec((None, 1, CONV_CH), lambda i: (layer, 0, 0))

    return pl.pallas_call(
        _conv_decode_kernel,
        grid=(1,),
        in_specs=[pl.BlockSpec(state_t.shape, lambda i: (0, 0, 0)),
                  pl.BlockSpec(glu_t.shape, lambda i: (0, 0, 0)),
                  pl.BlockSpec((None, CONV_WIDTH, CONV_CH), lambda i: (layer, 0, 0)),
                  vec_spec(), vec_spec(), vec_spec()],
        out_specs=pl.BlockSpec((t_new, n_seq, CONV_CH), lambda i: (0, 0, 0)),
        out_shape=jax.ShapeDtypeStruct((t_new, n_seq, CONV_CH), F32),
        compiler_params=_params("arbitrary"),
        name="conv_decode",
    )(state_t, glu_t, dw_w, dw_b, ln_g, ln_b)


def _merge_kernel(x_ref, g_ref, sh_ref, sc_ref, ga_ref, on_ref, yc_ref, of_ref,
                  wg_ref, bg_ref, wn_ref, wc_ref, wf_ref, wo_ref, o_ref):
    x = x_ref[...]
    u = _rms_mod(x, g_ref[...], sh_ref[...], sc_ref[...]).astype(BF16)

    def gate(k):
        return _sigmoid(_dot(u, wg_ref[:, k * D_MODEL:(k + 1) * D_MODEL]) + bg_ref[:, k * D_MODEL:(k + 1) * D_MODEL])

    merged = (gate(0) * _dot(on_ref[...].astype(BF16), wn_ref[...])
              + gate(1) * _dot(yc_ref[...].astype(BF16), wc_ref[...])
              + gate(2) * _dot(of_ref[...].astype(BF16), wf_ref[...]))
    o_ref[...] = x + ga_ref[...] * _dot(merged.astype(BF16), wo_ref[...])


def merge_and_project(x, mod, norm_g, o_nsa, y_conv, o_fox, w_mg, b_mg, wb_nsa, wb_conv, wb_fox, w_out, layer, tm):
    m = x.shape[0]

    def row_spec(width):
        return pl.BlockSpec((tm, width), lambda i: (i, 0))

    def w_spec(rows, cols=D_MODEL):
        return pl.BlockSpec((None, rows, cols), lambda i: (layer, 0, 0))

    return pl.pallas_call(
        _merge_kernel,
        grid=(m // tm,),
        in_specs=[row_spec(D_MODEL), pl.BlockSpec((None, None, 1, D_MODEL), lambda i: (layer, 1, 0, 0)),
                  mod.spec(3, tm), mod.spec(4, tm), mod.spec(5, tm),
                  row_spec(NSA_Q), row_spec(CONV_CH), row_spec(FOX_W),
                  w_spec(D_MODEL, 3 * D_MODEL), w_spec(1, 3 * D_MODEL),
                  w_spec(NSA_Q), w_spec(CONV_CH), w_spec(FOX_W), w_spec(D_MODEL)],
        out_specs=row_spec(D_MODEL),
        out_shape=jax.ShapeDtypeStruct((m, D_MODEL), F32),
        compiler_params=_params("parallel"),
        name="merge",
    )(x, norm_g, mod.arr, mod.arr, mod.arr, o_nsa, y_conv, o_fox, w_mg, b_mg, wb_nsa, wb_conv, wb_fox, w_out)


def _final_norm_kernel(x_ref, g_ref, o_ref):
    x = x_ref[...]
    o_ref[...] = x * lax.rsqrt(jnp.mean(x * x, axis=-1, keepdims=True) + RMS_EPS) * g_ref[...]


def final_norm(x, g, tm):
    m = x.shape[0]
    return pl.pallas_call(
        _final_norm_kernel,
        grid=(m // tm,),
        in_specs=[pl.BlockSpec((tm, D_MODEL), lambda i: (i, 0)), pl.BlockSpec((1, D_MODEL), lambda i: (0, 0))],
        out_specs=pl.BlockSpec((tm, D_MODEL), lambda i: (i, 0)),
        out_shape=jax.ShapeDtypeStruct((m, D_MODEL), F32),
        compiler_params=_params("parallel"),
        name="final_norm",
    )(x, g.reshape(1, D_MODEL))


def _rope_tables(pos):
    half = HEAD_DIM // 2
    inv_freq = ROPE_THETA ** (-jnp.arange(half, dtype=F32) / half)
    ang = pos.astype(F32)[:, None] * inv_freq[None, :]
    cos, sin = jnp.cos(ang), jnp.sin(ang)
    cos = jnp.concatenate([cos, cos] * (LANE // HEAD_DIM), axis=1)
    sin = jnp.concatenate([-sin, sin] * (LANE // HEAD_DIM), axis=1)
    return cos, sin


def _aligned_in_proj(w_in, b_in):
    o_gn = NSA_Q + 3 * KV_ROW
    o_glu = o_gn + 3 * NSA_HEADS
    o_qkvf = o_glu + 2 * CONV_CH
    o_lf = o_qkvf + 3 * FOX_W
    o_mg = o_lf + FOX_HEADS

    def cols(a):
        pad_gn = jnp.zeros(a.shape[:-1] + (LANE - 3 * NSA_HEADS,), a.dtype)
        pad_lf = jnp.zeros(a.shape[:-1] + (LANE - FOX_HEADS,), a.dtype)
        return jnp.concatenate([a[..., :o_gn], a[..., o_glu:o_lf],
                                a[..., o_gn:o_glu], pad_gn, a[..., o_lf:o_mg], pad_lf], axis=-1)

    return (cols(w_in).astype(BF16), cols(b_in)[:, None, :],
            w_in[..., o_mg:].astype(BF16), b_in[:, None, o_mg:])


def _compress_weights(pe, wk, wv):
    pe2 = jnp.concatenate([pe, pe], axis=-1)

    def block_diag(w):
        z = jnp.zeros_like(w)
        return jnp.concatenate([jnp.concatenate([w, z], axis=-1), jnp.concatenate([z, w], axis=-1)],
                               axis=-2).astype(BF16)

    return pe2, block_diag(wk), block_diag(wv)


def _pad_tokens(a, n_seq, t_new):
    a = a.reshape(n_seq, t_new, a.shape[-1])
    return jnp.pad(a, ((0, 0), (0, TQ_PAD - t_new), (0, 0)))


def kernel(x_prompt, x_sample, cache_nsa_cmp_kv, cache_nsa_slc_kv, cache_fox_kv, cache_fox_logf,
           state_nsa_win_kv, state_conv, page_table, c_prompt, c_sample, w_ada, b_ada, norm_g,
           w_ffn_gu, w_ffn_down, w_in, b_in, nsa_cmp_pe, nsa_cmp_wk, nsa_cmp_wv, conv_dw_w, conv_dw_b,
           conv_ln_g, conv_ln_b, w_branch_nsa, w_branch_conv, w_branch_fox, w_out, final_norm_g):
    batch, seq, _ = x_prompt.shape
    n_seq, t_new, _ = x_sample.shape
    depth = w_ada.shape[0]
    n_pool, page = cache_nsa_cmp_kv.shape[1], cache_nsa_cmp_kv.shape[2]
    n_pages = page_table.shape[1]
    past = n_pages * page
    w_buf = state_nsa_win_kv.shape[2]
    mp, ms = batch * seq, n_seq * t_new
    tm_p = min(512, seq)
    tm_ffn = min(1024, seq)
    tm_s = min(256, ms)

    w_ada_b = w_ada.astype(BF16)
    w_gu_b = w_ffn_gu.astype(BF16)
    w_down_b = w_ffn_down.astype(BF16)
    w_in_b, b_in_p, w_mg_b, b_mg = _aligned_in_proj(w_in, b_in)
    pe2, wk_bd, wv_bd = _compress_weights(nsa_cmp_pe, nsa_cmp_wk, nsa_cmp_wv)
    pe_t, wk_t, wv_t = _compress_weights(jnp.swapaxes(nsa_cmp_pe, 1, 2), jnp.swapaxes(nsa_cmp_wk, 1, 2),
                                         jnp.swapaxes(nsa_cmp_wv, 1, 2))
    wb_nsa_b, wb_conv_b = w_branch_nsa.astype(BF16), w_branch_conv.astype(BF16)
    wb_fox_b, w_out_b = w_branch_fox.astype(BF16), w_out.astype(BF16)
    norm_g4 = norm_g[:, :, None, :]
    dw_b3, ln_g3, ln_b3 = conv_dw_b[:, None, :], conv_ln_g[:, None, :], conv_ln_b[:, None, :]

    to_feature_major = (0, 1, 3, 4, 5, 2)
    cmp_pool = jnp.transpose(cache_nsa_cmp_kv, to_feature_major).reshape(depth, n_pool * KV_ROW, page)
    slc_pool = jnp.transpose(cache_nsa_slc_kv, to_feature_major).reshape(depth, n_pool, KV_ROW, page)
    fox_pool = jnp.transpose(cache_fox_kv, to_feature_major).reshape(depth, n_pool, FOX_ROW, page)
    lf_pool_t = jnp.pad(jnp.swapaxes(cache_fox_logf, 2, 3).astype(F32),
                        ((0, 0), (0, 0), (0, 8 - FOX_HEADS), (0, 0)))
    win_state = jnp.transpose(state_nsa_win_kv, to_feature_major).reshape(depth, n_seq, KV_ROW, w_buf)

    cos_p, sin_p = _rope_tables(jnp.arange(seq, dtype=jnp.int32))
    cos_s, sin_s = _rope_tables(past + jnp.arange(t_new, dtype=jnp.int32))
    cos_s, sin_s = jnp.tile(cos_s, (n_seq, 1)), jnp.tile(sin_s, (n_seq, 1))

    mod_all = ada_all_layers(jnp.concatenate([c_prompt, c_sample], axis=0), w_ada_b, b_ada)

    xp = x_prompt.reshape(mp, D_MODEL)
    xs = x_sample.reshape(ms, D_MODEL)
    stacked_p, conv_state_p = (), []
    out_s = [[] for _ in range(6)]

    for l in range(depth):
        mod_p = Mod(mod_all[l, :batch, None, :], seq)
        mod_s = Mod(jnp.repeat(mod_all[l, batch:], t_new, axis=0), 1)

        xp = ffn_half_step(xp, mod_p, 0, norm_g4, w_gu_b, w_down_b, l, 0, tm_ffn)
        xs = ffn_half_step(xs, mod_s, 0, norm_g4, w_gu_b, w_down_b, l, 0, tm_s)

        tm_in = min(512, seq)
        (q_p, kvc_p, kvs_p, kvw_p, glu_p, qf_p, kvf_p, gn_p, lf_p, *new_rows_p) = input_projection(
            xp, mod_p, norm_g4, w_in_b, b_in_p, cos_p, sin_p, l, tm_in, seq // tm_in, feature_major_batch=batch,
            depth=depth, stacked=stacked_p)
        stacked_p = tuple(new_rows_p[:5])
        kvs_pc, kvw_pc, kvf_pc = new_rows_p[5:]
        (q_s, kvc_s, kvs_s, kvw_s, glu_s, qf_s, kvf_s, gn_s, lf_s) = input_projection(
            xs, mod_s, norm_g4, w_in_b, b_in_p, cos_s, sin_s, l, tm_s, ms // tm_s)

        kcmp_p = nsa_compress_rows(kvc_p.reshape(mp * 2, NSA_KV), pe2, wk_bd, wv_bd, l, 128)
        o_nsa_p = nsa_attend_prompt(q_p, gn_p, kcmp_p, kvs_p, kvs_pc, kvw_p, kvw_pc, batch, seq)
        cum_p = cumsum_lanes(stacked_p[4][l])
        cum_q = jnp.swapaxes(cum_p, 1, 2).reshape(mp, 8)
        o_fox_p = fox_attend_prompt(qf_p, kvf_p, kvf_pc, cum_p, cum_q, batch, seq)
        y_conv_p = conv_module_prompt(glu_p, conv_dw_w, dw_b3, ln_g3, ln_b3, l, batch, seq)

        kcmp_pool = nsa_compress_pool(cmp_pool, pe_t, wk_t, wv_t, l, 128)
        o_nsa_s = nsa_attend_decode(page_table, _pad_tokens(q_s, n_seq, t_new), _pad_tokens(gn_s, n_seq, t_new),
                                    _pad_tokens(kvs_s, n_seq, t_new), _pad_tokens(kvw_s, n_seq, t_new),
                                    win_state, kcmp_pool, slc_pool, l, past)
        o_nsa_s = o_nsa_s[:, :t_new].reshape(ms, NSA_Q)
        lf_new_t = jnp.swapaxes(lf_s[:, :FOX_HEADS].reshape(n_seq, t_new, FOX_HEADS), 1, 2)
        lf_new_t = jnp.pad(lf_new_t, ((0, 0), (0, 8 - FOX_HEADS), (0, page - t_new)))
        o_fox_s = fox_attend_decode(page_table, _pad_tokens(qf_s, n_seq, t_new), _pad_tokens(kvf_s, n_seq, t_new),
                                    lf_new_t, fox_pool, lf_pool_t, l, past)
        o_fox_s = o_fox_s[:, :t_new].reshape(ms, FOX_W)
        state_t = jnp.swapaxes(state_conv[l], 0, 1)
        glu_t = jnp.swapaxes(glu_s.reshape(n_seq, t_new, CONV_CH), 0, 1)
        y_conv_s = conv_module_decode(state_t, glu_t, conv_dw_w, dw_b3, ln_g3, ln_b3, l)
        y_conv_s = jnp.swapaxes(y_conv_s, 0, 1).reshape(ms, CONV_CH)

        xp = merge_and_project(xp, mod_p, norm_g4, o_nsa_p, y_conv_p, o_fox_p, w_mg_b, b_mg,
                               wb_nsa_b, wb_conv_b, wb_fox_b, w_out_b, l, tm_p)
        xs = merge_and_project(xs, mod_s, norm_g4, o_nsa_s, y_conv_s, o_fox_s, w_mg_b, b_mg,
                               wb_nsa_b, wb_conv_b, wb_fox_b, w_out_b, l, tm_s)
        xp = ffn_half_step(xp, mod_p, 6, norm_g4, w_gu_b, w_down_b, l, 1, tm_ffn)
        xs = ffn_half_step(xs, mod_s, 6, norm_g4, w_gu_b, w_down_b, l, 1, tm_s)

        kv5 = (2, NSA_KV_HEADS, HEAD_DIM)
        conv_state_p.append(glu_p.reshape(batch, seq, CONV_CH)[:, seq - (CONV_WIDTH - 1):])
        kvw_s5 = kvw_s.reshape((n_seq, t_new) + kv5)
        out_s[0].append(kvc_s.reshape((n_seq, t_new) + kv5))
        out_s[1].append(kvs_s.reshape((n_seq, t_new) + kv5))
        out_s[2].append(kvf_s.reshape(n_seq, t_new, 2, FOX_HEADS, HEAD_DIM))
        out_s[3].append(lf_s[:, :FOX_HEADS].reshape(n_seq, t_new, FOX_HEADS))
        out_s[4].append(kvw_s5)
        out_s[5].append(glu_s.reshape(n_seq, t_new, CONV_CH))

    y_p = final_norm(xp, final_norm_g, tm_p).reshape(batch, seq, D_MODEL)
    y_s = final_norm(xs, final_norm_g, tm_s).reshape(n_seq, t_new, D_MODEL)
    def token_major(a, heads):
        a = a.reshape(depth, batch, 2, heads, HEAD_DIM, a.shape[-1])
        return jnp.transpose(a, (0, 1, 5, 2, 3, 4))

    kvc_all, kvs_all, kvw_all, kvf_all, lf_all = stacked_p
    w_keep = min(NSA_WINDOW, seq)
    out_p = [token_major(kvc_all, NSA_KV_HEADS), token_major(kvs_all, NSA_KV_HEADS),
             token_major(kvf_all, FOX_HEADS), jnp.swapaxes(lf_all[:, :, :FOX_HEADS], 2, 3),
             token_major(kvw_all[:, :, :, seq - w_keep:], NSA_KV_HEADS), jnp.stack(conv_state_p)]
    outs = [y_p, y_s]
    for f in range(6):
        outs.append(out_p[f])
        new_s = jnp.stack(out_s[f])
        if f == 4:
            new_s = jnp.concatenate([state_nsa_win_kv, new_s], axis=2)[:, :, t_new:]
        if f == 5:
            new_s = jnp.concatenate([state_conv, new_s], axis=2)
            new_s = new_s[:, :, new_s.shape[2] - (CONV_WIDTH - 1):]
        outs.append(new_s)
    return tuple(outs)
```
